```python
import math
import jax
import jax.numpy as jnp
from jax import lax
import numpy as np

D_MODEL = 2048
BATCH = 8
SEQ = 2048
DEPTH = 1
DEC_BATCH = 32
DEC_SEQ = 1
PAST_LEN = 8192
PAGE_SIZE = 128

HG_HEADS = 8
HG_DK = 128
HG_DV = 128
HG_CHUNK = 64
NSA_HEADS = 16
NSA_KV_HEADS = 4
NSA_HEAD_DIM = 64
NSA_HPG = NSA_HEADS // NSA_KV_HEADS
CMP_LEN = 32
CMP_STRIDE = 16
CMP_HIDDEN = 128
SLC_BLOCK = 64
N_SEL = 16
N_LOCAL_BLOCKS = 2
WINDOW = 512
WIN_Q_BLOCK = 128
SLC_Q_BLOCK = 32
ATTN_SCALE = NSA_HEAD_DIM ** -0.5
NUM_BUCKETS = 32
MAX_DISTANCE = 128
N_EXPERTS = 32
TOP_K = 4
D_FF = 2048
SWIGLU_ALPHA = 1.702
SWIGLU_LIMIT = 7.0
MOE_BLOCK = 128
MOE_SMALL_BLOCK = 8
RMS_EPS = 1e-5

HG_WIDTH = HG_HEADS * HG_DV
NSA_WIDTH = NSA_HEADS * NSA_HEAD_DIM
MIX_WIDTH = HG_WIDTH + NSA_WIDTH
KV_WIDTH = 2 * NSA_KV_HEADS * NSA_HEAD_DIM
IN_SPLITS = (HG_HEADS * HG_DK, HG_HEADS * HG_DK, HG_WIDTH, HG_WIDTH, NSA_WIDTH, KV_WIDTH, KV_WIDTH, KV_WIDTH, NSA_HEADS * 3)
IN_WIDTH = sum(IN_SPLITS)

kernel_name = 'hgrn2_nsa_moe_hybrid_step'


def rms_norm(x, gain):
    xf = x.astype(jnp.float32)
    y = xf * lax.rsqrt(jnp.mean(xf * xf, axis=-1, keepdims=True) + RMS_EPS)
    return (y * gain.astype(jnp.float32)).astype(x.dtype)


def masked_softmax(s, mask):
    s = jnp.where(mask, s.astype(jnp.float32), -jnp.inf)
    m = jnp.max(s, axis=-1, keepdims=True)
    m = jnp.where(jnp.isfinite(m), m, 0.0)
    e = jnp.exp(s - m)
    return e / jnp.maximum(jnp.sum(e, axis=-1, keepdims=True), 1e-30)


def rel_bucket(dist):
    n = jnp.maximum(dist, 0)
    max_exact = NUM_BUCKETS // 2
    nf = jnp.maximum(n, 1).astype(jnp.float32)
    large = max_exact + (jnp.log(nf / max_exact) / math.log(MAX_DISTANCE / max_exact)
                         * (NUM_BUCKETS - max_exact)).astype(jnp.int32)
    large = jnp.minimum(large, NUM_BUCKETS - 1)
    return jnp.where(n < max_exact, n, large)


def mixer_inputs(hn, w_in):
    B, T, _ = hn.shape
    hq, hf, hi, hg, nq, kvc, kvs, kvw, ng = jnp.split(hn @ w_in, np.cumsum(IN_SPLITS)[:-1].tolist(), axis=-1)
    kv_shape = (B, T, 2, NSA_KV_HEADS, NSA_HEAD_DIM)
    q = nq.reshape(B, T, NSA_KV_HEADS, NSA_HPG, NSA_HEAD_DIM)
    gates = jax.nn.sigmoid(ng.astype(jnp.float32)).reshape(B, T, NSA_KV_HEADS, NSA_HPG, 3)
    return hq, hf, hi, hg, q, kvc.reshape(kv_shape), kvs.reshape(kv_shape), kvw.reshape(kv_shape), gates


def gla_chunked(q, k, v, log_f, S0, chunk):
    B, T, H, DK = q.shape
    DV = v.shape[-1]
    n = T // chunk

    def to_chunks(a):
        return a.reshape(B, n, chunk, H, a.shape[-1]).transpose(1, 0, 3, 2, 4)

    causal = jnp.tril(jnp.ones((chunk, chunk), dtype=bool))

    def step(S, inp):
        qc, kc, vc, gc = inp
        b = jnp.cumsum(gc, axis=2)
        diff = b[:, :, :, None, :] - b[:, :, None, :, :]
        decay = jnp.exp(jnp.where(causal[:, :, None], diff, -jnp.inf))
        attn = jnp.einsum('bhtd,bhsd,bhtsd->bhts', qc, kc, decay)
        o = jnp.einsum('bhts,bhse->bhte', attn, vc) + jnp.einsum('bhtd,bhde->bhte', qc * jnp.exp(b), S)
        b_last = b[:, :, -1, :]
        S = jnp.exp(b_last)[..., None] * S + jnp.einsum(
            'bhsd,bhse->bhde', kc * jnp.exp(b_last[:, :, None, :] - b), vc)
        return S, o

    S, o = lax.scan(step, S0, (to_chunks(q), to_chunks(k), to_chunks(v), to_chunks(log_f)))
    return o.transpose(1, 0, 3, 2, 4).reshape(B, T, H, DV), S


def hgrn2_group(q_raw, f_raw, i_raw, g_raw, lower_bound, norm_gain, S0):
    B, T, _ = q_raw.shape
    f32 = jnp.float32
    q = jax.nn.silu(q_raw.astype(f32)).reshape(B, T, HG_HEADS, HG_DK)
    f = lower_bound + (1.0 - lower_bound) * jax.nn.sigmoid(f_raw.astype(f32))
    f = f.reshape(B, T, HG_HEADS, HG_DK)
    k = 1.0 - f
    v = i_raw.astype(f32).reshape(B, T, HG_HEADS, HG_DV)
    chunk = HG_CHUNK if T % HG_CHUNK == 0 else T
    o, S = gla_chunked(q, k, v, jnp.log(f), S0.astype(f32), chunk)
    o = rms_norm(o, norm_gain) * jax.nn.silu(g_raw.astype(f32)).reshape(B, T, HG_HEADS, HG_DV)
    return o.reshape(B, T, HG_WIDTH).astype(q_raw.dtype), S.astype(q_raw.dtype)


def compress_kv(kv, pe, w1, w2):
    B, L = kv.shape[:2]
    r = CMP_LEN // CMP_STRIDE
    ns = L // CMP_STRIDE
    nc = ns - r + 1
    strides = kv[:, :ns * CMP_STRIDE].reshape(B, ns, CMP_STRIDE, 2, NSA_KV_HEADS, NSA_HEAD_DIM)
    w1r = w1.reshape(2, r, CMP_STRIDE, NSA_HEAD_DIM, CMP_HIDDEN)
    per = jnp.einsum('bnscgd,cmsde->bnmcge', strides, w1r)
    pe_term = jnp.einsum('cmsd,cmsde->ce', pe.reshape(2, r, CMP_STRIDE, NSA_HEAD_DIM), w1r)
    hid = pe_term[None, None, :, None, :]
    for m in range(r):
        hid = hid + per[:, m:m + nc, m]
    return jnp.einsum('bncge,ced->bncgd', jax.nn.silu(hid), w2)


def cmp_branch(q, q_pos, kv_seq, pe, w1, w2, rel_bias):
    Tq = q.shape[1]
    kvc = compress_kv(kv_seq, pe, w1, w2)
    kc, vc = kvc[:, :, 0], kvc[:, :, 1]
    nc = kc.shape[1]
    s = jnp.einsum('bqghd,bngd->bqghn', q, kc).astype(jnp.float32) * ATTN_SCALE
    k_end = jnp.arange(nc) * CMP_STRIDE + CMP_LEN - 1
    dist = q_pos[:, None] - k_end[None, :]
    bias = rel_bias[:, rel_bucket(dist)].astype(jnp.float32)
    bias = bias.reshape(NSA_KV_HEADS, NSA_HPG, Tq, nc).transpose(2, 0, 1, 3)
    p = masked_softmax(s + bias, (dist >= 0)[:, None, None, :])
    o = jnp.einsum('bqghn,bngd->bqghd', p.astype(vc.dtype), vc)
    return o, p


def select_blocks(p_cmp, q_pos, n_blk):
    pg = p_cmp.sum(axis=3)
    nc = pg.shape[-1]
    ratio = SLC_BLOCK // CMP_STRIDE
    r = CMP_LEN // CMP_STRIDE
    ns = n_blk * ratio
    pp = jnp.pad(pg, ((0, 0), (0, 0), (0, 0), (0, ns - nc)))
    per_stride = pp
    for n in range(1, r):
        per_stride = per_stride + jnp.pad(pp, ((0, 0), (0, 0), (0, 0), (n, 0)))[..., :ns]
    p_slc = per_stride.reshape(pg.shape[:3] + (n_blk, ratio)).sum(-1)
    j = jnp.arange(n_blk)[None, :]
    cur = (q_pos // SLC_BLOCK)[:, None]
    forced = (j == 0) | ((j <= cur) & (j > cur - N_LOCAL_BLOCKS))
    future = j > cur
    score = jnp.where(forced[None, :, None, :], jnp.inf,
                      jnp.where(future[None, :, None, :], -jnp.inf, p_slc))
    _, idx = lax.top_k(score, min(N_SEL, n_blk))
    return idx.astype(jnp.int32)


def block_positions(idx):
    pos = idx[..., None] * SLC_BLOCK + jnp.arange(SLC_BLOCK)
    return pos.reshape(idx.shape[:-1] + (-1,))


def slc_attend(q, q_pos, k_sel, v_sel, k_pos, rel_bias):
    s = jnp.einsum('bqghd,bqgkd->bqghk', q, k_sel).astype(jnp.float32) * ATTN_SCALE
    dist = q_pos[None, :, None, None] - k_pos
    tbl = rel_bias.reshape(NSA_KV_HEADS, NSA_HPG, NUM_BUCKETS).transpose(0, 2, 1)
    g_i = jnp.arange(NSA_KV_HEADS)[None, None, :, None]
    bias = jnp.moveaxis(tbl[g_i, rel_bucket(dist)], -1, 3).astype(jnp.float32)
    p = masked_softmax(s + bias, (dist >= 0)[:, :, :, None, :])
    return jnp.einsum('bqghk,bqgkd->bqghd', p.astype(v_sel.dtype), v_sel)


def win_attend(q, q_pos, k, v, k_pos, rel_bias):
    Tq, Tk = q.shape[1], k.shape[1]
    s = jnp.einsum('bqghd,bkgd->bqghk', q, k).astype(jnp.float32) * ATTN_SCALE
    dist = q_pos[:, None] - k_pos[None, :]
    valid = (dist >= 0) & (dist < WINDOW) & (k_pos[None, :] >= 0)
    bias = rel_bias[:, rel_bucket(dist)].astype(jnp.float32)
    bias = bias.reshape(NSA_KV_HEADS, NSA_HPG, Tq, Tk).transpose(2, 0, 1, 3)
    p = masked_softmax(s + bias[None], valid[None, :, None, None, :])
    return jnp.einsum('bqghk,bkgd->bqghd', p.astype(v.dtype), v)


def merge_branches(gates, o_cmp, o_slc, o_win, dtype):
    o = gates[..., 0:1] * o_cmp + gates[..., 1:2] * o_slc + gates[..., 2:3] * o_win
    B, T = o.shape[:2]
    return o.reshape(B, T, NSA_WIDTH).astype(dtype)


def nsa_prompt(q, kv_cmp, kv_slc, kv_win, gates, pe, w1, w2, rel_bias):
    B, T = q.shape[:2]
    q_pos = jnp.arange(T)
    o_cmp, p_cmp = cmp_branch(q, q_pos, kv_cmp, pe, w1, w2, rel_bias)
    n_blk = T // SLC_BLOCK
    idx = select_blocks(p_cmp, q_pos, n_blk)
    n_sel = idx.shape[-1]
    kvb = kv_slc.reshape(B, n_blk, SLC_BLOCK, 2, NSA_KV_HEADS, NSA_HEAD_DIM).transpose(0, 4, 1, 2, 3, 5)
    b_i = jnp.arange(B)[:, None, None, None]
    g_i = jnp.arange(NSA_KV_HEADS)[None, None, :, None]

    def slc_block(c):
        start = c * SLC_Q_BLOCK
        q_c = lax.dynamic_slice_in_dim(q, start, SLC_Q_BLOCK, axis=1)
        idx_c = lax.dynamic_slice_in_dim(idx, start, SLC_Q_BLOCK, axis=1)
        sel = kvb[b_i, g_i, idx_c].reshape(B, SLC_Q_BLOCK, NSA_KV_HEADS, n_sel * SLC_BLOCK, 2, NSA_HEAD_DIM)
        return slc_attend(q_c, start + jnp.arange(SLC_Q_BLOCK), sel[..., 0, :], sel[..., 1, :],
                          block_positions(idx_c), rel_bias)

    o_slc = jnp.moveaxis(lax.map(slc_block, jnp.arange(T // SLC_Q_BLOCK)), 0, 1).reshape(q.shape)
    kv_pad = jnp.pad(kv_win, ((0, 0), (WINDOW, 0), (0, 0), (0, 0), (0, 0)))

    def win_block(c):
        start = c * WIN_Q_BLOCK
        q_c = lax.dynamic_slice_in_dim(q, start, WIN_Q_BLOCK, axis=1)
        kv_c = lax.dynamic_slice_in_dim(kv_pad, start, WIN_Q_BLOCK + WINDOW, axis=1)
        k_pos = start - WINDOW + jnp.arange(WIN_Q_BLOCK + WINDOW)
        return win_attend(q_c, start + jnp.arange(WIN_Q_BLOCK), kv_c[:, :, 0], kv_c[:, :, 1], k_pos, rel_bias)

    o_win = jnp.moveaxis(lax.map(win_block, jnp.arange(T // WIN_Q_BLOCK)), 0, 1).reshape(q.shape)
    return merge_branches(gates, o_cmp, o_slc, o_win, q.dtype)


def nsa_sample(q, kv_cmp_new, kv_slc_new, kv_win_new, gates, pool_cmp, pool_slc, win_buf, page_table,
               pe, w1, w2, rel_bias):
    B, Tn = q.shape[:2]
    n_pages = page_table.shape[1]
    ps = pool_cmp.shape[1]
    past = n_pages * ps
    L = past + Tn
    q_pos = past + jnp.arange(Tn)
    past_cmp = pool_cmp[page_table].reshape(B, past, 2, NSA_KV_HEADS, NSA_HEAD_DIM)
    kv_cmp_seq = jnp.concatenate([past_cmp, kv_cmp_new], axis=1)
    o_cmp, p_cmp = cmp_branch(q, q_pos, kv_cmp_seq, pe, w1, w2, rel_bias)
    idx = select_blocks(p_cmp, q_pos, -(-L // SLC_BLOCK))
    k_pos = block_positions(idx)
    b_i = jnp.arange(B)[:, None, None, None]
    g_i = jnp.arange(NSA_KV_HEADS)[None, None, :, None]
    page = page_table[b_i, jnp.clip(k_pos // ps, 0, n_pages - 1)]
    from_pool = pool_slc[page, k_pos % ps, :, g_i]
    from_new = kv_slc_new[b_i, jnp.clip(k_pos - past, 0, Tn - 1), :, g_i]
    sel = jnp.where((k_pos < past)[..., None, None], from_pool, from_new)
    o_slc = slc_attend(q, q_pos, sel[..., 0, :], sel[..., 1, :], k_pos, rel_bias)
    kv_w = jnp.concatenate([win_buf.astype(kv_win_new.dtype), kv_win_new], axis=1)
    wb = win_buf.shape[1]
    k_pos_w = past - wb + jnp.arange(wb + Tn)
    o_win = win_attend(q, q_pos, kv_w[:, :, 0], kv_w[:, :, 1], k_pos_w, rel_bias)
    new_win = kv_w[:, kv_w.shape[1] - min(WINDOW, kv_w.shape[1]):]
    return merge_branches(gates, o_cmp, o_slc, o_win, q.dtype), new_win


def moe_ffn(x, router_w, router_b, w1, b1, w2, b2):
    shp = x.shape
    D = shp[-1]
    xt = x.reshape(-1, D)
    N = xt.shape[0]
    logits = (xt @ router_w).astype(jnp.float32) + router_b.astype(jnp.float32)
    top_v, top_e = lax.top_k(logits, TOP_K)
    gate = jax.nn.softmax(top_v, axis=-1)
    n_assign = N * TOP_K
    blk = MOE_BLOCK if n_assign >= 4 * MOE_BLOCK * N_EXPERTS else MOE_SMALL_BLOCK
    flat_e = top_e.reshape(-1).astype(jnp.int32)
    order = jnp.argsort(flat_e)
    sorted_e = flat_e[order]
    sorted_tok = (order // TOP_K).astype(jnp.int32)
    sorted_gate = gate.reshape(-1)[order]
    counts = jax.ops.segment_sum(jnp.ones_like(flat_e), flat_e, num_segments=N_EXPERTS)
    starts = jnp.cumsum(counts) - counts
    padded = (counts + blk - 1) // blk * blk
    pad_end = jnp.cumsum(padded)
    slot = pad_end[sorted_e] - padded[sorted_e] + jnp.arange(n_assign) - starts[sorted_e]
    n_blocks = -(-n_assign // blk) + N_EXPERTS
    slot_tok = jnp.full((n_blocks * blk,), N, jnp.int32).at[slot].set(sorted_tok)
    slot_gate = jnp.zeros((n_blocks * blk,), jnp.float32).at[slot].set(sorted_gate)
    block_e = jnp.minimum(jnp.searchsorted(pad_end, jnp.arange(n_blocks) * blk, side='right'), N_EXPERTS - 1)
    x_ext = jnp.concatenate([xt, jnp.zeros((1, D), xt.dtype)], axis=0)

    def expert_block(args):
        tok, e = args
        h = x_ext[tok] @ w1[e] + b1[e]
        g = jnp.minimum(h[:, :D_FF], SWIGLU_LIMIT)
        up = jnp.clip(h[:, D_FF:], -SWIGLU_LIMIT, SWIGLU_LIMIT)
        act = (up + 1.0) * g * jax.nn.sigmoid(SWIGLU_ALPHA * g)
        return act @ w2[e] + b2[e]

    out = lax.map(expert_block, (slot_tok.reshape(n_blocks, blk), block_e))
    y = jnp.zeros((N + 1, D), x.dtype).at[slot_tok].add(
        out.reshape(-1, D) * slot_gate[:, None].astype(x.dtype))
    return y[:N].reshape(shp)


def setup_inputs(seed: int = 0) -> dict:
    key = jax.random.key(seed)
    ks = jax.random.split(key, 26)
    f32 = jnp.float32
    n_pages = PAST_LEN // PAGE_SIZE
    n_used = DEC_BATCH * n_pages
    n_pool = (n_used * 5 + 3) // 4
    win_len = min(WINDOW, PAST_LEN)

    def nrm(k, shape, scale):
        return scale * jax.random.normal(k, shape, f32)

    kvh, dh = NSA_KV_HEADS, NSA_HEAD_DIM
    return {
        'x_prompt': nrm(ks[0], (BATCH, SEQ, D_MODEL), 1.0),
        'x_sample': nrm(ks[1], (DEC_BATCH, DEC_SEQ, D_MODEL), 1.0),
        'cache_kv_cmp': nrm(ks[2], (DEPTH, n_pool, PAGE_SIZE, 2, kvh, dh), 1.0),
        'cache_kv_slc': nrm(ks[3], (DEPTH, n_pool, PAGE_SIZE, 2, kvh, dh), 1.0),
        'state_win_kv': nrm(ks[4], (DEPTH, DEC_BATCH, win_len, 2, kvh, dh), 1.0),
        'state_hgrn': nrm(ks[5], (DEPTH, DEC_BATCH, HG_HEADS, HG_DK, HG_DV), 0.5),
        'page_table': jax.random.permutation(ks[6], n_pool)[:n_used].reshape(DEC_BATCH, n_pages).astype(jnp.int32),
        'norm1': 1.0 + nrm(ks[7], (DEPTH, D_MODEL), 0.02),
        'w_in': nrm(ks[8], (DEPTH, D_MODEL, IN_WIDTH), D_MODEL ** -0.5),
        'hg_lower_bound': nrm(ks[9], (DEPTH + 1, HG_HEADS * HG_DK), 0.5),
        'hg_norm': 1.0 + nrm(ks[10], (DEPTH, HG_DV), 0.02),
        'cmp_pe': nrm(ks[11], (DEPTH, 2, CMP_LEN, dh), 0.1),
        'cmp_w1': nrm(ks[12], (DEPTH, 2, CMP_LEN * dh, CMP_HIDDEN), (CMP_LEN * dh) ** -0.5),
        'cmp_w2': nrm(ks[13], (DEPTH, 2, CMP_HIDDEN, dh), CMP_HIDDEN ** -0.5),
        'rel_bias': nrm(ks[14], (NSA_HEADS, NUM_BUCKETS), 0.5),
        'w_out': nrm(ks[15], (DEPTH, MIX_WIDTH, D_MODEL), MIX_WIDTH ** -0.5),
        'norm2': 1.0 + nrm(ks[16], (DEPTH, D_MODEL), 0.02),
        'router_w': nrm(ks[17], (DEPTH, D_MODEL, N_EXPERTS), D_MODEL ** -0.5),
        'router_b': nrm(ks[18], (DEPTH, N_EXPERTS), 0.01),
        'moe_w1': nrm(ks[19], (DEPTH, N_EXPERTS, D_MODEL, 2 * D_FF), D_MODEL ** -0.5),
        'moe_b1': nrm(ks[20], (DEPTH, N_EXPERTS, 2 * D_FF), 0.01),
        'moe_w2': nrm(ks[21], (DEPTH, N_EXPERTS, D_FF, D_MODEL), D_FF ** -0.5),
        'moe_b2': nrm(ks[22], (DEPTH, N_EXPERTS, D_MODEL), 0.01),
        'norm_f': 1.0 + nrm(ks[23], (D_MODEL,), 0.02),
    }


def reference(x_prompt, x_sample, cache_kv_cmp, cache_kv_slc, state_win_kv, state_hgrn, page_table,
              norm1, w_in, hg_lower_bound, hg_norm, cmp_pe, cmp_w1, cmp_w2, rel_bias,
              w_out, norm2, router_w, router_b, moe_w1, moe_b1, moe_w2, moe_b2, norm_f):
    lower_bounds = jnp.cumsum(jax.nn.softmax(hg_lower_bound.astype(jnp.float32), axis=0), axis=0)
    b_p = x_prompt.shape[0]
    xp, xs = x_prompt, x_sample
    cmp_p, slc_p, win_p, hg_p, cmp_s, slc_s, win_s, hg_s = ([] for _ in range(8))
    for l in range(DEPTH):
        hq, hf, hi, hg, q, kvc, kvs, kvw, gates = mixer_inputs(rms_norm(xp, norm1[l]), w_in[l])
        o_hg, s_new = hgrn2_group(hq, hf, hi, hg, lower_bounds[l], hg_norm[l],
                                  jnp.zeros((b_p, HG_HEADS, HG_DK, HG_DV), jnp.float32))
        o_nsa = nsa_prompt(q, kvc, kvs, kvw, gates, cmp_pe[l], cmp_w1[l], cmp_w2[l], rel_bias)
        xp = xp + jnp.concatenate([o_hg, o_nsa], axis=-1) @ w_out[l]
        xp = xp + moe_ffn(rms_norm(xp, norm2[l]), router_w[l], router_b[l],
                          moe_w1[l], moe_b1[l], moe_w2[l], moe_b2[l])
        cmp_p.append(kvc)
        slc_p.append(kvs)
        win_p.append(kvw[:, kvw.shape[1] - min(WINDOW, kvw.shape[1]):])
        hg_p.append(s_new)
        hq, hf, hi, hg, q, kvc, kvs, kvw, gates = mixer_inputs(rms_norm(xs, norm1[l]), w_in[l])
        o_hg, s_new = hgrn2_group(hq, hf, hi, hg, lower_bounds[l], hg_norm[l], state_hgrn[l])
        o_nsa, win_new = nsa_sample(q, kvc, kvs, kvw, gates, cache_kv_cmp[l], cache_kv_slc[l], state_win_kv[l],
                                    page_table, cmp_pe[l], cmp_w1[l], cmp_w2[l], rel_bias)
        xs = xs + jnp.concatenate([o_hg, o_nsa], axis=-1) @ w_out[l]
        xs = xs + moe_ffn(rms_norm(xs, norm2[l]), router_w[l], router_b[l],
                          moe_w1[l], moe_b1[l], moe_w2[l], moe_b2[l])
        cmp_s.append(kvc)
        slc_s.append(kvs)
        win_s.append(win_new)
        hg_s.append(s_new)
    y_prompt = rms_norm(xp, norm_f)
    y_sample = rms_norm(xs, norm_f)
    return (y_prompt, y_sample,
            jnp.stack(cmp_p), jnp.stack(slc_p), jnp.stack(win_p), jnp.stack(hg_p),
            jnp.stack(cmp_s), jnp.stack(slc_s), jnp.stack(win_s), jnp.stack(hg_s))
```

```python
import functools
import math

import jax
import jax.numpy as jnp
import numpy as np
from jax import lax
from jax.experimental import pallas as pl
from jax.experimental.pallas import tpu as pltpu

f32 = jnp.float32
bf16 = jnp.bfloat16

HG_HEADS, HG_DK, HG_DV = 8, 128, 128
HG_STEP = 16
NSA_HEADS, KVH, DH = 16, 4, 64
HPG = NSA_HEADS // KVH
CMP_LEN, CMP_STRIDE, CMP_HIDDEN = 32, 16, 128
SLC_BLOCK, N_SEL, N_LOCAL_BLOCKS, WINDOW = 64, 16, 2, 512
ATTN_SCALE = DH ** -0.5
NUM_BUCKETS, MAX_DISTANCE = 32, 128
N_EXPERTS, TOP_K, D_FF = 32, 4, 2048
SWIGLU_ALPHA, SWIGLU_LIMIT = 1.702, 7.0
RMS_EPS = 1e-5

HG_WIDTH = HG_HEADS * HG_DV
NSA_WIDTH = NSA_HEADS * DH
KV_WIDTH = 2 * KVH * DH
IN_SPLITS = (HG_WIDTH, HG_WIDTH, HG_WIDTH, HG_WIDTH, NSA_WIDTH, KV_WIDTH, KV_WIDTH, KV_WIDTH, NSA_HEADS * 3)
IN_WIDTH = sum(IN_SPLITS)
Z_WIDTH = 7168
COL_Q, COL_KVC, COL_KVS, COL_KVW, COL_GATE = 4096, 5120, 5632, 6144, 6656
LANE = 128
NEG = -1e30

V7X_VMEM_LIMIT = 56 * 1024 * 1024


def _cparams(sem, vmem=None):
    return pltpu.CompilerParams(dimension_semantics=sem, vmem_limit_bytes=vmem)


def _sigmoid(x):
    return 1.0 / (1.0 + jnp.exp(-x))


def _silu(x):
    return x * _sigmoid(x)


def _proj_kernel(x_ref, g_ref, w_ref, z_ref, hn_ref):
    @pl.when(pl.program_id(1) == 0)
    def _():
        x = x_ref[...]
        y = x * lax.rsqrt(jnp.mean(x * x, axis=-1, keepdims=True) + RMS_EPS) * g_ref[...]
        hn_ref[...] = y.astype(bf16)

    z_ref[...] = jnp.dot(hn_ref[...], w_ref[...], preferred_element_type=f32)


def _in_proj(x, gain, w_bf16, tm):
    n, d = x.shape
    tn = Z_WIDTH // 4
    return pl.pallas_call(
        _proj_kernel,
        grid=(n // tm, Z_WIDTH // tn),
        in_specs=[pl.BlockSpec((tm, d), lambda i, j: (i, 0)),
                  pl.BlockSpec((1, d), lambda i, j: (0, 0)),
                  pl.BlockSpec((d, tn), lambda i, j: (0, j))],
        out_specs=pl.BlockSpec((tm, tn), lambda i, j: (i, j)),
        out_shape=jax.ShapeDtypeStruct((n, Z_WIDTH), f32),
        scratch_shapes=[pltpu.VMEM((tm, d), bf16)],
        compiler_params=_cparams(("parallel", "arbitrary"), V7X_VMEM_LIMIT),
        name="in_proj",
    )(x, gain.reshape(1, d), w_bf16)


def _hgrn_gates(q_raw, f_raw, lb):
    q = _silu(q_raw)
    f = lb + (1.0 - lb) * _sigmoid(f_raw)
    return q, 1.0 - f, jnp.log(f)


def _hgrn_out(o, gain, g_raw):
    y = o * lax.rsqrt(jnp.mean(o * o, axis=-1, keepdims=True) + RMS_EPS) * gain
    return y * _silu(g_raw)


def _hgrn_prompt_kernel(q_ref, f_ref, i_ref, g_ref, lb_ref, gain_ref, o_ref, s_ref, st_ref):
    tb = pl.program_id(1)
    n_steps = q_ref.shape[0] // HG_STEP

    @pl.when(tb == 0)
    def _():
        st_ref[...] = jnp.zeros_like(st_ref)

    row = lax.broadcasted_iota(jnp.int32, (HG_STEP, HG_DK), 0)
    gain = gain_ref[...]

    def step(c, carry):
        r0 = pl.multiple_of(c * HG_STEP, HG_STEP)
        for h in range(HG_HEADS):
            cs = slice(h * HG_DK, (h + 1) * HG_DK)
            q, k, g = _hgrn_gates(q_ref[pl.ds(r0, HG_STEP), cs], f_ref[pl.ds(r0, HG_STEP), cs], lb_ref[:, cs])
            v = i_ref[pl.ds(r0, HG_STEP), cs]
            b = g
            for sh in (1, 2, 4, 8):
                b = b + jnp.where(row >= sh, pltpu.roll(b, sh, 0), 0.0)
            b_last = b[HG_STEP - 1:HG_STEP, :]
            st = st_ref[h]
            o = lax.dot_general((q * jnp.exp(b)).astype(bf16), st.astype(bf16),
                                (((1,), (1,)), ((), ())), preferred_element_type=f32)
            for s in range(HG_STEP):
                p = q * k[s:s + 1, :] * jnp.exp(b - b[s:s + 1, :])
                w = jnp.sum(jnp.where(row >= s, p, 0.0), axis=-1, keepdims=True)
                o = o + w * v[s:s + 1, :]
            kd = k * jnp.exp(b_last - b)
            st_ref[h] = jnp.exp(b_last) * st + lax.dot_general(
                v.astype(bf16), kd.astype(bf16), (((0,), (0,)), ((), ())), preferred_element_type=f32)
            o_ref[pl.ds(r0, HG_STEP), cs] = _hgrn_out(o, gain, g_ref[pl.ds(r0, HG_STEP), cs])
        return carry

    lax.fori_loop(0, n_steps, step, 0)

    @pl.when(tb == pl.num_programs(1) - 1)
    def _():
        for h in range(HG_HEADS):
            s_ref[h] = st_ref[h].T


def _hgrn_prompt(z, lb, gain, batch, seq, tt):
    nt = seq // tt
    blk = lambda seg: pl.BlockSpec((tt, HG_WIDTH), lambda b, t, seg=seg: (b * nt + t, seg))
    return pl.pallas_call(
        _hgrn_prompt_kernel,
        grid=(batch, nt),
        in_specs=[blk(0), blk(1), blk(2), blk(3),
                  pl.BlockSpec((1, HG_WIDTH), lambda b, t: (0, 0)),
                  pl.BlockSpec((1, HG_DV), lambda b, t: (0, 0))],
        out_specs=[pl.BlockSpec((tt, HG_WIDTH), lambda b, t: (b * nt + t, 0)),
                   pl.BlockSpec((None, HG_HEADS, HG_DK, HG_DV), lambda b, t: (b, 0, 0, 0))],
        out_shape=[jax.ShapeDtypeStruct((batch * seq, HG_WIDTH), f32),
                   jax.ShapeDtypeStruct((batch, HG_HEADS, HG_DK, HG_DV), f32)],
        scratch_shapes=[pltpu.VMEM((HG_HEADS, HG_DV, HG_DK), f32)],
        compiler_params=_cparams(("parallel", "arbitrary")),
        name="hgrn_prompt",
    )(z, z, z, z, lb.reshape(1, HG_WIDTH), gain.reshape(1, HG_DV))


def _bucket_table(max_dist):
    n = np.arange(max_dist + 1)
    max_exact = NUM_BUCKETS // 2

    def large(dtype):
        nf = np.maximum(n, 1).astype(dtype)
        v = np.log(nf / dtype(max_exact)) / dtype(math.log(MAX_DISTANCE / max_exact)) * dtype(NUM_BUCKETS - max_exact)
        return np.minimum(max_exact + v.astype(np.int32), NUM_BUCKETS - 1)

    lo, hi = large(np.float32), large(np.float64)
    assert (lo == hi).all(), "bucket boundaries must not depend on float rounding"
    return np.where(n < max_exact, n, lo).astype(np.int32)


def _bias_by_dist(rel_bias, dist):
    dist = np.maximum(np.asarray(dist), 0)
    table = _bucket_table(int(dist.max()))
    return jnp.take(rel_bias.astype(f32), jnp.asarray(table[dist]), axis=1)


def _split3(x):
    hi = x.astype(bf16)
    r1 = x - hi.astype(f32)
    mid = r1.astype(bf16)
    lo = (r1 - mid.astype(f32)).astype(bf16)
    return hi, mid, lo


def _dot_nt(a, b):
    return lax.dot_general(a, b, (((1,), (1,)), ((), ())), preferred_element_type=f32)


def _q_pad(q_ref_or_val, g, rows):
    q = q_ref_or_val
    z = jnp.zeros((rows, DH), f32)
    parts = []
    for h in range(HPG):
        c0 = (g * HPG + h) * DH
        qh = q[:, c0:c0 + DH]
        parts.append(jnp.concatenate([qh, z] if g % 2 == 0 else [z, qh], axis=1))
    return jnp.concatenate(parts, axis=0).astype(bf16)


def _compress_kernel(pt_ref, page_ref, pe_ref, w1_ref, w1bd_ref, w2bd_ref, o_ref, xbuf):
    p = pl.program_id(1)
    spp = page_ref.shape[0]
    xbuf[pl.ds(pl.multiple_of(p * spp, spp), spp)] = page_ref[...]

    @pl.when(p == pl.num_programs(1) - 1)
    def _():
        ns = xbuf.shape[0]
        for c in range(2):
            pe_term = jnp.dot(pe_ref[c].astype(bf16), w1_ref[c], preferred_element_type=f32)
            pe_pair = jnp.concatenate([pe_term, pe_term], axis=1)
            for pr in range(KVH // 2):
                lanes = slice(c * KVH * DH + pr * LANE, c * KVH * DH + (pr + 1) * LANE)
                acc = jnp.zeros((ns, 4 * CMP_HIDDEN), f32)
                for s in range(CMP_STRIDE):
                    acc = acc + jnp.dot(xbuf[:, s, lanes].astype(bf16), w1bd_ref[c, s], preferred_element_type=f32)
                hid = pe_pair + acc[:, :2 * CMP_HIDDEN] + pltpu.roll(acc[:, 2 * CMP_HIDDEN:], ns - 1, 0)
                o_ref[:, lanes] = jnp.dot(_silu(hid).astype(bf16), w2bd_ref[c], preferred_element_type=f32).astype(bf16)


def _compress_weights(cmp_pe, cmp_w1, cmp_w2):
    r = CMP_LEN // CMP_STRIDE
    w1r = cmp_w1.reshape(2, r, CMP_STRIDE, DH, CMP_HIDDEN)
    zero = jnp.zeros_like(w1r[:, 0])
    top = jnp.concatenate([w1r[:, 0], zero, w1r[:, 1], zero], axis=-1)
    bot = jnp.concatenate([zero, w1r[:, 0], zero, w1r[:, 1]], axis=-1)
    w1bd = jnp.concatenate([top, bot], axis=2).astype(bf16)
    z2 = jnp.zeros_like(cmp_w2)
    w2bd = jnp.concatenate([jnp.concatenate([cmp_w2, z2], axis=-1),
                            jnp.concatenate([z2, cmp_w2], axis=-1)], axis=1).astype(bf16)
    pe = cmp_pe.reshape(2, 1, CMP_LEN * DH)
    return pe, cmp_w1.astype(bf16), w1bd, w2bd


def _compress(pool, page_table, page_rows, cmp_pe, cmp_w1, cmp_w2):
    batch, n_pages = page_table.shape
    spp = page_rows // CMP_STRIDE
    ns = n_pages * spp
    pool4 = pool.reshape(pool.shape[0], spp, CMP_STRIDE, KV_WIDTH)
    pe, w1, w1bd, w2bd = _compress_weights(cmp_pe, cmp_w1, cmp_w2)
    full = lambda a: pl.BlockSpec(a.shape, lambda b, p, pt: (0,) * a.ndim)
    return pl.pallas_call(
        _compress_kernel,
        grid_spec=pltpu.PrefetchScalarGridSpec(
            num_scalar_prefetch=1,
            grid=(batch, n_pages),
            in_specs=[pl.BlockSpec((None, spp, CMP_STRIDE, KV_WIDTH), lambda b, p, pt: (pt[b * n_pages + p], 0, 0, 0)),
                      full(pe), full(w1), full(w1bd), full(w2bd)],
            out_specs=pl.BlockSpec((None, ns, KV_WIDTH), lambda b, p, pt: (b, 0, 0)),
            scratch_shapes=[pltpu.VMEM((ns, CMP_STRIDE, KV_WIDTH), f32)]),
        out_shape=jax.ShapeDtypeStruct((batch, ns, KV_WIDTH), bf16),
        compiler_params=_cparams(("parallel", "arbitrary"), V7X_VMEM_LIMIT),
        name="nsa_compress",
    )(page_table.reshape(-1).astype(jnp.int32), pool4, pe, w1, w1bd, w2bd)


def _stride_to_block_map(ns, n_lanes, lane0):
    ratio = SLC_BLOCK // CMP_STRIDE
    m = np.zeros((ns, n_lanes), np.float32)
    for n in range(ns - (CMP_LEN // CMP_STRIDE - 1)):
        for st in range(n, n + CMP_LEN // CMP_STRIDE):
            if lane0 + st // ratio < n_lanes:
                m[n, lane0 + st // ratio] += 1.0
    return m


def _rank_select(score, jidx, n_keep):
    rank = jnp.zeros(score.shape, f32)
    for jp in range(score.shape[0]):
        row = score[jp:jp + 1, :]
        ahead = (row > score) | ((row == score) & (jidx > jp))
        rank = rank + jnp.where(ahead, 1.0, 0.0)
    return rank < n_keep


def _cmp_prompt_kernel(q_ref, gate_ref, kc_ref, bias_ref, map_ref, o_ref, sel_ref):
    i = pl.program_id(0)
    tq = q_ref.shape[0]
    ns = kc_ref.shape[0]
    nc = ns - (CMP_LEN // CMP_STRIDE - 1)
    q = q_ref[...]
    gates = _sigmoid(gate_ref[...])
    t_glob = i * tq + lax.broadcasted_iota(jnp.int32, (tq, ns), 0)
    n_idx = lax.broadcasted_iota(jnp.int32, (tq, ns), 1)
    valid1 = (t_glob >= n_idx * CMP_STRIDE + (CMP_LEN - 1)) & (n_idx < nc)
    valid = jnp.concatenate([valid1] * HPG, axis=0)
    pieces = []
    p_slc = jnp.zeros((tq, LANE), f32)
    for g in range(KVH):
        pair = slice((g // 2) * LANE, (g // 2 + 1) * LANE)
        vpair = slice(KVH * DH + (g // 2) * LANE, KVH * DH + (g // 2 + 1) * LANE)
        s = _dot_nt(_q_pad(q, g, tq), kc_ref[:, pair]) * ATTN_SCALE + bias_ref[g]
        s = jnp.where(valid, s, NEG)
        m = jnp.max(s, axis=-1, keepdims=True)
        e = jnp.where(valid, jnp.exp(s - m), 0.0)
        p = e / jnp.maximum(jnp.sum(e, axis=-1, keepdims=True), 1e-30)
        o = jnp.dot(p.astype(bf16), kc_ref[:, vpair], preferred_element_type=f32)
        pg = p[0:tq]
        for h in range(HPG):
            col = (g * HPG + h) * 3
            oh = o[h * tq:(h + 1) * tq, (g % 2) * DH:(g % 2 + 1) * DH]
            pieces.append(gates[:, col:col + 1] * oh)
            if h:
                pg = pg + p[h * tq:(h + 1) * tq]
        for part in _split3(pg):
            p_slc = p_slc + jnp.dot(part, map_ref[g], preferred_element_type=f32)
    o_ref[...] = jnp.concatenate(pieces, axis=1)
    pt = p_slc.T
    n_blk_lanes = LANE // KVH
    jidx = lax.broadcasted_iota(jnp.int32, (n_blk_lanes, tq), 0)
    cur = (i * tq + lax.broadcasted_iota(jnp.int32, (n_blk_lanes, tq), 1)) // SLC_BLOCK
    forced = (jidx == 0) | ((jidx <= cur) & (jidx > cur - N_LOCAL_BLOCKS))
    sels = []
    for g in range(KVH):
        sc = pt[g * n_blk_lanes:(g + 1) * n_blk_lanes]
        sc = jnp.where(forced, jnp.inf, jnp.where(jidx > cur, -jnp.inf, sc))
        keep = _rank_select(sc, jidx, N_SEL) & (jidx <= cur)
        sels.append(jnp.where(keep, 1.0, 0.0))
    sel_ref[...] = jnp.concatenate(sels, axis=0).T


def _cmp_prompt(z, kc, rel_bias, batch, seq, tq):
    nt = seq // tq
    ns = kc.shape[1]
    n_blk = seq // SLC_BLOCK
    n_blk_lanes = LANE // KVH
    assert n_blk <= n_blk_lanes
    t = np.arange(seq)[:, None]
    dist = t - (np.arange(ns)[None, :] * CMP_STRIDE + CMP_LEN - 1)
    bias = _bias_by_dist(rel_bias, dist)
    bias = bias.reshape(KVH, HPG, nt, tq, ns).transpose(2, 0, 1, 3, 4).reshape(nt, KVH, HPG * tq, ns)
    smap = jnp.asarray(np.stack([_stride_to_block_map(ns, LANE, g * n_blk_lanes) for g in range(KVH)])).astype(bf16)
    nq = COL_Q // NSA_WIDTH
    return pl.pallas_call(
        _cmp_prompt_kernel,
        grid=(nt, batch),
        in_specs=[pl.BlockSpec((tq, NSA_WIDTH), lambda i, b: (b * nt + i, nq)),
                  pl.BlockSpec((tq, LANE), lambda i, b: (b * nt + i, COL_GATE // LANE)),
                  pl.BlockSpec((None, ns, KV_WIDTH), lambda i, b: (b, 0, 0)),
                  pl.BlockSpec((None, KVH, HPG * tq, ns), lambda i, b: (i, 0, 0, 0)),
                  pl.BlockSpec((KVH, ns, LANE), lambda i, b: (0, 0, 0))],
        out_specs=[pl.BlockSpec((tq, NSA_WIDTH), lambda i, b: (b * nt + i, 0)),
                   pl.BlockSpec((tq, LANE), lambda i, b: (b * nt + i, 0))],
        out_shape=[jax.ShapeDtypeStruct((batch * seq, NSA_WIDTH), f32),
                   jax.ShapeDtypeStruct((batch * seq, LANE), f32)],
        compiler_params=_cparams(("parallel", "arbitrary")),
        name="nsa_cmp_prompt",
    )(z, z, kc, bias, smap)


def _flash_step(s, mask, v, m, l, acc):
    s = jnp.where(mask, s, NEG)
    m_new = jnp.maximum(m, jnp.max(s, axis=-1, keepdims=True))
    alpha = jnp.exp(m - m_new)
    p = jnp.where(mask, jnp.exp(s - m_new), 0.0)
    l = alpha * l + jnp.sum(p, axis=-1, keepdims=True)
    acc = alpha * acc + jnp.dot(p.astype(bf16), v, preferred_element_type=f32)
    return m_new, l, acc


def _slcwin_prompt_kernel(q_ref, gate_ref, sel_ref, ocmp_ref, ks_ref, kw_ref, bias_ref, exp_ref, o_ref, ksb, kwb):
    i = pl.program_id(1)
    tq = q_ref.shape[0]
    rows = HPG * tq

    @pl.when(i == 0)
    def _():
        ksb[...] = ks_ref[...].astype(bf16)
        kwb[...] = kw_ref[...].astype(bf16)

    q = q_ref[...]
    gates = _sigmoid(gate_ref[...])
    sel = sel_ref[...].astype(bf16)
    rel1 = lax.broadcasted_iota(jnp.int32, (tq, tq), 0) - lax.broadcasted_iota(jnp.int32, (tq, tq), 1)
    rel = jnp.concatenate([rel1] * HPG, axis=0)
    n_win_tiles = WINDOW // tq + 1
    init = (jnp.full((rows, 1), NEG, f32), jnp.zeros((rows, 1), f32), jnp.zeros((rows, LANE), f32))
    pieces = []
    for g in range(KVH):
        kl = slice((g // 2) * LANE, (g // 2 + 1) * LANE)
        vl = slice(KVH * DH + (g // 2) * LANE, KVH * DH + (g // 2 + 1) * LANE)
        qp = _q_pad(q, g, tq)

        def slc_body(j, carry, g=g, kl=kl, vl=vl, qp=qp):
            r0 = pl.multiple_of(j * tq, tq)
            s = _dot_nt(qp, ksb[pl.ds(r0, tq), kl]) * ATTN_SCALE + bias_ref[jnp.minimum(i - j, 2), g]
            picked = jnp.dot(sel, exp_ref[g, j], preferred_element_type=f32) > 0.5
            mask = jnp.concatenate([picked] * HPG, axis=0) & ((i - j) * tq + rel >= 0)
            return _flash_step(s, mask, ksb[pl.ds(r0, tq), vl], *carry)

        _, l_s, acc_s = lax.fori_loop(0, i + 1, slc_body, init)

        def win_body(kk, carry, g=g, kl=kl, vl=vl, qp=qp):
            r0 = pl.multiple_of((i - kk) * tq, tq)
            s = _dot_nt(qp, kwb[pl.ds(r0, tq), kl]) * ATTN_SCALE + bias_ref[jnp.minimum(kk, 2), g]
            dist = kk * tq + rel
            return _flash_step(s, (dist >= 0) & (dist < WINDOW), kwb[pl.ds(r0, tq), vl], *carry)

        _, l_w, acc_w = lax.fori_loop(0, jnp.minimum(i, n_win_tiles - 1) + 1, win_body, init)
        o_s = acc_s / jnp.maximum(l_s, 1e-30)
        o_w = acc_w / jnp.maximum(l_w, 1e-30)
        half = slice((g % 2) * DH, (g % 2 + 1) * DH)
        for h in range(HPG):
            col = (g * HPG + h) * 3
            hr = slice(h * tq, (h + 1) * tq)
            pieces.append(gates[:, col + 1:col + 2] * o_s[hr, half] + gates[:, col + 2:col + 3] * o_w[hr, half])
    o_ref[...] = ocmp_ref[...] + jnp.concatenate(pieces, axis=1)


def _slcwin_prompt(z, sel, ocmp, rel_bias, batch, seq, tq):
    nt = seq // tq
    n_blk_lanes = LANE // KVH
    kk = np.arange(3)[:, None, None]
    dist = kk * tq + np.arange(tq)[None, :, None] - np.arange(tq)[None, None, :]
    assert 2 * tq - (tq - 1) >= MAX_DISTANCE, "tile distance >= 2 must map to the last bucket"
    bias = _bias_by_dist(rel_bias, dist)
    bias = bias.reshape(KVH, HPG, 3, tq, tq).transpose(2, 0, 1, 3, 4).reshape(3, KVH, HPG * tq, tq)
    ex = np.zeros((KVH, nt, LANE, tq), np.float32)
    for g in range(KVH):
        for j in range(nt):
            for s in range(tq):
                blk = (j * tq + s) // SLC_BLOCK
                if blk < n_blk_lanes:
                    ex[g, j, g * n_blk_lanes + blk, s] = 1.0
    ex = jnp.asarray(ex).astype(bf16)
    row = lambda w, c: pl.BlockSpec((tq, w), lambda b, i, c=c: (b * nt + i, c))
    return pl.pallas_call(
        _slcwin_prompt_kernel,
        grid=(batch, nt),
        in_specs=[row(NSA_WIDTH, COL_Q // NSA_WIDTH), row(LANE, COL_GATE // LANE),
                  pl.BlockSpec((tq, LANE), lambda b, i: (b * nt + i, 0)),
                  pl.BlockSpec((tq, NSA_WIDTH), lambda b, i: (b * nt + i, 0)),
                  pl.BlockSpec((seq, KV_WIDTH), lambda b, i: (b, COL_KVS // KV_WIDTH)),
                  pl.BlockSpec((seq, KV_WIDTH), lambda b, i: (b, COL_KVW // KV_WIDTH)),
                  pl.BlockSpec(bias.shape, lambda b, i: (0, 0, 0, 0)),
                  pl.BlockSpec(ex.shape, lambda b, i: (0, 0, 0, 0))],
        out_specs=pl.BlockSpec((tq, NSA_WIDTH), lambda b, i: (b * nt + i, 0)),
        out_shape=jax.ShapeDtypeStruct((batch * seq, NSA_WIDTH), f32),
        scratch_shapes=[pltpu.VMEM((seq, KV_WIDTH), bf16), pltpu.VMEM((seq, KV_WIDTH), bf16)],
        compiler_params=_cparams(("parallel", "arbitrary"), V7X_VMEM_LIMIT),
        name="nsa_slcwin_prompt",
    )(z, z, sel, ocmp, z, z, bias, ex)


ROUTE_GATE_LANE = 8


def _outproj_router_kernel(x_ref, ohg_ref, onsa_ref, wo_ref, g2_ref, rw_ref, rb_ref, x1_ref, xn_ref, route_ref):
    x1 = (x_ref[...]
          + jnp.dot(ohg_ref[...].astype(bf16), wo_ref[0:HG_WIDTH, :], preferred_element_type=f32)
          + jnp.dot(onsa_ref[...].astype(bf16), wo_ref[HG_WIDTH:, :], preferred_element_type=f32))
    x1_ref[...] = x1
    xn = x1 * lax.rsqrt(jnp.mean(x1 * x1, axis=-1, keepdims=True) + RMS_EPS) * g2_ref[...]
    xn_ref[...] = xn
    logits = jnp.dot(xn.astype(bf16), rw_ref[...], preferred_element_type=f32) + rb_ref[...]
    lane = lax.broadcasted_iota(jnp.int32, logits.shape, 1)
    route = jnp.zeros(logits.shape, f32)
    work = logits
    top = []
    for k in range(TOP_K):
        m = jnp.max(work, axis=-1, keepdims=True)
        idx = jnp.min(jnp.where(work == m, lane, LANE), axis=-1, keepdims=True)
        top.append(m)
        route = jnp.where(lane == k, idx.astype(f32), route)
        work = jnp.where(lane == idx, -jnp.inf, work)
    es = [jnp.exp(t - top[0]) for t in top]
    denom = es[0] + es[1] + es[2] + es[3]
    for k in range(TOP_K):
        route = jnp.where(lane == ROUTE_GATE_LANE + k, es[k] / denom, route)
    route_ref[...] = route


def _outproj_router(x, o_hg, o_nsa, wo_bf16, g2, rw_pad, rb_pad, tm):
    n, d = x.shape
    row = lambda w: pl.BlockSpec((tm, w), lambda i: (i, 0))
    full = lambda a: pl.BlockSpec(a.shape, lambda i: (0, 0))
    g2 = g2.reshape(1, d)
    return pl.pallas_call(
        _outproj_router_kernel,
        grid=(n // tm,),
        in_specs=[row(d), row(HG_WIDTH), row(NSA_WIDTH), full(wo_bf16), full(g2), full(rw_pad), full(rb_pad)],
        out_specs=[row(d), row(d), row(LANE)],
        out_shape=[jax.ShapeDtypeStruct((n, d), f32), jax.ShapeDtypeStruct((n, d), f32),
                   jax.ShapeDtypeStruct((n, LANE), f32)],
        compiler_params=_cparams(("parallel",), V7X_VMEM_LIMIT),
        name="outproj_router",
    )(x, o_hg, o_nsa, wo_bf16, g2, rw_pad, rb_pad)


MOE_ROWS = 1024
MOE_SUB = 256
MOE_TF = 256
ROW_DMA_UNROLL = 8


def _row_copies(n_rows, copy_fn):
    def start(r, c):
        for k in range(TOP_K):
            copy_fn(r, k).start()
        return c

    def wait(r, c):
        for k in range(TOP_K):
            copy_fn(r, k).wait()
        return c

    lax.fori_loop(0, n_rows, start, 0, unroll=ROW_DMA_UNROLL)
    lax.fori_loop(0, n_rows, wait, 0, unroll=ROW_DMA_UNROLL)


def _dispatch_kernel(slot_ref, xn_ref, xs_in_ref, xs_ref, sem):
    del xs_in_ref
    tb = xn_ref.shape[0]

    def copy(r, k):
        return pltpu.make_async_copy(xn_ref.at[pl.ds(r, 1)], xs_ref.at[pl.ds(slot_ref[r * TOP_K + k], 1)], sem)

    _row_copies(tb, copy)


def _dispatch(slots_flat, xn, xs, tb):
    n, d = xn.shape
    return pl.pallas_call(
        _dispatch_kernel,
        grid=(n // tb,),
        in_specs=[pl.BlockSpec((tb * TOP_K,), lambda i: (i,), memory_space=pltpu.SMEM),
                  pl.BlockSpec((tb, d), lambda i: (i, 0)),
                  pl.BlockSpec(memory_space=pl.ANY)],
        out_specs=pl.BlockSpec(memory_space=pl.ANY),
        out_shape=jax.ShapeDtypeStruct(xs.shape, xs.dtype),
        scratch_shapes=[pltpu.SemaphoreType.DMA(())],
        input_output_aliases={2: 0},
        compiler_params=_cparams(("arbitrary",)),
        name="moe_dispatch",
    )(slots_flat, xn, xs)


def _expert_kernel(ie_ref, ir_ref, x_ref, w1g_ref, w1u_ref, b1g_ref, b1u_ref, w2_ref, b2_ref, y_ref):
    m = pl.program_id(0)
    j = pl.program_id(1)
    rows = ir_ref[m]

    for sb in range(MOE_ROWS // MOE_SUB):
        @pl.when((j == 0) & (sb * MOE_SUB >= rows))
        def _():
            y_ref[sb * MOE_SUB:(sb + 1) * MOE_SUB, :] = jnp.zeros((MOE_SUB, y_ref.shape[1]), f32)

    @pl.when(rows > 0)
    def _():
        w1g = w1g_ref[...].astype(bf16)
        w1u = w1u_ref[...].astype(bf16)
        w2 = w2_ref[...].astype(bf16)
        for sb in range(MOE_ROWS // MOE_SUB):
            rs = slice(sb * MOE_SUB, (sb + 1) * MOE_SUB)

            @pl.when(sb * MOE_SUB < rows)
            def _():
                x = x_ref[rs, :].astype(bf16)
                hg = jnp.dot(x, w1g, preferred_element_type=f32) + b1g_ref[...]
                hu = jnp.dot(x, w1u, preferred_element_type=f32) + b1u_ref[...]
                gl = jnp.minimum(hg, SWIGLU_LIMIT)
                up = jnp.clip(hu, -SWIGLU_LIMIT, SWIGLU_LIMIT)
                act = (up + 1.0) * gl * _sigmoid(SWIGLU_ALPHA * gl)
                part = jnp.dot(act.astype(bf16), w2, preferred_element_type=f32)

                @pl.when(j == 0)
                def _():
                    y_ref[rs, :] = part + b2_ref[...]

                @pl.when(j > 0)
                def _():
                    y_ref[rs, :] += part


def _experts(item_e, item_rows, xs, w1, b1, w2, b2):
    n_items = item_e.shape[0]
    d = xs.shape[1]
    nf = D_FF // MOE_TF
    jj = lambda m, j, ir: jnp.where(ir[m] > 0, j, nf - 1)
    b1 = b1.reshape(N_EXPERTS, 1, 2 * D_FF)
    b2 = b2.reshape(N_EXPERTS, 1, d)
    return pl.pallas_call(
        _expert_kernel,
        grid_spec=pltpu.PrefetchScalarGridSpec(
            num_scalar_prefetch=2,
            grid=(n_items, nf),
            in_specs=[pl.BlockSpec((MOE_ROWS, d), lambda m, j, ie, ir: (m, 0)),
                      pl.BlockSpec((None, d, MOE_TF), lambda m, j, ie, ir: (ie[m], 0, jj(m, j, ir))),
                      pl.BlockSpec((None, d, MOE_TF), lambda m, j, ie, ir: (ie[m], 0, nf + jj(m, j, ir))),
                      pl.BlockSpec((None, 1, MOE_TF), lambda m, j, ie, ir: (ie[m], 0, jj(m, j, ir))),
                      pl.BlockSpec((None, 1, MOE_TF), lambda m, j, ie, ir: (ie[m], 0, nf + jj(m, j, ir))),
                      pl.BlockSpec((None, MOE_TF, d), lambda m, j, ie, ir: (ie[m], jj(m, j, ir), 0)),
                      pl.BlockSpec((None, 1, d), lambda m, j, ie, ir: (ie[m], 0, 0))],
            out_specs=pl.BlockSpec((MOE_ROWS, d), lambda m, j, ie, ir: (m, 0))),
        out_shape=jax.ShapeDtypeStruct(xs.shape, f32),
        compiler_params=_cparams(("arbitrary", "arbitrary"), V7X_VMEM_LIMIT),
        name="moe_experts",
    )(item_e, item_rows, xs, w1, w1, b1, b1, w2, b2)


def _combine_kernel(slot_ref, x1_ref, route_ref, gf_ref, ys_ref, y_ref, buf, sem):
    tb = x1_ref.shape[0]

    def copy(r, k):
        return pltpu.make_async_copy(ys_ref.at[pl.ds(slot_ref[r * TOP_K + k], 1)], buf.at[k, pl.ds(r, 1)], sem)

    _row_copies(tb, copy)
    route = route_ref[...]
    x2 = x1_ref[...]
    for k in range(TOP_K):
        x2 = x2 + route[:, ROUTE_GATE_LANE + k:ROUTE_GATE_LANE + k + 1] * buf[k]
    y_ref[...] = x2 * lax.rsqrt(jnp.mean(x2 * x2, axis=-1, keepdims=True) + RMS_EPS) * gf_ref[...]


def _combine(slots_flat, x1, route, gf, ys, tb):
    n, d = x1.shape
    return pl.pallas_call(
        _combine_kernel,
        grid=(n // tb,),
        in_specs=[pl.BlockSpec((tb * TOP_K,), lambda i: (i,), memory_space=pltpu.SMEM),
                  pl.BlockSpec((tb, d), lambda i: (i, 0)),
                  pl.BlockSpec((tb, LANE), lambda i: (i, 0)),
                  pl.BlockSpec((1, d), lambda i: (0, 0)),
                  pl.BlockSpec(memory_space=pl.ANY)],
        out_specs=pl.BlockSpec((tb, d), lambda i: (i, 0)),
        out_shape=jax.ShapeDtypeStruct((n, d), f32),
        scratch_shapes=[pltpu.VMEM((TOP_K, tb, d), f32), pltpu.SemaphoreType.DMA(())],
        compiler_params=_cparams(("arbitrary",), V7X_VMEM_LIMIT),
        name="moe_combine",
    )(slots_flat, x1, route, gf.reshape(1, d), ys)


def _routing_plan(top_e, n_items):
    flat_e = top_e.reshape(-1)
    onehot = (flat_e[:, None] == jnp.arange(N_EXPERTS, dtype=jnp.int32)[None, :]).astype(jnp.int32)
    csum = jnp.cumsum(onehot, axis=0)
    rank = jnp.sum(onehot * (csum - onehot), axis=1)
    counts = csum[-1]
    padded = (counts + MOE_ROWS - 1) // MOE_ROWS * MOE_ROWS
    pad_end = jnp.cumsum(padded)
    start = pad_end - padded
    slots = (start[flat_e] + rank).astype(jnp.int32)
    row0 = jnp.arange(n_items, dtype=jnp.int32) * MOE_ROWS
    item_e = jnp.minimum(jnp.searchsorted(pad_end, row0, side='right'), N_EXPERTS - 1).astype(jnp.int32)
    item_rows = jnp.clip(counts[item_e] - (row0 - start[item_e]), 0, MOE_ROWS).astype(jnp.int32)
    used = row0 < pad_end[-1]
    last_e = item_e[jnp.maximum(pad_end[-1] // MOE_ROWS - 1, 0)]
    item_e = jnp.where(used, item_e, last_e)
    item_rows = jnp.where(used, item_rows, 0)
    return slots, item_e, item_rows


SUB = 8


def _hgrn_sample_kernel(z_ref, lb_ref, gain_ref, s0_ref, o_ref, s_ref):
    gain = gain_ref[...]
    eye = lax.broadcasted_iota(jnp.int32, (HG_DK, HG_DK), 0) == lax.broadcasted_iota(jnp.int32, (HG_DK, HG_DK), 1)

    def column(rowvec):
        return jnp.sum(jnp.where(eye, jnp.broadcast_to(rowvec, (HG_DK, HG_DK)), 0.0), axis=-1, keepdims=True)

    for h in range(HG_HEADS):
        seg = lambda i, h=h: z_ref[:, i * HG_WIDTH + h * HG_DK:i * HG_WIDTH + (h + 1) * HG_DK]
        cs = slice(h * HG_DK, (h + 1) * HG_DK)
        q, k, g = _hgrn_gates(seg(0), seg(1), lb_ref[:, cs])
        v = seg(2)
        eg = jnp.exp(g)
        s0 = s0_ref[h]
        qe = jnp.broadcast_to(q * eg, (SUB, HG_DK)).astype(bf16)
        o = jnp.sum(q * k, axis=-1, keepdims=True) * v + jnp.dot(qe, s0.astype(bf16), preferred_element_type=f32)[0:1]
        s_ref[h] = column(eg) * s0 + column(k) * v
        o_ref[:, cs] = _hgrn_out(o, gain, seg(3))


def _row3(z):
    return z.reshape(z.shape[0], 1, z.shape[1])


def _hgrn_sample(z, lb, gain, s0):
    bs = z.shape[0]
    o, s = pl.pallas_call(
        _hgrn_sample_kernel,
        grid=(bs,),
        in_specs=[pl.BlockSpec((None, 1, 4 * HG_WIDTH), lambda b: (b, 0, 0)),
                  pl.BlockSpec((1, HG_WIDTH), lambda b: (0, 0)),
                  pl.BlockSpec((1, HG_DV), lambda b: (0, 0)),
                  pl.BlockSpec((None, HG_HEADS, HG_DK, HG_DV), lambda b: (b, 0, 0, 0))],
        out_specs=[pl.BlockSpec((None, 1, HG_WIDTH), lambda b: (b, 0, 0)),
                   pl.BlockSpec((None, HG_HEADS, HG_DK, HG_DV), lambda b: (b, 0, 0, 0))],
        out_shape=[jax.ShapeDtypeStruct((bs, 1, HG_WIDTH), f32), jax.ShapeDtypeStruct(s0.shape, f32)],
        compiler_params=_cparams(("parallel",)),
        name="hgrn_sample",
    )(_row3(z), lb.reshape(1, HG_WIDTH), gain.reshape(1, HG_DV), s0)
    return o.reshape(bs, HG_WIDTH), s


def _q_pad_row(q, g):
    qb = jnp.broadcast_to(q, (SUB, q.shape[1]))
    z = jnp.zeros((SUB, DH), f32)
    row = lax.broadcasted_iota(jnp.int32, (SUB, LANE), 0)
    out = jnp.zeros((SUB, LANE), f32)
    for h in range(HPG):
        c0 = (g * HPG + h) * DH
        piece = jnp.concatenate([qb[:, c0:c0 + DH], z] if g % 2 == 0 else [z, qb[:, c0:c0 + DH]], axis=1)
        out = jnp.where(row == h, piece, out)
    return out.astype(bf16)


def _head_pieces(o, g):
    half = slice((g % 2) * DH, (g % 2 + 1) * DH)
    return [o[h:h + 1, half] for h in range(HPG)]


def _cmp_sample_kernel(q_ref, gate_ref, kc_ref, bias_ref, map_ref, o_ref, idx_ref, *, cur, n_blk_lanes):
    q = q_ref[...]
    gates = _sigmoid(gate_ref[...])
    ns = kc_ref.shape[0]
    pieces = []
    idx_ref[...] = jnp.zeros(idx_ref.shape, jnp.int32)
    r_i = lax.broadcasted_iota(jnp.int32, (n_blk_lanes, n_blk_lanes), 0)
    c_i = lax.broadcasted_iota(jnp.int32, (n_blk_lanes, n_blk_lanes), 1)
    forced_c = (c_i == 0) | ((c_i <= cur) & (c_i > cur - N_LOCAL_BLOCKS))
    slot = lax.broadcasted_iota(jnp.int32, (n_blk_lanes, LANE), 1).astype(f32)
    blk_id = lax.broadcasted_iota(jnp.int32, (n_blk_lanes, LANE), 0)
    for g in range(KVH):
        pair = slice((g // 2) * LANE, (g // 2 + 1) * LANE)
        vpair = slice(KVH * DH + (g // 2) * LANE, KVH * DH + (g // 2 + 1) * LANE)
        s = _dot_nt(_q_pad_row(q, g), kc_ref[:, pair]) * ATTN_SCALE + bias_ref[g]
        valid = s > 0.5 * NEG
        m = jnp.max(s, axis=-1, keepdims=True)
        e = jnp.where(valid, jnp.exp(s - m), 0.0)
        p = e / jnp.maximum(jnp.sum(e, axis=-1, keepdims=True), 1e-30)
        o = jnp.dot(p.astype(bf16), kc_ref[:, vpair], preferred_element_type=f32)
        for h, oh in enumerate(_head_pieces(o, g)):
            col = (g * HPG + h) * 3
            pieces.append(gates[:, col:col + 1] * oh)
        pg = jnp.broadcast_to(jnp.sum(p[0:HPG], axis=0, keepdims=True), (SUB, ns))
        p_slc = jnp.zeros((SUB, n_blk_lanes), f32)
        for part in _split3(pg):
            p_slc = p_slc + jnp.dot(part, map_ref[...], preferred_element_type=f32)
        a = jnp.broadcast_to(p_slc[0:1], (n_blk_lanes, n_blk_lanes))
        a = jnp.where(forced_c, jnp.inf, jnp.where(c_i > cur, -jnp.inf, a))
        bt = a.T
        ahead = (a > bt) | ((a == bt) & (c_i < r_i))
        rank = jnp.sum(jnp.where(ahead, 1.0, 0.0), axis=-1, keepdims=True)
        hit = (rank == slot) & (blk_id <= cur)
        chosen = jnp.sum(jnp.where(hit, blk_id.astype(f32), 0.0), axis=0, keepdims=True)
        idx_ref[g:g + 1, :] = chosen.astype(jnp.int32)
    o_ref[...] = jnp.concatenate(pieces, axis=1)


def _cmp_sample(z, kc, rel_bias, q_pos):
    bs = z.shape[0]
    ns = kc.shape[1]
    nc = ns - (CMP_LEN // CMP_STRIDE - 1)
    n_blk = -(-(q_pos + 1) // SLC_BLOCK)
    assert n_blk >= N_SEL
    n_blk_lanes = -(-n_blk // LANE) * LANE
    k_end = np.arange(ns) * CMP_STRIDE + CMP_LEN - 1
    dist = q_pos - k_end
    bias = _bias_by_dist(rel_bias, dist)
    bias = jnp.where(jnp.asarray((dist >= 0) & (np.arange(ns) < nc))[None], bias, NEG)
    bias = jnp.pad(bias.reshape(KVH, HPG, ns), ((0, 0), (0, SUB - HPG), (0, 0)))
    smap = jnp.asarray(_stride_to_block_map(ns, n_blk_lanes, 0)).astype(bf16)
    kern = functools.partial(_cmp_sample_kernel, cur=q_pos // SLC_BLOCK, n_blk_lanes=n_blk_lanes)
    z3 = _row3(z)
    o, idx = pl.pallas_call(
        kern,
        grid=(bs,),
        in_specs=[pl.BlockSpec((None, 1, NSA_WIDTH), lambda b: (b, 0, COL_Q // NSA_WIDTH)),
                  pl.BlockSpec((None, 1, LANE), lambda b: (b, 0, COL_GATE // LANE)),
                  pl.BlockSpec((None, ns, KV_WIDTH), lambda b: (b, 0, 0)),
                  pl.BlockSpec(bias.shape, lambda b: (0, 0, 0)),
                  pl.BlockSpec(smap.shape, lambda b: (0, 0))],
        out_specs=[pl.BlockSpec((None, 1, NSA_WIDTH), lambda b: (b, 0, 0)),
                   pl.BlockSpec((None, SUB, LANE), lambda b: (b, 0, 0))],
        out_shape=[jax.ShapeDtypeStruct((bs, 1, NSA_WIDTH), f32), jax.ShapeDtypeStruct((bs, SUB, LANE), jnp.int32)],
        compiler_params=_cparams(("parallel",)),
        name="nsa_cmp_sample",
    )(z3, z3, kc, bias, smap)
    return o.reshape(bs, NSA_WIDTH), idx


def _slcwin_sample_kernel(idx_ref, pt_ref, q_ref, gate_ref, ocmp_ref, ksn_ref, kwn_ref, p0_ref, p1_ref, p2_ref, p3_ref,
                          win_ref, bslc_ref, bwin_ref, bnew_ref, o_ref, nwin_ref, qp_s, m_s, l_s, acc_s, ow_s, *, past):
    del pt_ref
    b = pl.program_id(0)
    k = pl.program_id(1)
    pools = (p0_ref, p1_ref, p2_ref, p3_ref)
    wlen = win_ref.shape[0]

    @pl.when(k == 0)
    def _():
        q = q_ref[...]
        knew = kwn_ref[...]
        w = win_ref[...]
        rowi = lax.broadcasted_iota(jnp.int32, w.shape, 0)
        nwin_ref[...] = jnp.where(rowi == wlen - 1, jnp.broadcast_to(knew, w.shape), pltpu.roll(w, wlen - 1, 0))
        wb = w.astype(bf16)
        for g in range(KVH):
            kl = slice((g // 2) * LANE, (g // 2 + 1) * LANE)
            vl = slice(KVH * DH + (g // 2) * LANE, KVH * DH + (g // 2 + 1) * LANE)
            qp = _q_pad_row(q, g)
            qp_s[g] = qp
            m_s[g] = jnp.full((SUB, 1), NEG, f32)
            l_s[g] = jnp.zeros((SUB, 1), f32)
            acc_s[g] = jnp.zeros((SUB, LANE), f32)
            s1 = _dot_nt(qp, wb[:, kl]) * ATTN_SCALE + bwin_ref[g]
            s2 = jnp.sum(qp.astype(f32) * knew[:, kl].astype(bf16).astype(f32), axis=-1, keepdims=True) * ATTN_SCALE \
                + bnew_ref[g][:, 0:1]
            ok = s1 > 0.5 * NEG
            mx = jnp.maximum(jnp.max(s1, axis=-1, keepdims=True), s2)
            e1 = jnp.where(ok, jnp.exp(s1 - mx), 0.0)
            e2 = jnp.exp(s2 - mx)
            den = jnp.maximum(jnp.sum(e1, axis=-1, keepdims=True) + e2, 1e-30)
            ow_s[g] = (jnp.dot(e1.astype(bf16), wb[:, vl], preferred_element_type=f32)
                       + e2.astype(bf16).astype(f32) * knew[:, vl].astype(bf16).astype(f32)) / den

    snew = ksn_ref[...]
    for g in range(KVH):
        kl = slice((g // 2) * LANE, (g // 2 + 1) * LANE)
        vl = slice(KVH * DH + (g // 2) * LANE, KVH * DH + (g // 2 + 1) * LANE)
        blk = idx_ref[(b * KVH + g) * N_SEL + k]
        tile = pools[g][...]
        krow = blk * SLC_BLOCK + lax.broadcasted_iota(jnp.int32, (SLC_BLOCK, LANE), 0)
        kt = jnp.where(krow >= past, jnp.broadcast_to(snew[:, kl], (SLC_BLOCK, LANE)), tile[:, kl]).astype(bf16)
        vt = jnp.where(krow >= past, jnp.broadcast_to(snew[:, vl], (SLC_BLOCK, LANE)), tile[:, vl]).astype(bf16)
        s = _dot_nt(qp_s[g], kt) * ATTN_SCALE + bslc_ref[blk, g]
        kpos = blk * SLC_BLOCK + lax.broadcasted_iota(jnp.int32, (SUB, SLC_BLOCK), 1)
        m, l, acc = _flash_step(s, kpos <= past, vt, m_s[g], l_s[g], acc_s[g])
        m_s[g] = m
        l_s[g] = l
        acc_s[g] = acc

    @pl.when(k == pl.num_programs(1) - 1)
    def _():
        gates = _sigmoid(gate_ref[...])
        pieces = []
        for g in range(KVH):
            o_sl = acc_s[g] / jnp.maximum(l_s[g], 1e-30)
            for h, (ps, pw) in enumerate(zip(_head_pieces(o_sl, g), _head_pieces(ow_s[g], g))):
                col = (g * HPG + h) * 3
                pieces.append(gates[:, col + 1:col + 2] * ps + gates[:, col + 2:col + 3] * pw)
        o_ref[...] = ocmp_ref[...] + jnp.concatenate(pieces, axis=1)


def _slcwin_sample(z, ocmp, idx, pool_slc, page_table, page_rows, win_buf, rel_bias, past):
    bs = z.shape[0]
    n_pages = page_table.shape[1]
    bpp = page_rows // SLC_BLOCK
    pool = pool_slc.reshape(pool_slc.shape[0] * bpp, SLC_BLOCK, KV_WIDTH)
    wlen = win_buf.shape[1]
    n_blk = -(-(past + 1) // SLC_BLOCK)
    kpos = np.arange(n_blk)[:, None] * SLC_BLOCK + np.arange(SLC_BLOCK)[None, :]
    bslc = _bias_by_dist(rel_bias, past - kpos)
    bslc = jnp.pad(bslc.reshape(KVH, HPG, n_blk, SLC_BLOCK), ((0, 0), (0, SUB - HPG), (0, 0), (0, 0))).transpose(2, 0, 1, 3)
    wpos = past - wlen + np.arange(wlen)
    wdist = past - wpos
    bwin = jnp.where(jnp.asarray((wdist < WINDOW) & (wpos >= 0))[None], _bias_by_dist(rel_bias, wdist), NEG)
    bwin = jnp.pad(bwin.reshape(KVH, HPG, wlen), ((0, 0), (0, SUB - HPG), (0, 0)))
    bnew = jnp.broadcast_to(_bias_by_dist(rel_bias, np.zeros((1,), np.int64)).reshape(KVH, HPG, 1), (KVH, HPG, LANE))
    bnew = jnp.pad(bnew, ((0, 0), (0, SUB - HPG), (0, 0)))

    def pool_map(g):
        def f(b, k, idx_r, pt_r):
            blk = idx_r[(b * KVH + g) * N_SEL + k]
            page = pt_r[b * n_pages + jnp.minimum(blk // bpp, n_pages - 1)]
            return (page * bpp + blk % bpp, 0, 0)
        return f

    rowblk = lambda w, c: pl.BlockSpec((None, 1, w), lambda b, k, i, p, c=c: (b, 0, c))
    full = lambda a: pl.BlockSpec(a.shape, lambda b, k, i, p: (0,) * a.ndim)
    kern = functools.partial(_slcwin_sample_kernel, past=past)
    z3 = _row3(z)
    o, new_win = pl.pallas_call(
        kern,
        grid_spec=pltpu.PrefetchScalarGridSpec(
            num_scalar_prefetch=2,
            grid=(bs, N_SEL),
            in_specs=[rowblk(NSA_WIDTH, COL_Q // NSA_WIDTH), rowblk(LANE, COL_GATE // LANE),
                      rowblk(NSA_WIDTH, 0),
                      rowblk(KV_WIDTH, COL_KVS // KV_WIDTH), rowblk(KV_WIDTH, COL_KVW // KV_WIDTH)]
                     + [pl.BlockSpec((None, SLC_BLOCK, KV_WIDTH), pool_map(g)) for g in range(KVH)]
                     + [pl.BlockSpec((None, wlen, KV_WIDTH), lambda b, k, i, p: (b, 0, 0)),
                        full(bslc), full(bwin), full(bnew)],
            out_specs=[rowblk(NSA_WIDTH, 0),
                       pl.BlockSpec((None, wlen, KV_WIDTH), lambda b, k, i, p: (b, 0, 0))],
            scratch_shapes=[pltpu.VMEM((KVH, SUB, LANE), bf16), pltpu.VMEM((KVH, SUB, 1), f32),
                            pltpu.VMEM((KVH, SUB, 1), f32), pltpu.VMEM((KVH, SUB, LANE), f32),
                            pltpu.VMEM((KVH, SUB, LANE), f32)]),
        out_shape=[jax.ShapeDtypeStruct((bs, 1, NSA_WIDTH), f32), jax.ShapeDtypeStruct(win_buf.shape, f32)],
        compiler_params=_cparams(("parallel", "arbitrary"), V7X_VMEM_LIMIT),
        name="nsa_slcwin_sample",
    )(idx[:, :KVH, :N_SEL].reshape(-1), page_table.reshape(-1).astype(jnp.int32), z3, z3, _row3(ocmp), z3, z3,
      pool, pool, pool, pool, win_buf, bslc, bwin, bnew)
    return o.reshape(bs, NSA_WIDTH), new_win


def kernel(x_prompt, x_sample, cache_kv_cmp, cache_kv_slc, state_win_kv, state_hgrn, page_table, norm1, w_in, hg_lower_bound, hg_norm, cmp_pe, cmp_w1, cmp_w2, rel_bias, w_out, norm2, router_w, router_b, moe_w1, moe_b1, moe_w2, moe_b2, norm_f):
    batch, seq, d = x_prompt.shape
    bs, dec_seq, _ = x_sample.shape
    assert norm1.shape[0] == 1 and dec_seq == 1
    n_pool, page_rows = cache_kv_cmp.shape[1:3]
    n_pages = page_table.shape[1]
    past = n_pages * page_rows
    wlen = state_win_kv.shape[2]
    assert wlen == WINDOW and past % CMP_STRIDE == 0 and seq % page_rows == 0

    lb = jnp.cumsum(jax.nn.softmax(hg_lower_bound.astype(f32), axis=0), axis=0)[0]
    w_in_p = jnp.pad(w_in[0], ((0, 0), (0, Z_WIDTH - IN_WIDTH))).astype(bf16)
    xp = x_prompt.reshape(batch * seq, d)
    xs = x_sample.reshape(bs, d)
    tq = min(128, seq)

    zp = _in_proj(xp, norm1[0], w_in_p, min(512, batch * seq))
    o_hg_p, s_p = _hgrn_prompt(zp, lb, hg_norm[0], batch, seq, min(256, seq))
    kvc_p = zp[:, COL_KVC:COL_KVC + KV_WIDTH]
    kvs_p = zp[:, COL_KVS:COL_KVS + KV_WIDTH]
    kvw_p = zp[:, COL_KVW:COL_KVW + KV_WIDTH]
    ident = jnp.arange(batch * seq // page_rows, dtype=jnp.int32).reshape(batch, seq // page_rows)
    kc_p = _compress(kvc_p.reshape(-1, page_rows, KV_WIDTH), ident, page_rows, cmp_pe[0], cmp_w1[0], cmp_w2[0])
    ocmp_p, sel = _cmp_prompt(zp, kc_p, rel_bias, batch, seq, tq)
    o_nsa_p = _slcwin_prompt(zp, sel, ocmp_p, rel_bias, batch, seq, tq)

    zs = _in_proj(xs, norm1[0], w_in_p, bs)
    o_hg_s, s_s = _hgrn_sample(zs, lb, hg_norm[0], state_hgrn[0])
    kc_s = _compress(cache_kv_cmp[0].reshape(n_pool, page_rows, KV_WIDTH), page_table, page_rows,
                     cmp_pe[0], cmp_w1[0], cmp_w2[0])
    ocmp_s, idx = _cmp_sample(zs, kc_s, rel_bias, past)
    o_nsa_s, new_win = _slcwin_sample(zs, ocmp_s, idx, cache_kv_slc[0].reshape(n_pool, page_rows, KV_WIDTH),
                                      page_table, page_rows, state_win_kv[0].reshape(bs, wlen, KV_WIDTH), rel_bias, past)

    wo = w_out[0].astype(bf16)
    rw = jnp.pad(router_w[0], ((0, 0), (0, LANE - N_EXPERTS))).astype(bf16)
    rb = jnp.pad(router_b[0].astype(f32), (0, LANE - N_EXPERTS), constant_values=NEG).reshape(1, LANE)
    x1_p, xn_p, route_p = _outproj_router(xp, o_hg_p, o_nsa_p, wo, norm2[0], rw, rb, min(256, batch * seq))
    x1_s, xn_s, route_s = _outproj_router(xs, o_hg_s, o_nsa_s, wo, norm2[0], rw, rb, bs)
    top_e = jnp.concatenate([route_p[:, :TOP_K], route_s[:, :TOP_K]], axis=0).astype(jnp.int32)
    n_tok = batch * seq + bs
    n_items = -(-n_tok * TOP_K // MOE_ROWS) + N_EXPERTS
    slots, item_e, item_rows = _routing_plan(top_e, n_items)
    slots_p, slots_s = slots[:batch * seq * TOP_K], slots[batch * seq * TOP_K:]
    xsort = jnp.zeros((n_items * MOE_ROWS, d), f32)
    xsort = _dispatch(slots_p, xn_p, xsort, min(256, batch * seq))
    xsort = _dispatch(slots_s, xn_s, xsort, bs)
    ysort = _experts(item_e, item_rows, xsort, moe_w1[0], moe_b1[0], moe_w2[0], moe_b2[0])
    y_p = _combine(slots_p, x1_p, route_p, norm_f, ysort, min(128, batch * seq))
    y_s = _combine(slots_s, x1_s, route_s, norm_f, ysort, bs)

    kv5 = lambda a, n, t: a.reshape(1, n, t, 2, KVH, DH)
    win_p = kvw_p.reshape(batch, seq, KV_WIDTH)[:, seq - min(WINDOW, seq):]
    return (y_p.reshape(batch, seq, d), y_s.reshape(bs, 1, d),
            kv5(kvc_p, batch, seq), kv5(kvs_p, batch, seq), kv5(win_p, batch, min(WINDOW, seq)), s_p[None],
            kv5(zs[:, COL_KVC:COL_KVC + KV_WIDTH], bs, 1), kv5(zs[:, COL_KVS:COL_KVS + KV_WIDTH], bs, 1),
            kv5(new_win, bs, wlen), s_s[None])
```

```python
import functools
import math

import jax
import jax.numpy as jnp
import numpy as np
from jax import lax
from jax.experimental import pallas as pl
from jax.experimental.pallas import tpu as pltpu

f32 = jnp.float32
bf16 = jnp.bfloat16

HG_HEADS, HG_DK, HG_DV = 8, 128, 128
HG_STEP = 16
NSA_HEADS, KVH, DH = 16, 4, 64
HPG = NSA_HEADS // KVH
CMP_LEN, CMP_STRIDE, CMP_HIDDEN = 32, 16, 128
SLC_BLOCK, N_SEL, N_LOCAL_BLOCKS, WINDOW = 64, 16, 2, 512
ATTN_SCALE = DH ** -0.5
NUM_BUCKETS, MAX_DISTANCE = 32, 128
N_EXPERTS, TOP_K, D_FF = 32, 4, 2048
SWIGLU_ALPHA, SWIGLU_LIMIT = 1.702, 7.0
RMS_EPS = 1e-5

HG_WIDTH = HG_HEADS * HG_DV
NSA_WIDTH = NSA_HEADS * DH
KV_WIDTH = 2 * KVH * DH
IN_SPLITS = (HG_WIDTH, HG_WIDTH, HG_WIDTH, HG_WIDTH, NSA_WIDTH, KV_WIDTH, KV_WIDTH, KV_WIDTH, NSA_HEADS * 3)
IN_WIDTH = sum(IN_SPLITS)
Z_WIDTH = 7168
COL_Q, COL_KVC, COL_KVS, COL_KVW, COL_GATE = 4096, 5120, 5632, 6144, 6656
LANE = 128
NEG = -1e30

V7X_VMEM_LIMIT = 56 * 1024 * 1024


def _cparams(sem, vmem=None):
    return pltpu.CompilerParams(dimension_semantics=sem, vmem_limit_bytes=vmem)


def _sigmoid(x):
    return 1.0 / (1.0 + jnp.exp(-x))


def _silu(x):
    return x * _sigmoid(x)


def _proj_kernel(x_ref, g_ref, w_ref, z_ref, hn_ref):
    @pl.when(pl.program_id(1) == 0)
    def _():
        x = x_ref[...]
        y = x * lax.rsqrt(jnp.mean(x * x, axis=-1, keepdims=True) + RMS_EPS) * g_ref[...]
        hn_ref[...] = y.astype(bf16)

    z_ref[...] = jnp.dot(hn_ref[...], w_ref[...], preferred_element_type=f32)


def _in_proj(x, gain, w_bf16, tm):
    n, d = x.shape
    tn = Z_WIDTH // 4
    return pl.pallas_call(
        _proj_kernel,
        grid=(n // tm, Z_WIDTH // tn),
        in_specs=[pl.BlockSpec((tm, d), lambda i, j: (i, 0)),
                  pl.BlockSpec((1, d), lambda i, j: (0, 0)),
                  pl.BlockSpec((d, tn), lambda i, j: (0, j))],
        out_specs=pl.BlockSpec((tm, tn), lambda i, j: (i, j)),
        out_shape=jax.ShapeDtypeStruct((n, Z_WIDTH), f32),
        scratch_shapes=[pltpu.VMEM((tm, d), bf16)],
        compiler_params=_cparams(("parallel", "arbitrary"), V7X_VMEM_LIMIT),
        name="in_proj",
    )(x, gain.reshape(1, d), w_bf16)


def _hgrn_gates(q_raw, f_raw, lb):
    q = _silu(q_raw)
    f = lb + (1.0 - lb) * _sigmoid(f_raw)
    return q, 1.0 - f, jnp.log(f)


def _hgrn_out(o, gain, g_raw):
    y = o * lax.rsqrt(jnp.mean(o * o, axis=-1, keepdims=True) + RMS_EPS) * gain
    return y * _silu(g_raw)


def _hgrn_prompt_kernel(q_ref, f_ref, i_ref, g_ref, lb_ref, gain_ref, o_ref, s_ref, st_ref):
    tb = pl.program_id(1)
    n_steps = q_ref.shape[0] // HG_STEP

    @pl.when(tb == 0)
    def _():
        st_ref[...] = jnp.zeros_like(st_ref)

    row = lax.broadcasted_iota(jnp.int32, (HG_STEP, HG_DK), 0)
    gain = gain_ref[...]

    def step(c, carry):
        r0 = pl.multiple_of(c * HG_STEP, HG_STEP)
        for h in range(HG_HEADS):
            cs = slice(h * HG_DK, (h + 1) * HG_DK)
            q, k, g = _hgrn_gates(q_ref[pl.ds(r0, HG_STEP), cs], f_ref[pl.ds(r0, HG_STEP), cs], lb_ref[:, cs])
            v = i_ref[pl.ds(r0, HG_STEP), cs]
            b = g
            for sh in (1, 2, 4, 8):
                b = b + jnp.where(row >= sh, pltpu.roll(b, sh, 0), 0.0)
            b_last = b[HG_STEP - 1:HG_STEP, :]
            st = st_ref[h]
            o = lax.dot_general((q * jnp.exp(b)).astype(bf16), st.astype(bf16),
                                (((1,), (1,)), ((), ())), preferred_element_type=f32)
            for s in range(HG_STEP):
                p = q * k[s:s + 1, :] * jnp.exp(b - b[s:s + 1, :])
                w = jnp.sum(jnp.where(row >= s, p, 0.0), axis=-1, keepdims=True)
                o = o + w * v[s:s + 1, :]
            kd = k * jnp.exp(b_last - b)
            st_ref[h] = jnp.exp(b_last) * st + lax.dot_general(
                v.astype(bf16), kd.astype(bf16), (((0,), (0,)), ((), ())), preferred_element_type=f32)
            o_ref[pl.ds(r0, HG_STEP), cs] = _hgrn_out(o, gain, g_ref[pl.ds(r0, HG_STEP), cs])
        return carry

    lax.fori_loop(0, n_steps, step, 0)

    @pl.when(tb == pl.num_programs(1) - 1)
    def _():
        for h in range(HG_HEADS):
            s_ref[h] = st_ref[h].T


def _hgrn_prompt(z, lb, gain, batch, seq, tt):
    nt = seq // tt
    blk = lambda seg: pl.BlockSpec((tt, HG_WIDTH), lambda b, t, seg=seg: (b * nt + t, seg))
    return pl.pallas_call(
        _hgrn_prompt_kernel,
        grid=(batch, nt),
        in_specs=[blk(0), blk(1), blk(2), blk(3),
                  pl.BlockSpec((1, HG_WIDTH), lambda b, t: (0, 0)),
                  pl.BlockSpec((1, HG_DV), lambda b, t: (0, 0))],
        out_specs=[pl.BlockSpec((tt, HG_WIDTH), lambda b, t: (b * nt + t, 0)),
                   pl.BlockSpec((None, HG_HEADS, HG_DK, HG_DV), lambda b, t: (b, 0, 0, 0))],
        out_shape=[jax.ShapeDtypeStruct((batch * seq, HG_WIDTH), f32),
                   jax.ShapeDtypeStruct((batch, HG_HEADS, HG_DK, HG_DV), f32)],
        scratch_shapes=[pltpu.VMEM((HG_HEADS, HG_DV, HG_DK), f32)],
        compiler_params=_cparams(("parallel", "arbitrary")),
        name="hgrn_prompt",
    )(z, z, z, z, lb.reshape(1, HG_WIDTH), gain.reshape(1, HG_DV))


def _bucket_table(max_dist):
    n = np.arange(max_dist + 1)
    max_exact = NUM_BUCKETS // 2

    def large(dtype):
        nf = np.maximum(n, 1).astype(dtype)
        v = np.log(nf / dtype(max_exact)) / dtype(math.log(MAX_DISTANCE / max_exact)) * dtype(NUM_BUCKETS - max_exact)
        return np.minimum(max_exact + v.astype(np.int32), NUM_BUCKETS - 1)

    lo, hi = large(np.float32), large(np.float64)
    assert (lo == hi).all(), "bucket boundaries must not depend on float rounding"
    return np.where(n < max_exact, n, lo).astype(np.int32)


def _bias_kernel(rb_ref, dist_ref, o_ref, *, thresholds):
    h = pl.program_id(0)
    dist = dist_ref[...]
    acc = jnp.full(dist.shape, rb_ref[h, 0], f32)
    for k, thr in thresholds:
        acc = jnp.where(dist >= thr, rb_ref[h, k], acc)
    o_ref[...] = acc


def _bias_by_dist(rel_bias, dist):
    dist = np.maximum(np.asarray(dist), 0).astype(np.int32)
    shape = dist.shape
    dist2 = dist.reshape(-1, shape[-1])
    table = _bucket_table(int(dist.max()))
    assert (np.diff(table) >= 0).all()
    thresholds = tuple((k, int(np.argmax(table >= k))) for k in range(1, NUM_BUCKETS) if (table >= k).any())
    n_heads = rel_bias.shape[0]
    out = pl.pallas_call(
        functools.partial(_bias_kernel, thresholds=thresholds),
        grid=(n_heads,),
        in_specs=[pl.BlockSpec(memory_space=pltpu.SMEM),
                  pl.BlockSpec(dist2.shape, lambda h: (0, 0))],
        out_specs=pl.BlockSpec((None,) + dist2.shape, lambda h: (h, 0, 0)),
        out_shape=jax.ShapeDtypeStruct((n_heads,) + dist2.shape, f32),
        compiler_params=_cparams(("parallel",)),
        name="rel_bias_table",
    )(rel_bias.astype(f32), jnp.asarray(dist2))
    return out.reshape((n_heads,) + shape)


def _split3(x):
    hi = x.astype(bf16)
    r1 = x - hi.astype(f32)
    mid = r1.astype(bf16)
    lo = (r1 - mid.astype(f32)).astype(bf16)
    return hi, mid, lo


def _dot_nt(a, b):
    return lax.dot_general(a, b, (((1,), (1,)), ((), ())), preferred_element_type=f32)


def _q_pad(q_ref_or_val, g, rows):
    q = q_ref_or_val
    z = jnp.zeros((rows, DH), f32)
    parts = []
    for h in range(HPG):
        c0 = (g * HPG + h) * DH
        qh = q[:, c0:c0 + DH]
        parts.append(jnp.concatenate([qh, z] if g % 2 == 0 else [z, qh], axis=1))
    return jnp.concatenate(parts, axis=0).astype(bf16)


def _compress_compute(xbuf, pe_ref, w1_ref, w1bd_ref, w2bd_ref, o_ref):
    ns = xbuf.shape[0]
    for c in range(2):
        pe_term = jnp.dot(pe_ref[c].astype(bf16), w1_ref[c], preferred_element_type=f32)
        pe_pair = jnp.concatenate([pe_term, pe_term], axis=1)
        for pr in range(KVH // 2):
            lanes = slice(c * KVH * DH + pr * LANE, c * KVH * DH + (pr + 1) * LANE)
            acc = jnp.zeros((ns, 4 * CMP_HIDDEN), f32)
            for s in range(CMP_STRIDE):
                acc = acc + jnp.dot(xbuf[:, s, lanes].astype(bf16), w1bd_ref[c, s], preferred_element_type=f32)
            hid = pe_pair + acc[:, :2 * CMP_HIDDEN] + pltpu.roll(acc[:, 2 * CMP_HIDDEN:], ns - 1, 0)
            o_ref[:, lanes] = jnp.dot(_silu(hid).astype(bf16), w2bd_ref[c], preferred_element_type=f32).astype(bf16)


def _compress_kernel(pt_ref, page_ref, pe_ref, w1_ref, w1bd_ref, w2bd_ref, o_ref, xbuf):
    p = pl.program_id(1)
    spp = page_ref.shape[0]
    xbuf[pl.ds(pl.multiple_of(p * spp, spp), spp)] = page_ref[...]

    @pl.when(p == pl.num_programs(1) - 1)
    def _():
        _compress_compute(xbuf, pe_ref, w1_ref, w1bd_ref, w2bd_ref, o_ref)


PAGES_PER_STEP = 8


def _compress_paged_kernel(pt_ref, *refs):
    page_refs = refs[:PAGES_PER_STEP]
    pe_ref, w1_ref, w1bd_ref, w2bd_ref, o_ref, xbuf = refs[PAGES_PER_STEP:]
    p = pl.program_id(1)
    page_rows = page_refs[0].shape[2]
    spp = page_rows // CMP_STRIDE
    for pi, page_ref in enumerate(page_refs):
        r0 = pl.multiple_of((p * PAGES_PER_STEP + pi) * spp, spp)
        for c in range(2):
            for pr in range(KVH // 2):
                rows = page_ref[c, pr * LANE:(pr + 1) * LANE, :].T
                lanes = slice(c * KVH * DH + pr * LANE, c * KVH * DH + (pr + 1) * LANE)
                xbuf[pl.ds(r0, spp), :, lanes] = rows.reshape(spp, CMP_STRIDE, LANE)

    @pl.when(p == pl.num_programs(1) - 1)
    def _():
        _compress_compute(xbuf, pe_ref, w1_ref, w1bd_ref, w2bd_ref, o_ref)


def _compress_paged(pool_t, page_table, cmp_pe, cmp_w1, cmp_w2):
    batch, n_pages = page_table.shape
    page_rows = pool_t.shape[3]
    assert n_pages % PAGES_PER_STEP == 0 and page_rows == LANE
    ns = n_pages * page_rows // CMP_STRIDE
    pe, w1, w1bd, w2bd = _compress_weights(cmp_pe, cmp_w1, cmp_w2)
    full = lambda a: pl.BlockSpec(a.shape, lambda b, p, pt: (0,) * a.ndim)
    page = lambda pi: pl.BlockSpec((None, 2, KVH * DH, page_rows),
                                   lambda b, p, pt, pi=pi: (pt[b * n_pages + p * PAGES_PER_STEP + pi], 0, 0, 0))
    return pl.pallas_call(
        _compress_paged_kernel,
        grid_spec=pltpu.PrefetchScalarGridSpec(
            num_scalar_prefetch=1,
            grid=(batch, n_pages // PAGES_PER_STEP),
            in_specs=[page(pi) for pi in range(PAGES_PER_STEP)] + [full(pe), full(w1), full(w1bd), full(w2bd)],
            out_specs=pl.BlockSpec((None, ns, KV_WIDTH), lambda b, p, pt: (b, 0, 0)),
            scratch_shapes=[pltpu.VMEM((ns, CMP_STRIDE, KV_WIDTH), f32)]),
        out_shape=jax.ShapeDtypeStruct((batch, ns, KV_WIDTH), bf16),
        compiler_params=_cparams(("parallel", "arbitrary"), V7X_VMEM_LIMIT),
        name="nsa_compress_paged",
    )(page_table.reshape(-1).astype(jnp.int32), *([pool_t] * PAGES_PER_STEP), pe, w1, w1bd, w2bd)


def _compress_weights(cmp_pe, cmp_w1, cmp_w2):
    r = CMP_LEN // CMP_STRIDE
    w1r = cmp_w1.reshape(2, r, CMP_STRIDE, DH, CMP_HIDDEN)
    zero = jnp.zeros_like(w1r[:, 0])
    top = jnp.concatenate([w1r[:, 0], zero, w1r[:, 1], zero], axis=-1)
    bot = jnp.concatenate([zero, w1r[:, 0], zero, w1r[:, 1]], axis=-1)
    w1bd = jnp.concatenate([top, bot], axis=2).astype(bf16)
    z2 = jnp.zeros_like(cmp_w2)
    w2bd = jnp.concatenate([jnp.concatenate([cmp_w2, z2], axis=-1),
                            jnp.concatenate([z2, cmp_w2], axis=-1)], axis=1).astype(bf16)
    pe = cmp_pe.reshape(2, 1, CMP_LEN * DH)
    return pe, cmp_w1.astype(bf16), w1bd, w2bd


def _compress(pool, page_table, page_rows, cmp_pe, cmp_w1, cmp_w2):
    batch, n_pages = page_table.shape
    spp = page_rows // CMP_STRIDE
    ns = n_pages * spp
    pool4 = pool.reshape(pool.shape[0], spp, CMP_STRIDE, KV_WIDTH)
    pe, w1, w1bd, w2bd = _compress_weights(cmp_pe, cmp_w1, cmp_w2)
    full = lambda a: pl.BlockSpec(a.shape, lambda b, p, pt: (0,) * a.ndim)
    return pl.pallas_call(
        _compress_kernel,
        grid_spec=pltpu.PrefetchScalarGridSpec(
            num_scalar_prefetch=1,
            grid=(batch, n_pages),
            in_specs=[pl.BlockSpec((None, spp, CMP_STRIDE, KV_WIDTH), lambda b, p, pt: (pt[b * n_pages + p], 0, 0, 0)),
                      full(pe), full(w1), full(w1bd), full(w2bd)],
            out_specs=pl.BlockSpec((None, ns, KV_WIDTH), lambda b, p, pt: (b, 0, 0)),
            scratch_shapes=[pltpu.VMEM((ns, CMP_STRIDE, KV_WIDTH), f32)]),
        out_shape=jax.ShapeDtypeStruct((batch, ns, KV_WIDTH), bf16),
        compiler_params=_cparams(("parallel", "arbitrary"), V7X_VMEM_LIMIT),
        name="nsa_compress",
    )(page_table.reshape(-1).astype(jnp.int32), pool4, pe, w1, w1bd, w2bd)


def _stride_to_block_map(ns, n_lanes, lane0):
    ratio = SLC_BLOCK // CMP_STRIDE
    m = np.zeros((ns, n_lanes), np.float32)
    for n in range(ns - (CMP_LEN // CMP_STRIDE - 1)):
        for st in range(n, n + CMP_LEN // CMP_STRIDE):
            if lane0 + st // ratio < n_lanes:
                m[n, lane0 + st // ratio] += 1.0
    return m


def _rank_select(score, jidx, n_keep):
    rank = jnp.zeros(score.shape, f32)
    for jp in range(score.shape[0]):
        row = score[jp:jp + 1, :]
        ahead = (row > score) | ((row == score) & (jidx > jp))
        rank = rank + jnp.where(ahead, 1.0, 0.0)
    return rank < n_keep


def _cmp_prompt_kernel(q_ref, gate_ref, kc_ref, bias_ref, map_ref, o_ref, sel_ref):
    i = pl.program_id(0)
    tq = q_ref.shape[0]
    ns = kc_ref.shape[0]
    nc = ns - (CMP_LEN // CMP_STRIDE - 1)
    q = q_ref[...]
    gates = _sigmoid(gate_ref[...])
    t_glob = i * tq + lax.broadcasted_iota(jnp.int32, (tq, ns), 0)
    n_idx = lax.broadcasted_iota(jnp.int32, (tq, ns), 1)
    valid1 = (t_glob >= n_idx * CMP_STRIDE + (CMP_LEN - 1)) & (n_idx < nc)
    valid = jnp.concatenate([valid1] * HPG, axis=0)
    pieces = []
    p_slc = jnp.zeros((tq, LANE), f32)
    for g in range(KVH):
        pair = slice((g // 2) * LANE, (g // 2 + 1) * LANE)
        vpair = slice(KVH * DH + (g // 2) * LANE, KVH * DH + (g // 2 + 1) * LANE)
        bias = jnp.concatenate([bias_ref[g * HPG + h] for h in range(HPG)], axis=0)
        s = _dot_nt(_q_pad(q, g, tq), kc_ref[:, pair]) * ATTN_SCALE + bias
        s = jnp.where(valid, s, NEG)
        m = jnp.max(s, axis=-1, keepdims=True)
        e = jnp.where(valid, jnp.exp(s - m), 0.0)
        p = e / jnp.maximum(jnp.sum(e, axis=-1, keepdims=True), 1e-30)
        o = jnp.dot(p.astype(bf16), kc_ref[:, vpair], preferred_element_type=f32)
        pg = p[0:tq]
        for h in range(HPG):
            col = (g * HPG + h) * 3
            oh = o[h * tq:(h + 1) * tq, (g % 2) * DH:(g % 2 + 1) * DH]
            pieces.append(gates[:, col:col + 1] * oh)
            if h:
                pg = pg + p[h * tq:(h + 1) * tq]
        for part in _split3(pg):
            p_slc = p_slc + jnp.dot(part, map_ref[g], preferred_element_type=f32)
    o_ref[...] = jnp.concatenate(pieces, axis=1)
    pt = p_slc.T
    n_blk_lanes = LANE // KVH
    jidx = lax.broadcasted_iota(jnp.int32, (n_blk_lanes, tq), 0)
    cur = (i * tq + lax.broadcasted_iota(jnp.int32, (n_blk_lanes, tq), 1)) // SLC_BLOCK
    forced = (jidx == 0) | ((jidx <= cur) & (jidx > cur - N_LOCAL_BLOCKS))
    sels = []
    for g in range(KVH):
        sc = pt[g * n_blk_lanes:(g + 1) * n_blk_lanes]
        sc = jnp.where(forced, jnp.inf, jnp.where(jidx > cur, -jnp.inf, sc))
        keep = _rank_select(sc, jidx, N_SEL) & (jidx <= cur)
        sels.append(jnp.where(keep, 1.0, 0.0))
    sel_ref[...] = jnp.concatenate(sels, axis=0).T


def _cmp_prompt(z, kc, rel_bias, batch, seq, tq):
    nt = seq // tq
    ns = kc.shape[1]
    n_blk = seq // SLC_BLOCK
    n_blk_lanes = LANE // KVH
    assert n_blk <= n_blk_lanes
    t = np.arange(seq)[:, None]
    dist = t - (np.arange(ns)[None, :] * CMP_STRIDE + CMP_LEN - 1)
    bias = _bias_by_dist(rel_bias, dist)
    smap = jnp.asarray(np.stack([_stride_to_block_map(ns, LANE, g * n_blk_lanes) for g in range(KVH)])).astype(bf16)
    nq = COL_Q // NSA_WIDTH
    return pl.pallas_call(
        _cmp_prompt_kernel,
        grid=(nt, batch),
        in_specs=[pl.BlockSpec((tq, NSA_WIDTH), lambda i, b: (b * nt + i, nq)),
                  pl.BlockSpec((tq, LANE), lambda i, b: (b * nt + i, COL_GATE // LANE)),
                  pl.BlockSpec((None, ns, KV_WIDTH), lambda i, b: (b, 0, 0)),
                  pl.BlockSpec((NSA_HEADS, tq, ns), lambda i, b: (0, i, 0)),
                  pl.BlockSpec((KVH, ns, LANE), lambda i, b: (0, 0, 0))],
        out_specs=[pl.BlockSpec((tq, NSA_WIDTH), lambda i, b: (b * nt + i, 0)),
                   pl.BlockSpec((tq, LANE), lambda i, b: (b * nt + i, 0))],
        out_shape=[jax.ShapeDtypeStruct((batch * seq, NSA_WIDTH), f32),
                   jax.ShapeDtypeStruct((batch * seq, LANE), f32)],
        compiler_params=_cparams(("parallel", "arbitrary")),
        name="nsa_cmp_prompt",
    )(z, z, kc, bias, smap)


def _flash_step(s, mask, v, m, l, acc):
    s = jnp.where(mask, s, NEG)
    m_new = jnp.maximum(m, jnp.max(s, axis=-1, keepdims=True))
    alpha = jnp.exp(m - m_new)
    p = jnp.where(mask, jnp.exp(s - m_new), 0.0)
    l = alpha * l + jnp.sum(p, axis=-1, keepdims=True)
    acc = alpha * acc + jnp.dot(p.astype(bf16), v, preferred_element_type=f32)
    return m_new, l, acc


def _flash_step_t(s, mask, vt, m, l, acc):
    s = jnp.where(mask, s, NEG)
    m_new = jnp.maximum(m, jnp.max(s, axis=0, keepdims=True))
    alpha = jnp.exp(m - m_new)
    p = jnp.exp(s - m_new)
    l = alpha * l + jnp.sum(p, axis=0, keepdims=True)
    acc[...] = alpha * acc[...] + jnp.dot(vt, p.astype(bf16), preferred_element_type=f32)
    return m_new, l


def _slcwin_prompt_kernel(q_ref, gate_ref, sel_ref, ocmp_ref, ks_ref, kw_ref, bias_ref, exp_ref, o_ref,
                          ksb, kwb, vst, vwt, acc_s, acc_w):
    i = pl.program_id(1)
    tq = q_ref.shape[0]
    cols = HPG * tq
    nt = vst.shape[0]
    half_w = KVH * DH

    @pl.when(i == 0)
    def _():
        ksb[...] = ks_ref[:, 0:half_w].astype(bf16)
        kwb[...] = kw_ref[:, 0:half_w].astype(bf16)
        for j in range(nt):
            vst[j] = ks_ref[j * tq:(j + 1) * tq, half_w:].T.astype(bf16)
            vwt[j] = kw_ref[j * tq:(j + 1) * tq, half_w:].T.astype(bf16)

    qs = q_ref[...] * ATTN_SCALE
    gates = _sigmoid(gate_ref[...])
    sel_t = sel_ref[...].T.astype(bf16)
    rel1 = lax.broadcasted_iota(jnp.int32, (tq, tq), 1) - lax.broadcasted_iota(jnp.int32, (tq, tq), 0)
    rel = jnp.concatenate([rel1] * HPG, axis=1)
    n_win_tiles = WINDOW // tq + 1
    init = (jnp.full((1, cols), NEG, f32), jnp.zeros((1, cols), f32))
    acc_s[...] = jnp.zeros_like(acc_s)
    acc_w[...] = jnp.zeros_like(acc_w)
    zpad = jnp.zeros((DH, tq), f32)
    kls = [slice((g // 2) * LANE, (g // 2 + 1) * LANE) for g in range(KVH)]
    qps = []
    for g in range(KVH):
        qt = qs[:, g * HPG * DH:(g + 1) * HPG * DH].T
        qps.append(jnp.concatenate(
            [jnp.concatenate([qt[h * DH:(h + 1) * DH], zpad] if g % 2 == 0 else [zpad, qt[h * DH:(h + 1) * DH]], axis=0)
             for h in range(HPG)], axis=1).astype(bf16))

    def slc_body(j, carry):
        r0 = pl.multiple_of(j * tq, tq)
        causal = (i - j) * tq + rel >= 0
        out = []
        for g in range(KVH):
            s = jnp.dot(ksb[pl.ds(r0, tq), kls[g]], qps[g], preferred_element_type=f32) + bias_ref[jnp.minimum(i - j, 2), g]
            picked = jnp.dot(exp_ref[g, j], sel_t, preferred_element_type=f32) > 0.5
            mask = jnp.concatenate([picked] * HPG, axis=1) & causal
            out.append(_flash_step_t(s, mask, vst[j, kls[g]], *carry[g], acc_s.at[g]))
        return tuple(out)

    slc = lax.fori_loop(0, i + 1, slc_body, (init,) * KVH)

    def win_body(kk, carry):
        j = i - kk
        r0 = pl.multiple_of(j * tq, tq)
        dist = kk * tq + rel
        mask = (dist >= 0) & (dist < WINDOW)
        out = []
        for g in range(KVH):
            s = jnp.dot(kwb[pl.ds(r0, tq), kls[g]], qps[g], preferred_element_type=f32) + bias_ref[jnp.minimum(kk, 2), g]
            out.append(_flash_step_t(s, mask, vwt[j, kls[g]], *carry[g], acc_w.at[g]))
        return tuple(out)

    win = lax.fori_loop(0, jnp.minimum(i, n_win_tiles - 1) + 1, win_body, (init,) * KVH)
    pieces = []
    for g in range(KVH):
        o_s = (acc_s[g] / jnp.maximum(slc[g][1], 1e-30)).T
        o_w = (acc_w[g] / jnp.maximum(win[g][1], 1e-30)).T
        half = slice((g % 2) * DH, (g % 2 + 1) * DH)
        for h in range(HPG):
            col = (g * HPG + h) * 3
            hr = slice(h * tq, (h + 1) * tq)
            pieces.append(gates[:, col + 1:col + 2] * o_s[hr, half] + gates[:, col + 2:col + 3] * o_w[hr, half])
    o_ref[...] = ocmp_ref[...] + jnp.concatenate(pieces, axis=1)


def _slcwin_prompt(z, sel, ocmp, rel_bias, batch, seq, tq):
    nt = seq // tq
    n_blk_lanes = LANE // KVH
    kk = np.arange(3)[:, None, None]
    dist = kk * tq + np.arange(tq)[None, :, None] - np.arange(tq)[None, None, :]
    assert 2 * tq - (tq - 1) >= MAX_DISTANCE, "tile distance >= 2 must map to the last bucket"
    bias = _bias_by_dist(rel_bias, dist)
    bias = bias.reshape(KVH, HPG, 3, tq, tq).transpose(2, 0, 4, 1, 3).reshape(3, KVH, tq, HPG * tq)
    ex = np.zeros((KVH, nt, tq, LANE), np.float32)
    for g in range(KVH):
        for j in range(nt):
            for s in range(tq):
                blk = (j * tq + s) // SLC_BLOCK
                if blk < n_blk_lanes:
                    ex[g, j, s, g * n_blk_lanes + blk] = 1.0
    ex = jnp.asarray(ex).astype(bf16)
    row = lambda w, c: pl.BlockSpec((tq, w), lambda b, i, c=c: (b * nt + i, c))
    return pl.pallas_call(
        _slcwin_prompt_kernel,
        grid=(batch, nt),
        in_specs=[row(NSA_WIDTH, COL_Q // NSA_WIDTH), row(LANE, COL_GATE // LANE),
                  pl.BlockSpec((tq, LANE), lambda b, i: (b * nt + i, 0)),
                  pl.BlockSpec((tq, NSA_WIDTH), lambda b, i: (b * nt + i, 0)),
                  pl.BlockSpec((seq, KV_WIDTH), lambda b, i: (b, COL_KVS // KV_WIDTH)),
                  pl.BlockSpec((seq, KV_WIDTH), lambda b, i: (b, COL_KVW // KV_WIDTH)),
                  pl.BlockSpec(bias.shape, lambda b, i: (0, 0, 0, 0), pipeline_mode=pl.Buffered(1)),
                  pl.BlockSpec(ex.shape, lambda b, i: (0, 0, 0, 0), pipeline_mode=pl.Buffered(1))],
        out_specs=pl.BlockSpec((tq, NSA_WIDTH), lambda b, i: (b * nt + i, 0)),
        out_shape=jax.ShapeDtypeStruct((batch * seq, NSA_WIDTH), f32),
        scratch_shapes=[pltpu.VMEM((seq, KVH * DH), bf16), pltpu.VMEM((seq, KVH * DH), bf16),
                        pltpu.VMEM((nt, KVH * DH, tq), bf16), pltpu.VMEM((nt, KVH * DH, tq), bf16),
                        pltpu.VMEM((KVH, LANE, HPG * tq), f32), pltpu.VMEM((KVH, LANE, HPG * tq), f32)],
        compiler_params=_cparams(("parallel", "arbitrary"), V7X_VMEM_LIMIT),
        name="nsa_slcwin_prompt",
    )(z, z, sel, ocmp, z, z, bias, ex)


ROUTE_GATE_LANE = 8


def _outproj_router_kernel(x_ref, ohg_ref, onsa_ref, wo_ref, g2_ref, rw_ref, rb_ref, x1_ref, xn_ref, route_ref):
    x1 = (x_ref[...]
          + jnp.dot(ohg_ref[...].astype(bf16), wo_ref[0:HG_WIDTH, :], preferred_element_type=f32)
          + jnp.dot(onsa_ref[...].astype(bf16), wo_ref[HG_WIDTH:, :], preferred_element_type=f32))
    x1_ref[...] = x1
    xn = x1 * lax.rsqrt(jnp.mean(x1 * x1, axis=-1, keepdims=True) + RMS_EPS) * g2_ref[...]
    xn_ref[...] = xn
    logits = jnp.dot(xn.astype(bf16), rw_ref[...], preferred_element_type=f32) + rb_ref[...]
    lane = lax.broadcasted_iota(jnp.int32, logits.shape, 1)
    route = jnp.zeros(logits.shape, f32)
    work = logits
    top = []
    for k in range(TOP_K):
        m = jnp.max(work, axis=-1, keepdims=True)
        idx = jnp.min(jnp.where(work == m, lane, LANE), axis=-1, keepdims=True)
        top.append(m)
        route = jnp.where(lane == k, idx.astype(f32), route)
        work = jnp.where(lane == idx, -jnp.inf, work)
    es = [jnp.exp(t - top[0]) for t in top]
    denom = es[0] + es[1] + es[2] + es[3]
    for k in range(TOP_K):
        route = jnp.where(lane == ROUTE_GATE_LANE + k, es[k] / denom, route)
    route_ref[...] = route


def _outproj_router(x, o_hg, o_nsa, wo_bf16, g2, rw_pad, rb_pad, tm):
    n, d = x.shape
    row = lambda w: pl.BlockSpec((tm, w), lambda i: (i, 0))
    full = lambda a: pl.BlockSpec(a.shape, lambda i: (0, 0))
    g2 = g2.reshape(1, d)
    return pl.pallas_call(
        _outproj_router_kernel,
        grid=(n // tm,),
        in_specs=[row(d), row(HG_WIDTH), row(NSA_WIDTH), full(wo_bf16), full(g2), full(rw_pad), full(rb_pad)],
        out_specs=[row(d), row(d), row(LANE)],
        out_shape=[jax.ShapeDtypeStruct((n, d), f32), jax.ShapeDtypeStruct((n, d), f32),
                   jax.ShapeDtypeStruct((n, LANE), f32)],
        compiler_params=_cparams(("parallel",), V7X_VMEM_LIMIT),
        name="outproj_router",
    )(x, o_hg, o_nsa, wo_bf16, g2, rw_pad, rb_pad)


MOE_ROWS = 1024
MOE_REGION = 512
MOE_SUB = 512
MOE_TF = 256
ROW_DMA_UNROLL = 8


def _row_copies(n_rows, copy_fn):
    def start(r, c):
        for k in range(TOP_K):
            copy_fn(r, k).start()
        return c

    def wait(r, c):
        for k in range(TOP_K):
            copy_fn(r, k).wait()
        return c

    lax.fori_loop(0, n_rows, start, 0, unroll=ROW_DMA_UNROLL)
    lax.fori_loop(0, n_rows, wait, 0, unroll=ROW_DMA_UNROLL)


def _dispatch_kernel(slot_ref, xn_ref, xs_in_ref, xs_ref, sem):
    del xs_in_ref
    tb = xn_ref.shape[0]

    def copy(r, k):
        return pltpu.make_async_copy(xn_ref.at[pl.ds(r, 1)], xs_ref.at[pl.ds(slot_ref[r * TOP_K + k], 1)], sem)

    _row_copies(tb, copy)


def _dispatch(slots_flat, xn, xs, tb):
    n, d = xn.shape
    return pl.pallas_call(
        _dispatch_kernel,
        grid=(n // tb,),
        in_specs=[pl.BlockSpec((tb * TOP_K,), lambda i: (i,), memory_space=pltpu.SMEM),
                  pl.BlockSpec((tb, d), lambda i: (i, 0)),
                  pl.BlockSpec(memory_space=pl.ANY)],
        out_specs=pl.BlockSpec(memory_space=pl.ANY),
        out_shape=jax.ShapeDtypeStruct(xs.shape, xs.dtype),
        scratch_shapes=[pltpu.SemaphoreType.DMA(())],
        input_output_aliases={2: 0},
        compiler_params=_cparams(("arbitrary",)),
        name="moe_dispatch",
    )(slots_flat, xn, xs)


def _expert_kernel(ie_ref, ir_ref, x_ref, w1g_ref, w1u_ref, b1g_ref, b1u_ref, w2_ref, b2_ref, y_ref):
    m = pl.program_id(0)
    j = pl.program_id(1)
    rows = ir_ref[m]

    @pl.when(j == 0)
    def _():
        y_ref[...] = jnp.broadcast_to(b2_ref[...], y_ref.shape)

    @pl.when(rows > 0)
    def _():
        w1g = w1g_ref[...].astype(bf16)
        w1u = w1u_ref[...].astype(bf16)
        w2 = w2_ref[...].astype(bf16)
        for rg in range(MOE_ROWS // MOE_REGION):
            @pl.when(rg * MOE_REGION < rows)
            def _():
                for sb in range(MOE_REGION // MOE_SUB):
                    rs = slice(rg * MOE_REGION + sb * MOE_SUB, rg * MOE_REGION + (sb + 1) * MOE_SUB)
                    x = x_ref[rs, :].astype(bf16)
                    hg = jnp.dot(x, w1g, preferred_element_type=f32) + b1g_ref[...]
                    hu = jnp.dot(x, w1u, preferred_element_type=f32) + b1u_ref[...]
                    gl = jnp.minimum(hg, SWIGLU_LIMIT)
                    up = jnp.clip(hu, -SWIGLU_LIMIT, SWIGLU_LIMIT)
                    act = (up + 1.0) * gl * _sigmoid(SWIGLU_ALPHA * gl)
                    y_ref[rs, :] += jnp.dot(act.astype(bf16), w2, preferred_element_type=f32)


def _experts(item_e, item_rows, xs, w1, b1, w2, b2):
    n_items = item_e.shape[0]
    d = xs.shape[1]
    nf = D_FF // MOE_TF
    jj = lambda m, j, ir: jnp.where(ir[m] > 0, j, nf - 1)
    b1 = b1.reshape(N_EXPERTS, 1, 2 * D_FF)
    b2 = b2.reshape(N_EXPERTS, 1, d)
    return pl.pallas_call(
        _expert_kernel,
        grid_spec=pltpu.PrefetchScalarGridSpec(
            num_scalar_prefetch=2,
            grid=(n_items, nf),
            in_specs=[pl.BlockSpec((MOE_ROWS, d), lambda m, j, ie, ir: (m, 0)),
                      pl.BlockSpec((None, d, MOE_TF), lambda m, j, ie, ir: (ie[m], 0, jj(m, j, ir))),
                      pl.BlockSpec((None, d, MOE_TF), lambda m, j, ie, ir: (ie[m], 0, nf + jj(m, j, ir))),
                      pl.BlockSpec((None, 1, MOE_TF), lambda m, j, ie, ir: (ie[m], 0, jj(m, j, ir))),
                      pl.BlockSpec((None, 1, MOE_TF), lambda m, j, ie, ir: (ie[m], 0, nf + jj(m, j, ir))),
                      pl.BlockSpec((None, MOE_TF, d), lambda m, j, ie, ir: (ie[m], jj(m, j, ir), 0)),
                      pl.BlockSpec((None, 1, d), lambda m, j, ie, ir: (ie[m], 0, 0))],
            out_specs=pl.BlockSpec((MOE_ROWS, d), lambda m, j, ie, ir: (m, 0))),
        out_shape=jax.ShapeDtypeStruct(xs.shape, f32),
        compiler_params=_cparams(("arbitrary", "arbitrary"), V7X_VMEM_LIMIT),
        name="moe_experts",
    )(item_e, item_rows, xs, w1, w1, b1, b1, w2, b2)


def _combine_kernel(slot_ref, x1_ref, route_ref, gf_ref, ys_ref, y_ref, buf, sem):
    tb = x1_ref.shape[0]

    def copy(r, k):
        return pltpu.make_async_copy(ys_ref.at[pl.ds(slot_ref[r * TOP_K + k], 1)], buf.at[k, pl.ds(r, 1)], sem)

    _row_copies(tb, copy)
    route = route_ref[...]
    x2 = x1_ref[...]
    for k in range(TOP_K):
        x2 = x2 + route[:, ROUTE_GATE_LANE + k:ROUTE_GATE_LANE + k + 1] * buf[k]
    y_ref[...] = x2 * lax.rsqrt(jnp.mean(x2 * x2, axis=-1, keepdims=True) + RMS_EPS) * gf_ref[...]


def _combine(slots_flat, x1, route, gf, ys, tb):
    n, d = x1.shape
    return pl.pallas_call(
        _combine_kernel,
        grid=(n // tb,),
        in_specs=[pl.BlockSpec((tb * TOP_K,), lambda i: (i,), memory_space=pltpu.SMEM),
                  pl.BlockSpec((tb, d), lambda i: (i, 0)),
                  pl.BlockSpec((tb, LANE), lambda i: (i, 0)),
                  pl.BlockSpec((1, d), lambda i: (0, 0)),
                  pl.BlockSpec(memory_space=pl.ANY)],
        out_specs=pl.BlockSpec((tb, d), lambda i: (i, 0)),
        out_shape=jax.ShapeDtypeStruct((n, d), f32),
        scratch_shapes=[pltpu.VMEM((TOP_K, tb, d), f32), pltpu.SemaphoreType.DMA(())],
        compiler_params=_cparams(("arbitrary",), V7X_VMEM_LIMIT),
        name="moe_combine",
    )(slots_flat, x1, route, gf.reshape(1, d), ys)


def _routing_plan(top_e, n_items):
    flat_e = top_e.reshape(-1)
    onehot = (flat_e[:, None] == jnp.arange(N_EXPERTS, dtype=jnp.int32)[None, :]).astype(jnp.int32)
    csum = jnp.cumsum(onehot, axis=0)
    rank = jnp.sum(onehot * (csum - onehot), axis=1)
    counts = csum[-1]
    padded = (counts + MOE_ROWS - 1) // MOE_ROWS * MOE_ROWS
    pad_end = jnp.cumsum(padded)
    start = pad_end - padded
    slots = (start[flat_e] + rank).astype(jnp.int32)
    row0 = jnp.arange(n_items, dtype=jnp.int32) * MOE_ROWS
    item_e = jnp.minimum(jnp.searchsorted(pad_end, row0, side='right'), N_EXPERTS - 1).astype(jnp.int32)
    item_rows = jnp.clip(counts[item_e] - (row0 - start[item_e]), 0, MOE_ROWS).astype(jnp.int32)
    used = row0 < pad_end[-1]
    last_e = item_e[jnp.maximum(pad_end[-1] // MOE_ROWS - 1, 0)]
    item_e = jnp.where(used, item_e, last_e)
    item_rows = jnp.where(used, item_rows, 0)
    return slots, item_e, item_rows


SUB = 8


def _hgrn_sample_kernel(z_ref, lb_ref, gain_ref, s0_ref, o_ref, s_ref):
    gain = gain_ref[...]
    eye = lax.broadcasted_iota(jnp.int32, (HG_DK, HG_DK), 0) == lax.broadcasted_iota(jnp.int32, (HG_DK, HG_DK), 1)

    def column(rowvec):
        return jnp.sum(jnp.where(eye, jnp.broadcast_to(rowvec, (HG_DK, HG_DK)), 0.0), axis=-1, keepdims=True)

    for h in range(HG_HEADS):
        seg = lambda i, h=h: z_ref[:, i * HG_WIDTH + h * HG_DK:i * HG_WIDTH + (h + 1) * HG_DK]
        cs = slice(h * HG_DK, (h + 1) * HG_DK)
        q, k, g = _hgrn_gates(seg(0), seg(1), lb_ref[:, cs])
        v = seg(2)
        eg = jnp.exp(g)
        s0 = s0_ref[h]
        qe = jnp.broadcast_to(q * eg, (SUB, HG_DK)).astype(bf16)
        o = jnp.sum(q * k, axis=-1, keepdims=True) * v + jnp.dot(qe, s0.astype(bf16), preferred_element_type=f32)[0:1]
        s_ref[h] = column(eg) * s0 + column(k) * v
        o_ref[:, cs] = _hgrn_out(o, gain, seg(3))


def _row3(z):
    return z.reshape(z.shape[0], 1, z.shape[1])


def _hgrn_sample(z, lb, gain, s0):
    bs = z.shape[0]
    o, s = pl.pallas_call(
        _hgrn_sample_kernel,
        grid=(bs,),
        in_specs=[pl.BlockSpec((None, 1, 4 * HG_WIDTH), lambda b: (b, 0, 0)),
                  pl.BlockSpec((1, HG_WIDTH), lambda b: (0, 0)),
                  pl.BlockSpec((1, HG_DV), lambda b: (0, 0)),
                  pl.BlockSpec((None, HG_HEADS, HG_DK, HG_DV), lambda b: (b, 0, 0, 0))],
        out_specs=[pl.BlockSpec((None, 1, HG_WIDTH), lambda b: (b, 0, 0)),
                   pl.BlockSpec((None, HG_HEADS, HG_DK, HG_DV), lambda b: (b, 0, 0, 0))],
        out_shape=[jax.ShapeDtypeStruct((bs, 1, HG_WIDTH), f32), jax.ShapeDtypeStruct(s0.shape, f32)],
        compiler_params=_cparams(("parallel",)),
        name="hgrn_sample",
    )(_row3(z), lb.reshape(1, HG_WIDTH), gain.reshape(1, HG_DV), s0)
    return o.reshape(bs, HG_WIDTH), s


def _q_pad_row(q, g):
    qb = jnp.broadcast_to(q, (SUB, q.shape[1]))
    z = jnp.zeros((SUB, DH), f32)
    row = lax.broadcasted_iota(jnp.int32, (SUB, LANE), 0)
    out = jnp.zeros((SUB, LANE), f32)
    for h in range(HPG):
        c0 = (g * HPG + h) * DH
        piece = jnp.concatenate([qb[:, c0:c0 + DH], z] if g % 2 == 0 else [z, qb[:, c0:c0 + DH]], axis=1)
        out = jnp.where(row == h, piece, out)
    return out.astype(bf16)


def _head_pieces(o, g):
    half = slice((g % 2) * DH, (g % 2 + 1) * DH)
    return [o[h:h + 1, half] for h in range(HPG)]


def _cmp_sample_kernel(q_ref, gate_ref, kc_ref, bias_ref, map_ref, o_ref, idx_ref, *, cur, n_blk_lanes):
    q = q_ref[...]
    gates = _sigmoid(gate_ref[...])
    ns = kc_ref.shape[0]
    pieces = []
    idx_ref[...] = jnp.zeros(idx_ref.shape, jnp.int32)
    r_i = lax.broadcasted_iota(jnp.int32, (n_blk_lanes, n_blk_lanes), 0)
    c_i = lax.broadcasted_iota(jnp.int32, (n_blk_lanes, n_blk_lanes), 1)
    forced_c = (c_i == 0) | ((c_i <= cur) & (c_i > cur - N_LOCAL_BLOCKS))
    slot = lax.broadcasted_iota(jnp.int32, (n_blk_lanes, LANE), 1).astype(f32)
    blk_id = lax.broadcasted_iota(jnp.int32, (n_blk_lanes, LANE), 0)
    for g in range(KVH):
        pair = slice((g // 2) * LANE, (g // 2 + 1) * LANE)
        vpair = slice(KVH * DH + (g // 2) * LANE, KVH * DH + (g // 2 + 1) * LANE)
        s = _dot_nt(_q_pad_row(q, g), kc_ref[:, pair]) * ATTN_SCALE + bias_ref[g]
        valid = s > 0.5 * NEG
        m = jnp.max(s, axis=-1, keepdims=True)
        e = jnp.where(valid, jnp.exp(s - m), 0.0)
        p = e / jnp.maximum(jnp.sum(e, axis=-1, keepdims=True), 1e-30)
        o = jnp.dot(p.astype(bf16), kc_ref[:, vpair], preferred_element_type=f32)
        for h, oh in enumerate(_head_pieces(o, g)):
            col = (g * HPG + h) * 3
            pieces.append(gates[:, col:col + 1] * oh)
        pg = jnp.broadcast_to(jnp.sum(p[0:HPG], axis=0, keepdims=True), (SUB, ns))
        p_slc = jnp.zeros((SUB, n_blk_lanes), f32)
        for part in _split3(pg):
            p_slc = p_slc + jnp.dot(part, map_ref[...], preferred_element_type=f32)
        a = jnp.broadcast_to(p_slc[0:1], (n_blk_lanes, n_blk_lanes))
        a = jnp.where(forced_c, jnp.inf, jnp.where(c_i > cur, -jnp.inf, a))
        bt = a.T
        ahead = (a > bt) | ((a == bt) & (c_i < r_i))
        rank = jnp.sum(jnp.where(ahead, 1.0, 0.0), axis=-1, keepdims=True)
        hit = (rank == slot) & (blk_id <= cur)
        chosen = jnp.sum(jnp.where(hit, blk_id.astype(f32), 0.0), axis=0, keepdims=True)
        idx_ref[g:g + 1, :] = chosen.astype(jnp.int32)
    o_ref[...] = jnp.concatenate(pieces, axis=1)


def _cmp_sample(z, kc, rel_bias, q_pos):
    bs = z.shape[0]
    ns = kc.shape[1]
    nc = ns - (CMP_LEN // CMP_STRIDE - 1)
    n_blk = -(-(q_pos + 1) // SLC_BLOCK)
    assert n_blk >= N_SEL
    n_blk_lanes = -(-n_blk // LANE) * LANE
    k_end = np.arange(ns) * CMP_STRIDE + CMP_LEN - 1
    dist = q_pos - k_end
    bias = _bias_by_dist(rel_bias, dist)
    bias = jnp.where(jnp.asarray((dist >= 0) & (np.arange(ns) < nc))[None], bias, NEG)
    bias = jnp.pad(bias.reshape(KVH, HPG, ns), ((0, 0), (0, SUB - HPG), (0, 0)))
    smap = jnp.asarray(_stride_to_block_map(ns, n_blk_lanes, 0)).astype(bf16)
    kern = functools.partial(_cmp_sample_kernel, cur=q_pos // SLC_BLOCK, n_blk_lanes=n_blk_lanes)
    z3 = _row3(z)
    o, idx = pl.pallas_call(
        kern,
        grid=(bs,),
        in_specs=[pl.BlockSpec((None, 1, NSA_WIDTH), lambda b: (b, 0, COL_Q // NSA_WIDTH)),
                  pl.BlockSpec((None, 1, LANE), lambda b: (b, 0, COL_GATE // LANE)),
                  pl.BlockSpec((None, ns, KV_WIDTH), lambda b: (b, 0, 0)),
                  pl.BlockSpec(bias.shape, lambda b: (0, 0, 0)),
                  pl.BlockSpec(smap.shape, lambda b: (0, 0))],
        out_specs=[pl.BlockSpec((None, 1, NSA_WIDTH), lambda b: (b, 0, 0)),
                   pl.BlockSpec((None, SUB, LANE), lambda b: (b, 0, 0))],
        out_shape=[jax.ShapeDtypeStruct((bs, 1, NSA_WIDTH), f32), jax.ShapeDtypeStruct((bs, SUB, LANE), jnp.int32)],
        compiler_params=_cparams(("parallel",)),
        name="nsa_cmp_sample",
    )(z3, z3, kc, bias, smap)
    return o.reshape(bs, NSA_WIDTH), idx


def _q_rows(q, g):
    qb = jnp.broadcast_to(q, (SUB, q.shape[1]))
    row = lax.broadcasted_iota(jnp.int32, (SUB, DH), 0)
    out = jnp.zeros((SUB, DH), f32)
    for h in range(HPG):
        c0 = (g * HPG + h) * DH
        out = jnp.where(row == h, qb[:, c0:c0 + DH], out)
    return out


def _column(rowvec):
    n = rowvec.shape[1]
    eye = lax.broadcasted_iota(jnp.int32, (n, n), 0) == lax.broadcasted_iota(jnp.int32, (n, n), 1)
    return jnp.sum(jnp.where(eye, jnp.broadcast_to(rowvec, (n, n)), 0.0), axis=-1, keepdims=True)


def _slcwin_sample_kernel(idx_ref, pt_ref, q_ref, gate_ref, ocmp_ref, ksn_ref, kwn_ref, p0_ref, p1_ref, p2_ref, p3_ref,
                          win_ref, bslc_ref, bwin_ref, bnew_ref, o_ref, nwin_ref, qr_s, m_s, l_s, acc_s, ow_s,
                          *, past, bpp):
    del pt_ref
    b = pl.program_id(0)
    k = pl.program_id(1)
    pools = (p0_ref, p1_ref, p2_ref, p3_ref)
    wlen = win_ref.shape[3]
    page_rows = p0_ref.shape[2]
    half_w = KVH * DH

    @pl.when(k == 0)
    def _():
        q = q_ref[...] * ATTN_SCALE
        wnew = kwn_ref[...]
        lane = lax.broadcasted_iota(jnp.int32, (DH, wlen), 1)
        for g in range(KVH):
            knew = wnew[:, g * DH:(g + 1) * DH]
            vnew = wnew[:, half_w + g * DH:half_w + (g + 1) * DH]
            kt = win_ref[0, g]
            vt = win_ref[1, g]
            nwin_ref[0, g] = jnp.where(lane == wlen - 1, _column(knew), pltpu.roll(kt, wlen - 1, 1))
            nwin_ref[1, g] = jnp.where(lane == wlen - 1, _column(vnew), pltpu.roll(vt, wlen - 1, 1))
            qr = _q_rows(q, g).astype(bf16)
            qr_s[g] = qr
            m_s[g] = jnp.full((SUB, 1), NEG, f32)
            l_s[g] = jnp.zeros((SUB, 1), f32)
            acc_s[g] = jnp.zeros((SUB, DH), f32)
            s1 = jnp.dot(qr, kt.astype(bf16), preferred_element_type=f32) + bwin_ref[g]
            s2 = jnp.sum(qr.astype(f32) * knew.astype(bf16).astype(f32), axis=-1, keepdims=True) + bnew_ref[g][:, 0:1]
            ok = s1 > 0.5 * NEG
            mx = jnp.maximum(jnp.max(s1, axis=-1, keepdims=True), s2)
            e1 = jnp.where(ok, jnp.exp(s1 - mx), 0.0)
            e2 = jnp.exp(s2 - mx)
            den = jnp.maximum(jnp.sum(e1, axis=-1, keepdims=True) + e2, 1e-30)
            ow_s[g] = (_dot_nt(e1.astype(bf16), vt.astype(bf16))
                       + e2.astype(bf16).astype(f32) * vnew.astype(bf16).astype(f32)) / den

    snew = ksn_ref[...]
    lane_k = lax.broadcasted_iota(jnp.int32, (DH, page_rows), 1)
    lane_s = lax.broadcasted_iota(jnp.int32, (SUB, page_rows), 1)
    for g in range(KVH):
        blk = idx_ref[(b * KVH + g) * N_SEL + k]
        page = blk // bpp
        kcol = _column(snew[:, g * DH:(g + 1) * DH])
        vcol = _column(snew[:, half_w + g * DH:half_w + (g + 1) * DH])
        fresh = page * page_rows + lane_k >= past
        kt = jnp.where(fresh, kcol, pools[g][0]).astype(bf16)
        vt = jnp.where(fresh, vcol, pools[g][1]).astype(bf16)
        s = jnp.dot(qr_s[g], kt, preferred_element_type=f32) + bslc_ref[page, g]
        kpos = page * page_rows + lane_s
        mask = (lane_s // SLC_BLOCK == blk % bpp) & (kpos <= past)
        s = jnp.where(mask, s, NEG)
        m_new = jnp.maximum(m_s[g], jnp.max(s, axis=-1, keepdims=True))
        alpha = jnp.exp(m_s[g] - m_new)
        p = jnp.where(mask, jnp.exp(s - m_new), 0.0)
        l_s[g] = alpha * l_s[g] + jnp.sum(p, axis=-1, keepdims=True)
        acc_s[g] = alpha * acc_s[g] + _dot_nt(p.astype(bf16), vt)
        m_s[g] = m_new

    @pl.when(k == pl.num_programs(1) - 1)
    def _():
        gates = _sigmoid(gate_ref[...])
        pieces = []
        for g in range(KVH):
            o_sl = acc_s[g] / jnp.maximum(l_s[g], 1e-30)
            o_w = ow_s[g]
            for h in range(HPG):
                col = (g * HPG + h) * 3
                pieces.append(gates[:, col + 1:col + 2] * o_sl[h:h + 1] + gates[:, col + 2:col + 3] * o_w[h:h + 1])
        o_ref[...] = ocmp_ref[...] + jnp.concatenate(pieces, axis=1)


def _slcwin_sample(z, ocmp, idx, pool_t, page_table, win_t, rel_bias, past):
    bs = z.shape[0]
    n_pages = page_table.shape[1]
    page_rows = pool_t.shape[4]
    bpp = page_rows // SLC_BLOCK
    wlen = win_t.shape[4]
    kpos = np.arange(n_pages + 1)[:, None] * page_rows + np.arange(page_rows)[None, :]
    bslc = _bias_by_dist(rel_bias, past - kpos)
    bslc = jnp.pad(bslc.reshape(KVH, HPG, n_pages + 1, page_rows),
                   ((0, 0), (0, SUB - HPG), (0, 0), (0, 0))).transpose(2, 0, 1, 3)
    wpos = past - wlen + np.arange(wlen)
    wdist = past - wpos
    bwin = jnp.where(jnp.asarray((wdist < WINDOW) & (wpos >= 0))[None], _bias_by_dist(rel_bias, wdist), NEG)
    bwin = jnp.pad(bwin.reshape(KVH, HPG, wlen), ((0, 0), (0, SUB - HPG), (0, 0)))
    bnew = jnp.broadcast_to(_bias_by_dist(rel_bias, np.zeros((1,), np.int64)).reshape(KVH, HPG, 1), (KVH, HPG, LANE))
    bnew = jnp.pad(bnew, ((0, 0), (0, SUB - HPG), (0, 0)))

    def pool_map(g):
        def f(b, k, idx_r, pt_r):
            blk = idx_r[(b * KVH + g) * N_SEL + k]
            return (pt_r[b * n_pages + jnp.minimum(blk // bpp, n_pages - 1)], 0, g, 0, 0)
        return f

    rowblk = lambda w, c: pl.BlockSpec((None, 1, w), lambda b, k, i, p, c=c: (b, 0, c))
    full = lambda a: pl.BlockSpec(a.shape, lambda b, k, i, p: (0,) * a.ndim)
    win_spec = pl.BlockSpec((None, 2, KVH, DH, wlen), lambda b, k, i, p: (b, 0, 0, 0, 0))
    kern = functools.partial(_slcwin_sample_kernel, past=past, bpp=bpp)
    z3 = _row3(z)
    o, new_win_t = pl.pallas_call(
        kern,
        grid_spec=pltpu.PrefetchScalarGridSpec(
            num_scalar_prefetch=2,
            grid=(bs, N_SEL),
            in_specs=[rowblk(NSA_WIDTH, COL_Q // NSA_WIDTH), rowblk(LANE, COL_GATE // LANE),
                      rowblk(NSA_WIDTH, 0),
                      rowblk(KV_WIDTH, COL_KVS // KV_WIDTH), rowblk(KV_WIDTH, COL_KVW // KV_WIDTH)]
                     + [pl.BlockSpec((None, 2, None, DH, page_rows), pool_map(g)) for g in range(KVH)]
                     + [win_spec, full(bslc), full(bwin), full(bnew)],
            out_specs=[rowblk(NSA_WIDTH, 0), win_spec],
            scratch_shapes=[pltpu.VMEM((KVH, SUB, DH), bf16), pltpu.VMEM((KVH, SUB, 1), f32),
                            pltpu.VMEM((KVH, SUB, 1), f32), pltpu.VMEM((KVH, SUB, DH), f32),
                            pltpu.VMEM((KVH, SUB, DH), f32)]),
        out_shape=[jax.ShapeDtypeStruct((bs, 1, NSA_WIDTH), f32), jax.ShapeDtypeStruct(win_t.shape, f32)],
        compiler_params=_cparams(("parallel", "arbitrary"), V7X_VMEM_LIMIT),
        name="nsa_slcwin_sample",
    )(idx[:, :KVH, :N_SEL].reshape(-1), page_table.reshape(-1).astype(jnp.int32), z3, z3, _row3(ocmp), z3, z3,
      pool_t, pool_t, pool_t, pool_t, win_t, bslc, bwin, bnew)
    return o.reshape(bs, NSA_WIDTH), new_win_t


def kernel(x_prompt, x_sample, cache_kv_cmp, cache_kv_slc, state_win_kv, state_hgrn, page_table, norm1, w_in, hg_lower_bound, hg_norm, cmp_pe, cmp_w1, cmp_w2, rel_bias, w_out, norm2, router_w, router_b, moe_w1, moe_b1, moe_w2, moe_b2, norm_f):
    batch, seq, d = x_prompt.shape
    bs, dec_seq, _ = x_sample.shape
    assert norm1.shape[0] == 1 and dec_seq == 1
    n_pool, page_rows = cache_kv_cmp.shape[1:3]
    n_pages = page_table.shape[1]
    past = n_pages * page_rows
    wlen = state_win_kv.shape[2]
    assert wlen == WINDOW and past % CMP_STRIDE == 0 and seq % page_rows == 0

    lb = jnp.cumsum(jax.nn.softmax(hg_lower_bound.astype(f32), axis=0), axis=0)[0]
    w_in_p = jnp.pad(w_in[0], ((0, 0), (0, Z_WIDTH - IN_WIDTH))).astype(bf16)
    xp = x_prompt.reshape(batch * seq, d)
    xs = x_sample.reshape(bs, d)
    tq = min(128, seq)

    zp = _in_proj(xp, norm1[0], w_in_p, min(512, batch * seq))
    o_hg_p, s_p = _hgrn_prompt(zp, lb, hg_norm[0], batch, seq, min(256, seq))
    kvc_p = zp[:, COL_KVC:COL_KVC + KV_WIDTH]
    kvs_p = zp[:, COL_KVS:COL_KVS + KV_WIDTH]
    kvw_p = zp[:, COL_KVW:COL_KVW + KV_WIDTH]
    ident = jnp.arange(batch * seq // page_rows, dtype=jnp.int32).reshape(batch, seq // page_rows)
    kc_p = _compress(kvc_p.reshape(-1, page_rows, KV_WIDTH), ident, page_rows, cmp_pe[0], cmp_w1[0], cmp_w2[0])
    ocmp_p, sel = _cmp_prompt(zp, kc_p, rel_bias, batch, seq, tq)
    o_nsa_p = _slcwin_prompt(zp, sel, ocmp_p, rel_bias, batch, seq, min(256, seq))

    rows_minor = lambda a: jnp.transpose(a, (0, 2, 3, 4, 1))
    zs = _in_proj(xs, norm1[0], w_in_p, bs)
    o_hg_s, s_s = _hgrn_sample(zs, lb, hg_norm[0], state_hgrn[0])
    kc_s = _compress_paged(rows_minor(cache_kv_cmp[0]).reshape(n_pool, 2, KVH * DH, page_rows), page_table,
                           cmp_pe[0], cmp_w1[0], cmp_w2[0])
    ocmp_s, idx = _cmp_sample(zs, kc_s, rel_bias, past)
    o_nsa_s, new_win_t = _slcwin_sample(zs, ocmp_s, idx, rows_minor(cache_kv_slc[0]), page_table,
                                        rows_minor(state_win_kv[0]), rel_bias, past)
    new_win = jnp.transpose(new_win_t, (0, 4, 1, 2, 3))

    wo = w_out[0].astype(bf16)
    rw = jnp.pad(router_w[0], ((0, 0), (0, LANE - N_EXPERTS))).astype(bf16)
    rb = jnp.pad(router_b[0].astype(f32), (0, LANE - N_EXPERTS), constant_values=NEG).reshape(1, LANE)
    x1_p, xn_p, route_p = _outproj_router(xp, o_hg_p, o_nsa_p, wo, norm2[0], rw, rb, min(256, batch * seq))
    x1_s, xn_s, route_s = _outproj_router(xs, o_hg_s, o_nsa_s, wo, norm2[0], rw, rb, bs)
    top_e = jnp.concatenate([route_p[:, :TOP_K], route_s[:, :TOP_K]], axis=0).astype(jnp.int32)
    n_tok = batch * seq + bs
    n_items = -(-n_tok * TOP_K // MOE_ROWS) + N_EXPERTS
    slots, item_e, item_rows = _routing_plan(top_e, n_items)
    slots_p, slots_s = slots[:batch * seq * TOP_K], slots[batch * seq * TOP_K:]
    xsort = jnp.zeros((n_items * MOE_ROWS, d), f32)
    xsort = _dispatch(slots_p, xn_p, xsort, min(256, batch * seq))
    xsort = _dispatch(slots_s, xn_s, xsort, bs)
    ysort = _experts(item_e, item_rows, xsort, moe_w1[0], moe_b1[0], moe_w2[0], moe_b2[0])
    y_p = _combine(slots_p, x1_p, route_p, norm_f, ysort, min(128, batch * seq))
    y_s = _combine(slots_s, x1_s, route_s, norm_f, ysort, bs)

    kv5 = lambda a, n, t: a.reshape(1, n, t, 2, KVH, DH)
    win_p = kvw_p.reshape(batch, seq, KV_WIDTH)[:, seq - min(WINDOW, seq):]
    return (y_p.reshape(batch, seq, d), y_s.reshape(bs, 1, d),
            kv5(kvc_p, batch, seq), kv5(kvs_p, batch, seq), kv5(win_p, batch, min(WINDOW, seq)), s_p[None],
            kv5(zs[:, COL_KVC:COL_KVC + KV_WIDTH], bs, 1), kv5(zs[:, COL_KVS:COL_KVS + KV_WIDTH], bs, 1),
            kv5(new_win, bs, wlen), s_s[None])
```

```python
import functools
import math

import jax
import jax.numpy as jnp
import numpy as np
from jax import lax
from jax.experimental import pallas as pl
from jax.experimental.pallas import tpu as pltpu

f32 = jnp.float32
bf16 = jnp.bfloat16

HG_HEADS, HG_DK, HG_DV = 8, 128, 128
HG_STEP = 16
NSA_HEADS, KVH, DH = 16, 4, 64
HPG = NSA_HEADS // KVH
CMP_LEN, CMP_STRIDE, CMP_HIDDEN = 32, 16, 128
SLC_BLOCK, N_SEL, N_LOCAL_BLOCKS, WINDOW = 64, 16, 2, 512
ATTN_SCALE = DH ** -0.5
NUM_BUCKETS, MAX_DISTANCE = 32, 128
N_EXPERTS, TOP_K, D_FF = 32, 4, 2048
SWIGLU_ALPHA, SWIGLU_LIMIT = 1.702, 7.0
RMS_EPS = 1e-5

HG_WIDTH = HG_HEADS * HG_DV
NSA_WIDTH = NSA_HEADS * DH
KV_WIDTH = 2 * KVH * DH
IN_SPLITS = (HG_WIDTH, HG_WIDTH, HG_WIDTH, HG_WIDTH, NSA_WIDTH, KV_WIDTH, KV_WIDTH, KV_WIDTH, NSA_HEADS * 3)
IN_WIDTH = sum(IN_SPLITS)
Z_WIDTH = 7168
COL_Q, COL_KVC, COL_KVS, COL_KVW, COL_GATE = 4096, 5120, 5632, 6144, 6656
LANE = 128
NEG = -1e30

V7X_VMEM_LIMIT = 56 * 1024 * 1024


def _cparams(sem, vmem=None):
    return pltpu.CompilerParams(dimension_semantics=sem, vmem_limit_bytes=vmem)


def _sigmoid(x):
    return 1.0 / (1.0 + jnp.exp(-x))


def _silu(x):
    return x * _sigmoid(x)


def _proj_kernel(x_ref, g_ref, w_ref, z_ref, hn_ref):
    @pl.when(pl.program_id(1) == 0)
    def _():
        x = x_ref[...]
        y = x * lax.rsqrt(jnp.mean(x * x, axis=-1, keepdims=True) + RMS_EPS) * g_ref[...]
        hn_ref[...] = y.astype(bf16)

    z_ref[...] = jnp.dot(hn_ref[...], w_ref[...], preferred_element_type=f32)


def _in_proj(x, gain, w_bf16, tm):
    n, d = x.shape
    tn = Z_WIDTH // 4
    return pl.pallas_call(
        _proj_kernel,
        grid=(n // tm, Z_WIDTH // tn),
        in_specs=[pl.BlockSpec((tm, d), lambda i, j: (i, 0)),
                  pl.BlockSpec((1, d), lambda i, j: (0, 0)),
                  pl.BlockSpec((d, tn), lambda i, j: (0, j))],
        out_specs=pl.BlockSpec((tm, tn), lambda i, j: (i, j)),
        out_shape=jax.ShapeDtypeStruct((n, Z_WIDTH), f32),
        scratch_shapes=[pltpu.VMEM((tm, d), bf16)],
        compiler_params=_cparams(("parallel", "arbitrary"), V7X_VMEM_LIMIT),
        name="in_proj",
    )(x, gain.reshape(1, d), w_bf16)


def _hgrn_gates(q_raw, f_raw, lb):
    q = _silu(q_raw)
    f = lb + (1.0 - lb) * _sigmoid(f_raw)
    return q, 1.0 - f, jnp.log(f)


def _hgrn_out(o, gain, g_raw):
    y = o * lax.rsqrt(jnp.mean(o * o, axis=-1, keepdims=True) + RMS_EPS) * gain
    return y * _silu(g_raw)


def _hgrn_prompt_kernel(q_ref, f_ref, i_ref, g_ref, lb_ref, gain_ref, o_ref, s_ref, st_ref):
    tb = pl.program_id(1)
    n_steps = q_ref.shape[0] // HG_STEP

    @pl.when(tb == 0)
    def _():
        st_ref[...] = jnp.zeros_like(st_ref)

    row = lax.broadcasted_iota(jnp.int32, (HG_STEP, HG_DK), 0)
    gain = gain_ref[...]

    def step(c, carry):
        r0 = pl.multiple_of(c * HG_STEP, HG_STEP)
        for h in range(HG_HEADS):
            cs = slice(h * HG_DK, (h + 1) * HG_DK)
            q, k, g = _hgrn_gates(q_ref[pl.ds(r0, HG_STEP), cs], f_ref[pl.ds(r0, HG_STEP), cs], lb_ref[:, cs])
            v = i_ref[pl.ds(r0, HG_STEP), cs]
            b = g
            for sh in (1, 2, 4, 8):
                b = b + jnp.where(row >= sh, pltpu.roll(b, sh, 0), 0.0)
            b_last = b[HG_STEP - 1:HG_STEP, :]
            st = st_ref[h]
            o = lax.dot_general((q * jnp.exp(b)).astype(bf16), st.astype(bf16),
                                (((1,), (1,)), ((), ())), preferred_element_type=f32)
            half = HG_STEP // 2
            parts = [(q[:half], b[:half], jnp.zeros((half, HG_DV), f32)), (q[half:], b[half:], jnp.zeros((half, HG_DV), f32))]
            for s in range(HG_STEP):
                for ti in range(s // half, 2):
                    qt, bt, ot = parts[ti]
                    p = qt * k[s:s + 1, :] * jnp.exp(bt - b[s:s + 1, :])
                    if s // half == ti:
                        p = jnp.where(row[:half] >= s - ti * half, p, 0.0)
                    parts[ti] = (qt, bt, ot + jnp.sum(p, axis=-1, keepdims=True) * v[s:s + 1, :])
            o = o + jnp.concatenate([parts[0][2], parts[1][2]], axis=0)
            kd = k * jnp.exp(b_last - b)
            st_ref[h] = jnp.exp(b_last) * st + lax.dot_general(
                v.astype(bf16), kd.astype(bf16), (((0,), (0,)), ((), ())), preferred_element_type=f32)
            o_ref[pl.ds(r0, HG_STEP), cs] = _hgrn_out(o, gain, g_ref[pl.ds(r0, HG_STEP), cs])
        return carry

    lax.fori_loop(0, n_steps, step, 0)

    @pl.when(tb == pl.num_programs(1) - 1)
    def _():
        for h in range(HG_HEADS):
            s_ref[h] = st_ref[h].T


def _hgrn_prompt(z, lb, gain, batch, seq, tt):
    nt = seq // tt
    blk = lambda seg: pl.BlockSpec((tt, HG_WIDTH), lambda b, t, seg=seg: (b * nt + t, seg))
    return pl.pallas_call(
        _hgrn_prompt_kernel,
        grid=(batch, nt),
        in_specs=[blk(0), blk(1), blk(2), blk(3),
                  pl.BlockSpec((1, HG_WIDTH), lambda b, t: (0, 0)),
                  pl.BlockSpec((1, HG_DV), lambda b, t: (0, 0))],
        out_specs=[pl.BlockSpec((tt, HG_WIDTH), lambda b, t: (b * nt + t, 0)),
                   pl.BlockSpec((None, HG_HEADS, HG_DK, HG_DV), lambda b, t: (b, 0, 0, 0))],
        out_shape=[jax.ShapeDtypeStruct((batch * seq, HG_WIDTH), f32),
                   jax.ShapeDtypeStruct((batch, HG_HEADS, HG_DK, HG_DV), f32)],
        scratch_shapes=[pltpu.VMEM((HG_HEADS, HG_DV, HG_DK), f32)],
        compiler_params=_cparams(("parallel", "arbitrary")),
        name="hgrn_prompt",
    )(z, z, z, z, lb.reshape(1, HG_WIDTH), gain.reshape(1, HG_DV))


def _bucket_table(max_dist):
    n = np.arange(max_dist + 1)
    max_exact = NUM_BUCKETS // 2

    def large(dtype):
        nf = np.maximum(n, 1).astype(dtype)
        v = np.log(nf / dtype(max_exact)) / dtype(math.log(MAX_DISTANCE / max_exact)) * dtype(NUM_BUCKETS - max_exact)
        return np.minimum(max_exact + v.astype(np.int32), NUM_BUCKETS - 1)

    lo, hi = large(np.float32), large(np.float64)
    assert (lo == hi).all(), "bucket boundaries must not depend on float rounding"
    return np.where(n < max_exact, n, lo).astype(np.int32)


def _bias_kernel(rb_ref, dist_ref, o_ref, *, thresholds):
    h = pl.program_id(0)
    dist = dist_ref[...]
    acc = jnp.full(dist.shape, rb_ref[h, 0], f32)
    for k, thr in thresholds:
        acc = jnp.where(dist >= thr, rb_ref[h, k], acc)
    o_ref[...] = acc


def _bias_by_dist(rel_bias, dist):
    dist = np.maximum(np.asarray(dist), 0).astype(np.int32)
    shape = dist.shape
    dist2 = dist.reshape(-1, shape[-1])
    table = _bucket_table(int(dist.max()))
    assert (np.diff(table) >= 0).all()
    thresholds = tuple((k, int(np.argmax(table >= k))) for k in range(1, NUM_BUCKETS) if (table >= k).any())
    n_heads = rel_bias.shape[0]
    out = pl.pallas_call(
        functools.partial(_bias_kernel, thresholds=thresholds),
        grid=(n_heads,),
        in_specs=[pl.BlockSpec(memory_space=pltpu.SMEM),
                  pl.BlockSpec(dist2.shape, lambda h: (0, 0))],
        out_specs=pl.BlockSpec((None,) + dist2.shape, lambda h: (h, 0, 0)),
        out_shape=jax.ShapeDtypeStruct((n_heads,) + dist2.shape, f32),
        compiler_params=_cparams(("parallel",)),
        name="rel_bias_table",
    )(rel_bias.astype(f32), jnp.asarray(dist2))
    return out.reshape((n_heads,) + shape)


def _split3(x):
    hi = x.astype(bf16)
    r1 = x - hi.astype(f32)
    mid = r1.astype(bf16)
    lo = (r1 - mid.astype(f32)).astype(bf16)
    return hi, mid, lo


def _dot_nt(a, b):
    return lax.dot_general(a, b, (((1,), (1,)), ((), ())), preferred_element_type=f32)


def _q_pad(q_ref_or_val, g, rows):
    q = q_ref_or_val
    z = jnp.zeros((rows, DH), f32)
    parts = []
    for h in range(HPG):
        c0 = (g * HPG + h) * DH
        qh = q[:, c0:c0 + DH]
        parts.append(jnp.concatenate([qh, z] if g % 2 == 0 else [z, qh], axis=1))
    return jnp.concatenate(parts, axis=0).astype(bf16)


def _compress_compute(xbuf, pe_ref, w1_ref, w1bd_ref, w2bd_ref, o_ref):
    ns = xbuf.shape[1]
    for c in range(2):
        pe_term = jnp.dot(pe_ref[c].astype(bf16), w1_ref[c], preferred_element_type=f32)
        pe_pair = jnp.concatenate([pe_term, pe_term], axis=1)
        for pr in range(KVH // 2):
            lanes = slice(c * KVH * DH + pr * LANE, c * KVH * DH + (pr + 1) * LANE)
            xs = jnp.concatenate([xbuf[s, :, lanes].astype(bf16) for s in range(CMP_STRIDE)], axis=1)
            acc = jnp.dot(xs, w1bd_ref[c], preferred_element_type=f32)
            hid = pe_pair + acc[:, :2 * CMP_HIDDEN] + pltpu.roll(acc[:, 2 * CMP_HIDDEN:], ns - 1, 0)
            o_ref[:, lanes] = jnp.dot(_silu(hid).astype(bf16), w2bd_ref[c], preferred_element_type=f32).astype(bf16)


def _stride_perm(page_rows):
    spp = page_rows // CMP_STRIDE
    perm = np.zeros((page_rows, page_rows), np.float32)
    for s in range(CMP_STRIDE):
        for n in range(spp):
            perm[s * spp + n, n * CMP_STRIDE + s] = 1.0
    return jnp.asarray(perm).astype(bf16)


def _scatter_page(xbuf, xp, r0, spp, lanes):
    for s in range(CMP_STRIDE):
        xbuf[s, pl.ds(r0, spp), lanes] = xp[s * spp:(s + 1) * spp, :]


def _compress_kernel(pt_ref, page_ref, perm_ref, pe_ref, w1_ref, w1bd_ref, w2bd_ref, o_ref, xbuf):
    p = pl.program_id(1)
    spp = page_ref.shape[0] // CMP_STRIDE
    xp = jnp.dot(perm_ref[...], page_ref[...].astype(bf16), preferred_element_type=f32)
    _scatter_page(xbuf, xp, pl.multiple_of(p * spp, spp), spp, slice(None))

    @pl.when(p == pl.num_programs(1) - 1)
    def _():
        _compress_compute(xbuf, pe_ref, w1_ref, w1bd_ref, w2bd_ref, o_ref)


PAGES_PER_STEP = 8


def _compress_paged_kernel(pt_ref, *refs):
    page_refs = refs[:PAGES_PER_STEP]
    perm_ref, pe_ref, w1_ref, w1bd_ref, w2bd_ref, o_ref, xbuf = refs[PAGES_PER_STEP:]
    p = pl.program_id(1)
    page_rows = page_refs[0].shape[2]
    spp = page_rows // CMP_STRIDE
    perm = perm_ref[...]
    for pi, page_ref in enumerate(page_refs):
        r0 = pl.multiple_of((p * PAGES_PER_STEP + pi) * spp, spp)
        for c in range(2):
            for pr in range(KVH // 2):
                xp = _dot_nt(perm, page_ref[c, pr * LANE:(pr + 1) * LANE, :].astype(bf16))
                _scatter_page(xbuf, xp, r0, spp, slice(c * KVH * DH + pr * LANE, c * KVH * DH + (pr + 1) * LANE))

    @pl.when(p == pl.num_programs(1) - 1)
    def _():
        _compress_compute(xbuf, pe_ref, w1_ref, w1bd_ref, w2bd_ref, o_ref)


def _compress_paged(pool_t, page_table, cmp_pe, cmp_w1, cmp_w2):
    batch, n_pages = page_table.shape
    page_rows = pool_t.shape[3]
    assert n_pages % PAGES_PER_STEP == 0 and page_rows == LANE
    ns = n_pages * page_rows // CMP_STRIDE
    pe, w1, w1bd, w2bd = _compress_weights(cmp_pe, cmp_w1, cmp_w2)
    perm = _stride_perm(page_rows)
    full = lambda a: pl.BlockSpec(a.shape, lambda b, p, pt: (0,) * a.ndim)
    page = lambda pi: pl.BlockSpec((None, 2, KVH * DH, page_rows),
                                   lambda b, p, pt, pi=pi: (pt[b * n_pages + p * PAGES_PER_STEP + pi], 0, 0, 0))
    return pl.pallas_call(
        _compress_paged_kernel,
        grid_spec=pltpu.PrefetchScalarGridSpec(
            num_scalar_prefetch=1,
            grid=(batch, n_pages // PAGES_PER_STEP),
            in_specs=[page(pi) for pi in range(PAGES_PER_STEP)]
                     + [full(perm), full(pe), full(w1), full(w1bd), full(w2bd)],
            out_specs=pl.BlockSpec((None, ns, KV_WIDTH), lambda b, p, pt: (b, 0, 0)),
            scratch_shapes=[pltpu.VMEM((CMP_STRIDE, ns, KV_WIDTH), f32)]),
        out_shape=jax.ShapeDtypeStruct((batch, ns, KV_WIDTH), bf16),
        compiler_params=_cparams(("parallel", "arbitrary"), V7X_VMEM_LIMIT),
        name="nsa_compress_paged",
    )(page_table.reshape(-1).astype(jnp.int32), *([pool_t] * PAGES_PER_STEP), perm, pe, w1, w1bd, w2bd)


def _compress_weights(cmp_pe, cmp_w1, cmp_w2):
    r = CMP_LEN // CMP_STRIDE
    w1r = cmp_w1.reshape(2, r, CMP_STRIDE, DH, CMP_HIDDEN)
    zero = jnp.zeros_like(w1r[:, 0])
    top = jnp.concatenate([w1r[:, 0], zero, w1r[:, 1], zero], axis=-1)
    bot = jnp.concatenate([zero, w1r[:, 0], zero, w1r[:, 1]], axis=-1)
    w1bd = jnp.concatenate([top, bot], axis=2).astype(bf16)
    w1bd = w1bd.reshape(2, CMP_STRIDE * LANE, 4 * CMP_HIDDEN)
    z2 = jnp.zeros_like(cmp_w2)
    w2bd = jnp.concatenate([jnp.concatenate([cmp_w2, z2], axis=-1),
                            jnp.concatenate([z2, cmp_w2], axis=-1)], axis=1).astype(bf16)
    pe = cmp_pe.reshape(2, 1, CMP_LEN * DH)
    return pe, cmp_w1.astype(bf16), w1bd, w2bd


def _compress(pool, page_table, page_rows, cmp_pe, cmp_w1, cmp_w2):
    batch, n_pages = page_table.shape
    ns = n_pages * page_rows // CMP_STRIDE
    pe, w1, w1bd, w2bd = _compress_weights(cmp_pe, cmp_w1, cmp_w2)
    perm = _stride_perm(page_rows)
    full = lambda a: pl.BlockSpec(a.shape, lambda b, p, pt: (0,) * a.ndim)
    return pl.pallas_call(
        _compress_kernel,
        grid_spec=pltpu.PrefetchScalarGridSpec(
            num_scalar_prefetch=1,
            grid=(batch, n_pages),
            in_specs=[pl.BlockSpec((None, page_rows, KV_WIDTH), lambda b, p, pt: (pt[b * n_pages + p], 0, 0)),
                      full(perm), full(pe), full(w1), full(w1bd), full(w2bd)],
            out_specs=pl.BlockSpec((None, ns, KV_WIDTH), lambda b, p, pt: (b, 0, 0)),
            scratch_shapes=[pltpu.VMEM((CMP_STRIDE, ns, KV_WIDTH), f32)]),
        out_shape=jax.ShapeDtypeStruct((batch, ns, KV_WIDTH), bf16),
        compiler_params=_cparams(("parallel", "arbitrary"), V7X_VMEM_LIMIT),
        name="nsa_compress",
    )(page_table.reshape(-1).astype(jnp.int32), pool, perm, pe, w1, w1bd, w2bd)


def _stride_to_block_map(ns, n_lanes, lane0):
    ratio = SLC_BLOCK // CMP_STRIDE
    m = np.zeros((ns, n_lanes), np.float32)
    for n in range(ns - (CMP_LEN // CMP_STRIDE - 1)):
        for st in range(n, n + CMP_LEN // CMP_STRIDE):
            if lane0 + st // ratio < n_lanes:
                m[n, lane0 + st // ratio] += 1.0
    return m


def _rank_select(score, jidx, n_keep):
    rank = jnp.zeros(score.shape, f32)
    for jp in range(score.shape[0]):
        row = score[jp:jp + 1, :]
        ahead = (row > score) | ((row == score) & (jidx > jp))
        rank = rank + jnp.where(ahead, 1.0, 0.0)
    return rank < n_keep


def _cmp_prompt_kernel(q_ref, gate_ref, kc_ref, bias_ref, map_ref, o_ref, sel_ref):
    i = pl.program_id(0)
    tq = q_ref.shape[0]
    ns = kc_ref.shape[0]
    nc = ns - (CMP_LEN // CMP_STRIDE - 1)
    q = q_ref[...]
    gates = _sigmoid(gate_ref[...])
    t_glob = i * tq + lax.broadcasted_iota(jnp.int32, (tq, ns), 0)
    n_idx = lax.broadcasted_iota(jnp.int32, (tq, ns), 1)
    valid1 = (t_glob >= n_idx * CMP_STRIDE + (CMP_LEN - 1)) & (n_idx < nc)
    valid = jnp.concatenate([valid1] * HPG, axis=0)
    pieces = []
    p_slc = jnp.zeros((tq, LANE), f32)
    for g in range(KVH):
        pair = slice((g // 2) * LANE, (g // 2 + 1) * LANE)
        vpair = slice(KVH * DH + (g // 2) * LANE, KVH * DH + (g // 2 + 1) * LANE)
        bias = jnp.concatenate([bias_ref[g * HPG + h] for h in range(HPG)], axis=0)
        s = _dot_nt(_q_pad(q, g, tq), kc_ref[:, pair]) * ATTN_SCALE + bias
        s = jnp.where(valid, s, NEG)
        m = jnp.max(s, axis=-1, keepdims=True)
        e = jnp.where(valid, jnp.exp(s - m), 0.0)
        p = e / jnp.maximum(jnp.sum(e, axis=-1, keepdims=True), 1e-30)
        o = jnp.dot(p.astype(bf16), kc_ref[:, vpair], preferred_element_type=f32)
        pg = p[0:tq]
        for h in range(HPG):
            col = (g * HPG + h) * 3
            oh = o[h * tq:(h + 1) * tq, (g % 2) * DH:(g % 2 + 1) * DH]
            pieces.append(gates[:, col:col + 1] * oh)
            if h:
                pg = pg + p[h * tq:(h + 1) * tq]
        for part in _split3(pg):
            p_slc = p_slc + jnp.dot(part, map_ref[g], preferred_element_type=f32)
    o_ref[...] = jnp.concatenate(pieces, axis=1)
    pt = p_slc.T
    n_blk_lanes = LANE // KVH
    jidx = lax.broadcasted_iota(jnp.int32, (n_blk_lanes, tq), 0)
    cur = (i * tq + lax.broadcasted_iota(jnp.int32, (n_blk_lanes, tq), 1)) // SLC_BLOCK
    forced = (jidx == 0) | ((jidx <= cur) & (jidx > cur - N_LOCAL_BLOCKS))
    sels = []
    for g in range(KVH):
        sc = pt[g * n_blk_lanes:(g + 1) * n_blk_lanes]
        sc = jnp.where(forced, jnp.inf, jnp.where(jidx > cur, -jnp.inf, sc))
        keep = _rank_select(sc, jidx, N_SEL) & (jidx <= cur)
        sels.append(jnp.where(keep, 1.0, 0.0))
    sel_ref[...] = jnp.concatenate(sels, axis=0).T


def _cmp_prompt(z, kc, rel_bias, batch, seq, tq):
    nt = seq // tq
    ns = kc.shape[1]
    n_blk = seq // SLC_BLOCK
    n_blk_lanes = LANE // KVH
    assert n_blk <= n_blk_lanes
    t = np.arange(seq)[:, None]
    dist = t - (np.arange(ns)[None, :] * CMP_STRIDE + CMP_LEN - 1)
    bias = _bias_by_dist(rel_bias, dist)
    smap = jnp.asarray(np.stack([_stride_to_block_map(ns, LANE, g * n_blk_lanes) for g in range(KVH)])).astype(bf16)
    nq = COL_Q // NSA_WIDTH
    return pl.pallas_call(
        _cmp_prompt_kernel,
        grid=(nt, batch),
        in_specs=[pl.BlockSpec((tq, NSA_WIDTH), lambda i, b: (b * nt + i, nq)),
                  pl.BlockSpec((tq, LANE), lambda i, b: (b * nt + i, COL_GATE // LANE)),
                  pl.BlockSpec((None, ns, KV_WIDTH), lambda i, b: (b, 0, 0)),
                  pl.BlockSpec((NSA_HEADS, tq, ns), lambda i, b: (0, i, 0)),
                  pl.BlockSpec((KVH, ns, LANE), lambda i, b: (0, 0, 0))],
        out_specs=[pl.BlockSpec((tq, NSA_WIDTH), lambda i, b: (b * nt + i, 0)),
                   pl.BlockSpec((tq, LANE), lambda i, b: (b * nt + i, 0))],
        out_shape=[jax.ShapeDtypeStruct((batch * seq, NSA_WIDTH), f32),
                   jax.ShapeDtypeStruct((batch * seq, LANE), f32)],
        compiler_params=_cparams(("parallel", "arbitrary")),
        name="nsa_cmp_prompt",
    )(z, z, kc, bias, smap)


def _flash_step_t(s, mask, vt, m, l, acc):
    s = jnp.where(mask, s, NEG)
    m_new = jnp.maximum(m, jnp.max(s, axis=0, keepdims=True))
    alpha = jnp.exp(m - m_new)
    p = jnp.exp(s - m_new)
    l = alpha * l + jnp.sum(p, axis=0, keepdims=True)
    acc[...] = alpha * acc[...] + jnp.dot(vt, p.astype(bf16), preferred_element_type=f32)
    return m_new, l


def _slcwin_prompt_kernel(q_ref, gate_ref, sel_ref, ocmp_ref, ks_ref, kw_ref, bias_ref, exp_ref, o_ref,
                          ksb, kwb, vst, vwt, acc_s, acc_w):
    i = pl.program_id(1)
    tq = q_ref.shape[0]
    cols = HPG * tq
    nt = vst.shape[0]
    half_w = KVH * DH

    @pl.when(i == 0)
    def _():
        ksb[...] = ks_ref[:, 0:half_w].astype(bf16)
        kwb[...] = kw_ref[:, 0:half_w].astype(bf16)
        for j in range(nt):
            vst[j] = ks_ref[j * tq:(j + 1) * tq, half_w:].T.astype(bf16)
            vwt[j] = kw_ref[j * tq:(j + 1) * tq, half_w:].T.astype(bf16)

    qs = q_ref[...] * ATTN_SCALE
    gates = _sigmoid(gate_ref[...])
    sel_t = sel_ref[...].T.astype(bf16)
    rel1 = lax.broadcasted_iota(jnp.int32, (tq, tq), 1) - lax.broadcasted_iota(jnp.int32, (tq, tq), 0)
    rel = jnp.concatenate([rel1] * HPG, axis=1)
    n_win_tiles = WINDOW // tq + 1
    init = (jnp.full((1, cols), NEG, f32), jnp.zeros((1, cols), f32))
    acc_s[...] = jnp.zeros_like(acc_s)
    acc_w[...] = jnp.zeros_like(acc_w)
    zpad = jnp.zeros((DH, tq), f32)
    kls = [slice((g // 2) * LANE, (g // 2 + 1) * LANE) for g in range(KVH)]
    qps = []
    for g in range(KVH):
        qt = qs[:, g * HPG * DH:(g + 1) * HPG * DH].T
        qps.append(jnp.concatenate(
            [jnp.concatenate([qt[h * DH:(h + 1) * DH], zpad] if g % 2 == 0 else [zpad, qt[h * DH:(h + 1) * DH]], axis=0)
             for h in range(HPG)], axis=1).astype(bf16))

    def slc_body(j, carry):
        r0 = pl.multiple_of(j * tq, tq)
        causal = (i - j) * tq + rel >= 0
        out = []
        for g in range(KVH):
            s = jnp.dot(ksb[pl.ds(r0, tq), kls[g]], qps[g], preferred_element_type=f32) + bias_ref[jnp.minimum(i - j, 2), g]
            picked = jnp.dot(exp_ref[g, j], sel_t, preferred_element_type=f32) > 0.5
            mask = jnp.concatenate([picked] * HPG, axis=1) & causal
            out.append(_flash_step_t(s, mask, vst[j, kls[g]], *carry[g], acc_s.at[g]))
        return tuple(out)

    slc = lax.fori_loop(0, i + 1, slc_body, (init,) * KVH)

    def win_body(kk, carry):
        j = i - kk
        r0 = pl.multiple_of(j * tq, tq)
        dist = kk * tq + rel
        mask = (dist >= 0) & (dist < WINDOW)
        out = []
        for g in range(KVH):
            s = jnp.dot(kwb[pl.ds(r0, tq), kls[g]], qps[g], preferred_element_type=f32) + bias_ref[jnp.minimum(kk, 2), g]
            out.append(_flash_step_t(s, mask, vwt[j, kls[g]], *carry[g], acc_w.at[g]))
        return tuple(out)

    win = lax.fori_loop(0, jnp.minimum(i, n_win_tiles - 1) + 1, win_body, (init,) * KVH)
    pieces = []
    for g in range(KVH):
        o_s = (acc_s[g] / jnp.maximum(slc[g][1], 1e-30)).T
        o_w = (acc_w[g] / jnp.maximum(win[g][1], 1e-30)).T
        half = slice((g % 2) * DH, (g % 2 + 1) * DH)
        for h in range(HPG):
            col = (g * HPG + h) * 3
            hr = slice(h * tq, (h + 1) * tq)
            pieces.append(gates[:, col + 1:col + 2] * o_s[hr, half] + gates[:, col + 2:col + 3] * o_w[hr, half])
    o_ref[...] = ocmp_ref[...] + jnp.concatenate(pieces, axis=1)


def _slcwin_prompt(z, sel, ocmp, rel_bias, batch, seq, tq):
    nt = seq // tq
    n_blk_lanes = LANE // KVH
    kk = np.arange(3)[:, None, None]
    dist = kk * tq + np.arange(tq)[None, :, None] - np.arange(tq)[None, None, :]
    assert 2 * tq - (tq - 1) >= MAX_DISTANCE, "tile distance >= 2 must map to the last bucket"
    bias = _bias_by_dist(rel_bias, dist)
    bias = bias.reshape(KVH, HPG, 3, tq, tq).transpose(2, 0, 4, 1, 3).reshape(3, KVH, tq, HPG * tq)
    ex = np.zeros((KVH, nt, tq, LANE), np.float32)
    for g in range(KVH):
        for j in range(nt):
            for s in range(tq):
                blk = (j * tq + s) // SLC_BLOCK
                if blk < n_blk_lanes:
                    ex[g, j, s, g * n_blk_lanes + blk] = 1.0
    ex = jnp.asarray(ex).astype(bf16)
    row = lambda w, c: pl.BlockSpec((tq, w), lambda b, i, c=c: (b * nt + i, c))
    return pl.pallas_call(
        _slcwin_prompt_kernel,
        grid=(batch, nt),
        in_specs=[row(NSA_WIDTH, COL_Q // NSA_WIDTH), row(LANE, COL_GATE // LANE),
                  pl.BlockSpec((tq, LANE), lambda b, i: (b * nt + i, 0)),
                  pl.BlockSpec((tq, NSA_WIDTH), lambda b, i: (b * nt + i, 0)),
                  pl.BlockSpec((seq, KV_WIDTH), lambda b, i: (b, COL_KVS // KV_WIDTH)),
                  pl.BlockSpec((seq, KV_WIDTH), lambda b, i: (b, COL_KVW // KV_WIDTH)),
                  pl.BlockSpec(bias.shape, lambda b, i: (0, 0, 0, 0), pipeline_mode=pl.Buffered(1)),
                  pl.BlockSpec(ex.shape, lambda b, i: (0, 0, 0, 0), pipeline_mode=pl.Buffered(1))],
        out_specs=pl.BlockSpec((tq, NSA_WIDTH), lambda b, i: (b * nt + i, 0)),
        out_shape=jax.ShapeDtypeStruct((batch * seq, NSA_WIDTH), f32),
        scratch_shapes=[pltpu.VMEM((seq, KVH * DH), bf16), pltpu.VMEM((seq, KVH * DH), bf16),
                        pltpu.VMEM((nt, KVH * DH, tq), bf16), pltpu.VMEM((nt, KVH * DH, tq), bf16),
                        pltpu.VMEM((KVH, LANE, HPG * tq), f32), pltpu.VMEM((KVH, LANE, HPG * tq), f32)],
        compiler_params=_cparams(("parallel", "arbitrary"), V7X_VMEM_LIMIT),
        name="nsa_slcwin_prompt",
    )(z, z, sel, ocmp, z, z, bias, ex)


ROUTE_GATE_LANE = 8


def _outproj_router_kernel(x_ref, ohg_ref, onsa_ref, wo_ref, g2_ref, rw_ref, rb_ref, x1_ref, xn_ref, route_ref):
    x1 = (x_ref[...]
          + jnp.dot(ohg_ref[...].astype(bf16), wo_ref[0:HG_WIDTH, :], preferred_element_type=f32)
          + jnp.dot(onsa_ref[...].astype(bf16), wo_ref[HG_WIDTH:, :], preferred_element_type=f32))
    x1_ref[...] = x1
    xn = x1 * lax.rsqrt(jnp.mean(x1 * x1, axis=-1, keepdims=True) + RMS_EPS) * g2_ref[...]
    xn_ref[...] = xn
    logits = jnp.dot(xn.astype(bf16), rw_ref[...], preferred_element_type=f32) + rb_ref[...]
    lane = lax.broadcasted_iota(jnp.int32, logits.shape, 1)
    route = jnp.zeros(logits.shape, f32)
    work = logits
    top = []
    for k in range(TOP_K):
        m = jnp.max(work, axis=-1, keepdims=True)
        idx = jnp.min(jnp.where(work == m, lane, LANE), axis=-1, keepdims=True)
        top.append(m)
        route = jnp.where(lane == k, idx.astype(f32), route)
        work = jnp.where(lane == idx, -jnp.inf, work)
    es = [jnp.exp(t - top[0]) for t in top]
    denom = es[0] + es[1] + es[2] + es[3]
    for k in range(TOP_K):
        route = jnp.where(lane == ROUTE_GATE_LANE + k, es[k] / denom, route)
    route_ref[...] = route


def _outproj_router(x, o_hg, o_nsa, wo_bf16, g2, rw_pad, rb_pad, tm):
    n, d = x.shape
    row = lambda w: pl.BlockSpec((tm, w), lambda i: (i, 0))
    full = lambda a: pl.BlockSpec(a.shape, lambda i: (0, 0))
    g2 = g2.reshape(1, d)
    return pl.pallas_call(
        _outproj_router_kernel,
        grid=(n // tm,),
        in_specs=[row(d), row(HG_WIDTH), row(NSA_WIDTH), full(wo_bf16), full(g2), full(rw_pad), full(rb_pad)],
        out_specs=[row(d), row(d), row(LANE)],
        out_shape=[jax.ShapeDtypeStruct((n, d), f32), jax.ShapeDtypeStruct((n, d), f32),
                   jax.ShapeDtypeStruct((n, LANE), f32)],
        compiler_params=_cparams(("parallel",), V7X_VMEM_LIMIT),
        name="outproj_router",
    )(x, o_hg, o_nsa, wo_bf16, g2, rw_pad, rb_pad)


MOE_ROWS = 1088
MOE_REGION = 544
MOE_TF = 256
ROW_DMA_UNROLL = 8


def _row_copies(n_rows, copy_fn):
    def start(r, c):
        for k in range(TOP_K):
            copy_fn(r, k).start(priority=k % 2)
        return c

    def wait(r, c):
        for k in range(TOP_K):
            copy_fn(r, k).wait()
        return c

    lax.fori_loop(0, n_rows, start, 0, unroll=ROW_DMA_UNROLL)
    lax.fori_loop(0, n_rows, wait, 0, unroll=ROW_DMA_UNROLL)


def _dispatch_kernel(slot_ref, xn_ref, xs_in_ref, xs_ref, sem):
    del xs_in_ref
    tb = xn_ref.shape[0]

    def copy(r, k):
        return pltpu.make_async_copy(xn_ref.at[pl.ds(r, 1)], xs_ref.at[pl.ds(slot_ref[r * TOP_K + k], 1)], sem)

    _row_copies(tb, copy)


def _dispatch(slots_flat, xn, xs, tb):
    n, d = xn.shape
    return pl.pallas_call(
        _dispatch_kernel,
        grid=(n // tb,),
        in_specs=[pl.BlockSpec((tb * TOP_K,), lambda i: (i,), memory_space=pltpu.SMEM),
                  pl.BlockSpec((tb, d), lambda i: (i, 0)),
                  pl.BlockSpec(memory_space=pl.ANY)],
        out_specs=pl.BlockSpec(memory_space=pl.ANY),
        out_shape=jax.ShapeDtypeStruct(xs.shape, xs.dtype),
        scratch_shapes=[pltpu.SemaphoreType.DMA(())],
        input_output_aliases={2: 0},
        compiler_params=_cparams(("arbitrary",)),
        name="moe_dispatch",
    )(slots_flat, xn, xs)


def _expert_kernel(ie_ref, ir_ref, x_ref, w1g_ref, w1u_ref, b1g_ref, b1u_ref, w2_ref, b2_ref, y_ref):
    m = pl.program_id(0)
    j = pl.program_id(1)
    rows = ir_ref[m]

    @pl.when(j == 0)
    def _():
        y_ref[...] = jnp.broadcast_to(b2_ref[...], y_ref.shape)

    @pl.when(rows > 0)
    def _():
        w1g = w1g_ref[...].astype(bf16)
        w1u = w1u_ref[...].astype(bf16)
        w2 = w2_ref[...].astype(bf16)
        for start in range(0, MOE_ROWS, MOE_REGION):
            rs = slice(start, min(start + MOE_REGION, MOE_ROWS))

            @pl.when(start < rows)
            def _():
                x = x_ref[rs, :].astype(bf16)
                hg = jnp.dot(x, w1g, preferred_element_type=f32) + b1g_ref[...]
                hu = jnp.dot(x, w1u, preferred_element_type=f32) + b1u_ref[...]
                gl = jnp.minimum(hg, SWIGLU_LIMIT)
                up = jnp.clip(hu, -SWIGLU_LIMIT, SWIGLU_LIMIT)
                act = (up + 1.0) * gl * _sigmoid(SWIGLU_ALPHA * gl)
                y_ref[rs, :] += jnp.dot(act.astype(bf16), w2, preferred_element_type=f32)


def _experts(item_e, item_rows, xs, w1, b1, w2, b2):
    n_items = item_e.shape[0]
    d = xs.shape[1]
    nf = D_FF // MOE_TF
    jj = lambda m, j, ir: jnp.where(ir[m] > 0, j, nf - 1)
    b1 = b1.reshape(N_EXPERTS, 1, 2 * D_FF)
    b2 = b2.reshape(N_EXPERTS, 1, d)
    return pl.pallas_call(
        _expert_kernel,
        grid_spec=pltpu.PrefetchScalarGridSpec(
            num_scalar_prefetch=2,
            grid=(n_items, nf),
            in_specs=[pl.BlockSpec((MOE_ROWS, d), lambda m, j, ie, ir: (m, 0)),
                      pl.BlockSpec((None, d, MOE_TF), lambda m, j, ie, ir: (ie[m], 0, jj(m, j, ir))),
                      pl.BlockSpec((None, d, MOE_TF), lambda m, j, ie, ir: (ie[m], 0, nf + jj(m, j, ir))),
                      pl.BlockSpec((None, 1, MOE_TF), lambda m, j, ie, ir: (ie[m], 0, jj(m, j, ir))),
                      pl.BlockSpec((None, 1, MOE_TF), lambda m, j, ie, ir: (ie[m], 0, nf + jj(m, j, ir))),
                      pl.BlockSpec((None, MOE_TF, d), lambda m, j, ie, ir: (ie[m], jj(m, j, ir), 0)),
                      pl.BlockSpec((None, 1, d), lambda m, j, ie, ir: (ie[m], 0, 0))],
            out_specs=pl.BlockSpec((MOE_ROWS, d), lambda m, j, ie, ir: (m, 0))),
        out_shape=jax.ShapeDtypeStruct(xs.shape, f32),
        compiler_params=_cparams(("arbitrary", "arbitrary"), V7X_VMEM_LIMIT),
        name="moe_experts",
    )(item_e, item_rows, xs, w1, w1, b1, b1, w2, b2)


def _combine_kernel(slot_ref, x1_ref, route_ref, gf_ref, ys_ref, y_ref, buf, sem):
    tb = x1_ref.shape[0]

    def copy(r, k):
        return pltpu.make_async_copy(ys_ref.at[pl.ds(slot_ref[r * TOP_K + k], 1)], buf.at[k, pl.ds(r, 1)], sem)

    _row_copies(tb, copy)
    route = route_ref[...]
    x2 = x1_ref[...]
    for k in range(TOP_K):
        x2 = x2 + route[:, ROUTE_GATE_LANE + k:ROUTE_GATE_LANE + k + 1] * buf[k]
    y_ref[...] = x2 * lax.rsqrt(jnp.mean(x2 * x2, axis=-1, keepdims=True) + RMS_EPS) * gf_ref[...]


def _combine(slots_flat, x1, route, gf, ys, tb):
    n, d = x1.shape
    return pl.pallas_call(
        _combine_kernel,
        grid=(n // tb,),
        in_specs=[pl.BlockSpec((tb * TOP_K,), lambda i: (i,), memory_space=pltpu.SMEM),
                  pl.BlockSpec((tb, d), lambda i: (i, 0)),
                  pl.BlockSpec((tb, LANE), lambda i: (i, 0)),
                  pl.BlockSpec((1, d), lambda i: (0, 0)),
                  pl.BlockSpec(memory_space=pl.ANY)],
        out_specs=pl.BlockSpec((tb, d), lambda i: (i, 0)),
        out_shape=jax.ShapeDtypeStruct((n, d), f32),
        scratch_shapes=[pltpu.VMEM((TOP_K, tb, d), f32), pltpu.SemaphoreType.DMA(())],
        compiler_params=_cparams(("arbitrary",), V7X_VMEM_LIMIT),
        name="moe_combine",
    )(slots_flat, x1, route, gf.reshape(1, d), ys)


def _routing_plan(top_e, n_items):
    flat_e = top_e.reshape(-1)
    onehot = (flat_e[:, None] == jnp.arange(N_EXPERTS, dtype=jnp.int32)[None, :]).astype(jnp.int32)
    csum = jnp.cumsum(onehot, axis=0)
    rank = jnp.sum(onehot * (csum - onehot), axis=1)
    counts = csum[-1]
    padded = (counts + MOE_ROWS - 1) // MOE_ROWS * MOE_ROWS
    pad_end = jnp.cumsum(padded)
    start = pad_end - padded
    slots = (start[flat_e] + rank).astype(jnp.int32)
    row0 = jnp.arange(n_items, dtype=jnp.int32) * MOE_ROWS
    item_e = jnp.minimum(jnp.searchsorted(pad_end, row0, side='right'), N_EXPERTS - 1).astype(jnp.int32)
    item_rows = jnp.clip(counts[item_e] - (row0 - start[item_e]), 0, MOE_ROWS).astype(jnp.int32)
    used = row0 < pad_end[-1]
    last_e = item_e[jnp.maximum(pad_end[-1] // MOE_ROWS - 1, 0)]
    item_e = jnp.where(used, item_e, last_e)
    item_rows = jnp.where(used, item_rows, 0)
    return slots, item_e, item_rows


SUB = 8


def _hgrn_sample_kernel(z_ref, lb_ref, gain_ref, s0_ref, o_ref, s_ref):
    gain = gain_ref[...]
    eye = lax.broadcasted_iota(jnp.int32, (HG_DK, HG_DK), 0) == lax.broadcasted_iota(jnp.int32, (HG_DK, HG_DK), 1)

    def column(rowvec):
        return jnp.sum(jnp.where(eye, jnp.broadcast_to(rowvec, (HG_DK, HG_DK)), 0.0), axis=-1, keepdims=True)

    for h in range(HG_HEADS):
        seg = lambda i, h=h: z_ref[:, i * HG_WIDTH + h * HG_DK:i * HG_WIDTH + (h + 1) * HG_DK]
        cs = slice(h * HG_DK, (h + 1) * HG_DK)
        q, k, g = _hgrn_gates(seg(0), seg(1), lb_ref[:, cs])
        v = seg(2)
        eg = jnp.exp(g)
        s0 = s0_ref[h]
        qe = jnp.broadcast_to(q * eg, (SUB, HG_DK)).astype(bf16)
        o = jnp.sum(q * k, axis=-1, keepdims=True) * v + jnp.dot(qe, s0.astype(bf16), preferred_element_type=f32)[0:1]
        s_ref[h] = column(eg) * s0 + column(k) * v
        o_ref[:, cs] = _hgrn_out(o, gain, seg(3))


def _row3(z):
    return z.reshape(z.shape[0], 1, z.shape[1])


def _hgrn_sample(z, lb, gain, s0):
    bs = z.shape[0]
    o, s = pl.pallas_call(
        _hgrn_sample_kernel,
        grid=(bs,),
        in_specs=[pl.BlockSpec((None, 1, 4 * HG_WIDTH), lambda b: (b, 0, 0)),
                  pl.BlockSpec((1, HG_WIDTH), lambda b: (0, 0)),
                  pl.BlockSpec((1, HG_DV), lambda b: (0, 0)),
                  pl.BlockSpec((None, HG_HEADS, HG_DK, HG_DV), lambda b: (b, 0, 0, 0))],
        out_specs=[pl.BlockSpec((None, 1, HG_WIDTH), lambda b: (b, 0, 0)),
                   pl.BlockSpec((None, HG_HEADS, HG_DK, HG_DV), lambda b: (b, 0, 0, 0))],
        out_shape=[jax.ShapeDtypeStruct((bs, 1, HG_WIDTH), f32), jax.ShapeDtypeStruct(s0.shape, f32)],
        compiler_params=_cparams(("parallel",)),
        name="hgrn_sample",
    )(_row3(z), lb.reshape(1, HG_WIDTH), gain.reshape(1, HG_DV), s0)
    return o.reshape(bs, HG_WIDTH), s


def _q_pad_row(q, g):
    qb = jnp.broadcast_to(q, (SUB, q.shape[1]))
    z = jnp.zeros((SUB, DH), f32)
    row = lax.broadcasted_iota(jnp.int32, (SUB, LANE), 0)
    out = jnp.zeros((SUB, LANE), f32)
    for h in range(HPG):
        c0 = (g * HPG + h) * DH
        piece = jnp.concatenate([qb[:, c0:c0 + DH], z] if g % 2 == 0 else [z, qb[:, c0:c0 + DH]], axis=1)
        out = jnp.where(row == h, piece, out)
    return out.astype(bf16)


def _head_pieces(o, g):
    half = slice((g % 2) * DH, (g % 2 + 1) * DH)
    return [o[h:h + 1, half] for h in range(HPG)]


def _cmp_sample_kernel(q_ref, gate_ref, kc_ref, bias_ref, map_ref, o_ref, idx_ref, *, cur, n_blk_lanes):
    q = q_ref[...]
    gates = _sigmoid(gate_ref[...])
    ns = kc_ref.shape[0]
    pieces = []
    idx_ref[...] = jnp.zeros(idx_ref.shape, jnp.int32)
    r_i = lax.broadcasted_iota(jnp.int32, (n_blk_lanes, n_blk_lanes), 0)
    c_i = lax.broadcasted_iota(jnp.int32, (n_blk_lanes, n_blk_lanes), 1)
    forced_c = (c_i == 0) | ((c_i <= cur) & (c_i > cur - N_LOCAL_BLOCKS))
    slot = lax.broadcasted_iota(jnp.int32, (n_blk_lanes, LANE), 1).astype(f32)
    blk_id = lax.broadcasted_iota(jnp.int32, (n_blk_lanes, LANE), 0)
    for g in range(KVH):
        pair = slice((g // 2) * LANE, (g // 2 + 1) * LANE)
        vpair = slice(KVH * DH + (g // 2) * LANE, KVH * DH + (g // 2 + 1) * LANE)
        s = _dot_nt(_q_pad_row(q, g), kc_ref[:, pair]) * ATTN_SCALE + bias_ref[g]
        valid = s > 0.5 * NEG
        m = jnp.max(s, axis=-1, keepdims=True)
        e = jnp.where(valid, jnp.exp(s - m), 0.0)
        p = e / jnp.maximum(jnp.sum(e, axis=-1, keepdims=True), 1e-30)
        o = jnp.dot(p.astype(bf16), kc_ref[:, vpair], preferred_element_type=f32)
        for h, oh in enumerate(_head_pieces(o, g)):
            col = (g * HPG + h) * 3
            pieces.append(gates[:, col:col + 1] * oh)
        pg = jnp.broadcast_to(jnp.sum(p[0:HPG], axis=0, keepdims=True), (SUB, ns))
        p_slc = jnp.zeros((SUB, n_blk_lanes), f32)
        for part in _split3(pg):
            p_slc = p_slc + jnp.dot(part, map_ref[...], preferred_element_type=f32)
        a = jnp.broadcast_to(p_slc[0:1], (n_blk_lanes, n_blk_lanes))
        a = jnp.where(forced_c, jnp.inf, jnp.where(c_i > cur, -jnp.inf, a))
        bt = a.T
        ahead = (a > bt) | ((a == bt) & (c_i < r_i))
        rank = jnp.sum(jnp.where(ahead, 1.0, 0.0), axis=-1, keepdims=True)
        hit = (rank == slot) & (blk_id <= cur)
        chosen = jnp.sum(jnp.where(hit, blk_id.astype(f32), 0.0), axis=0, keepdims=True)
        idx_ref[g:g + 1, :] = chosen.astype(jnp.int32)
    o_ref[...] = jnp.concatenate(pieces, axis=1)


def _cmp_sample(z, kc, rel_bias, q_pos):
    bs = z.shape[0]
    ns = kc.shape[1]
    nc = ns - (CMP_LEN // CMP_STRIDE - 1)
    n_blk = -(-(q_pos + 1) // SLC_BLOCK)
    assert n_blk >= N_SEL
    n_blk_lanes = -(-n_blk // LANE) * LANE
    k_end = np.arange(ns) * CMP_STRIDE + CMP_LEN - 1
    dist = q_pos - k_end
    bias = _bias_by_dist(rel_bias, dist)
    bias = jnp.where(jnp.asarray((dist >= 0) & (np.arange(ns) < nc))[None], bias, NEG)
    bias = jnp.pad(bias.reshape(KVH, HPG, ns), ((0, 0), (0, SUB - HPG), (0, 0)))
    smap = jnp.asarray(_stride_to_block_map(ns, n_blk_lanes, 0)).astype(bf16)
    kern = functools.partial(_cmp_sample_kernel, cur=q_pos // SLC_BLOCK, n_blk_lanes=n_blk_lanes)
    z3 = _row3(z)
    o, idx = pl.pallas_call(
        kern,
        grid=(bs,),
        in_specs=[pl.BlockSpec((None, 1, NSA_WIDTH), lambda b: (b, 0, COL_Q // NSA_WIDTH)),
                  pl.BlockSpec((None, 1, LANE), lambda b: (b, 0, COL_GATE // LANE)),
                  pl.BlockSpec((None, ns, KV_WIDTH), lambda b: (b, 0, 0)),
                  pl.BlockSpec(bias.shape, lambda b: (0, 0, 0)),
                  pl.BlockSpec(smap.shape, lambda b: (0, 0))],
        out_specs=[pl.BlockSpec((None, 1, NSA_WIDTH), lambda b: (b, 0, 0)),
                   pl.BlockSpec((None, SUB, LANE), lambda b: (b, 0, 0))],
        out_shape=[jax.ShapeDtypeStruct((bs, 1, NSA_WIDTH), f32), jax.ShapeDtypeStruct((bs, SUB, LANE), jnp.int32)],
        compiler_params=_cparams(("parallel",)),
        name="nsa_cmp_sample",
    )(z3, z3, kc, bias, smap)
    return o.reshape(bs, NSA_WIDTH), idx


def _q_rows(q, g):
    qb = jnp.broadcast_to(q, (SUB, q.shape[1]))
    row = lax.broadcasted_iota(jnp.int32, (SUB, DH), 0)
    out = jnp.zeros((SUB, DH), f32)
    for h in range(HPG):
        c0 = (g * HPG + h) * DH
        out = jnp.where(row == h, qb[:, c0:c0 + DH], out)
    return out


def _column(rowvec):
    n = rowvec.shape[1]
    eye = lax.broadcasted_iota(jnp.int32, (n, n), 0) == lax.broadcasted_iota(jnp.int32, (n, n), 1)
    return jnp.sum(jnp.where(eye, jnp.broadcast_to(rowvec, (n, n)), 0.0), axis=-1, keepdims=True)


SEL_PER_STEP = 4


def _slcwin_sample_kernel(idx_ref, pt_ref, q_ref, gate_ref, ocmp_ref, ksn_ref, kwn_ref, *refs, past, bpp):
    del pt_ref
    n_pool_refs = SEL_PER_STEP * KVH
    pools = refs[:n_pool_refs]
    win_ref, bslc_ref, bwin_ref, bnew_ref, o_ref, nwin_ref, qr_s, m_s, l_s, acc_s, ow_s = refs[n_pool_refs:]
    b = pl.program_id(0)
    k = pl.program_id(1)
    wlen = win_ref.shape[3]
    page_rows = pools[0].shape[2]
    half_w = KVH * DH

    @pl.when(k == 0)
    def _():
        q = q_ref[...] * ATTN_SCALE
        wnew = kwn_ref[...]
        lane = lax.broadcasted_iota(jnp.int32, (DH, wlen), 1)
        for g in range(KVH):
            knew = wnew[:, g * DH:(g + 1) * DH]
            vnew = wnew[:, half_w + g * DH:half_w + (g + 1) * DH]
            kt = win_ref[0, g]
            vt = win_ref[1, g]
            nwin_ref[0, g] = jnp.where(lane == wlen - 1, _column(knew), pltpu.roll(kt, wlen - 1, 1))
            nwin_ref[1, g] = jnp.where(lane == wlen - 1, _column(vnew), pltpu.roll(vt, wlen - 1, 1))
            qr = _q_rows(q, g).astype(bf16)
            qr_s[g] = qr
            m_s[g] = jnp.full((SUB, 1), NEG, f32)
            l_s[g] = jnp.zeros((SUB, 1), f32)
            acc_s[g] = jnp.zeros((SUB, DH), f32)
            s1 = jnp.dot(qr, kt.astype(bf16), preferred_element_type=f32) + bwin_ref[g]
            s2 = jnp.sum(qr.astype(f32) * knew.astype(bf16).astype(f32), axis=-1, keepdims=True) + bnew_ref[g][:, 0:1]
            ok = s1 > 0.5 * NEG
            mx = jnp.maximum(jnp.max(s1, axis=-1, keepdims=True), s2)
            e1 = jnp.where(ok, jnp.exp(s1 - mx), 0.0)
            e2 = jnp.exp(s2 - mx)
            den = jnp.maximum(jnp.sum(e1, axis=-1, keepdims=True) + e2, 1e-30)
            ow_s[g] = (_dot_nt(e1.astype(bf16), vt.astype(bf16))
                       + e2.astype(bf16).astype(f32) * vnew.astype(bf16).astype(f32)) / den

    snew = ksn_ref[...]
    lane_k = lax.broadcasted_iota(jnp.int32, (DH, page_rows), 1)
    lane_s = lax.broadcasted_iota(jnp.int32, (SUB, page_rows), 1)
    for g in range(KVH):
        kcol = _column(snew[:, g * DH:(g + 1) * DH])
        vcol = _column(snew[:, half_w + g * DH:half_w + (g + 1) * DH])
        m, l, acc = m_s[g], l_s[g], acc_s[g]
        for kk in range(SEL_PER_STEP):
            blk = idx_ref[(b * KVH + g) * N_SEL + k * SEL_PER_STEP + kk]
            page = blk // bpp
            tile = pools[kk * KVH + g]
            fresh = page * page_rows + lane_k >= past
            kt = jnp.where(fresh, kcol, tile[0]).astype(bf16)
            vt = jnp.where(fresh, vcol, tile[1]).astype(bf16)
            s = jnp.dot(qr_s[g], kt, preferred_element_type=f32) + bslc_ref[page, g]
            kpos = page * page_rows + lane_s
            mask = (lane_s // SLC_BLOCK == blk % bpp) & (kpos <= past)
            s = jnp.where(mask, s, NEG)
            m_new = jnp.maximum(m, jnp.max(s, axis=-1, keepdims=True))
            alpha = jnp.exp(m - m_new)
            p = jnp.where(mask, jnp.exp(s - m_new), 0.0)
            l = alpha * l + jnp.sum(p, axis=-1, keepdims=True)
            acc = alpha * acc + _dot_nt(p.astype(bf16), vt)
            m = m_new
        m_s[g], l_s[g], acc_s[g] = m, l, acc

    @pl.when(k == pl.num_programs(1) - 1)
    def _():
        gates = _sigmoid(gate_ref[...])
        pieces = []
        for g in range(KVH):
            o_sl = acc_s[g] / jnp.maximum(l_s[g], 1e-30)
            o_w = ow_s[g]
            for h in range(HPG):
                col = (g * HPG + h) * 3
                pieces.append(gates[:, col + 1:col + 2] * o_sl[h:h + 1] + gates[:, col + 2:col + 3] * o_w[h:h + 1])
        o_ref[...] = ocmp_ref[...] + jnp.concatenate(pieces, axis=1)


def _slcwin_sample(z, ocmp, idx, pool_t, page_table, win_t, rel_bias, past):
    bs = z.shape[0]
    n_pages = page_table.shape[1]
    page_rows = pool_t.shape[4]
    bpp = page_rows // SLC_BLOCK
    wlen = win_t.shape[4]
    kpos = np.arange(n_pages + 1)[:, None] * page_rows + np.arange(page_rows)[None, :]
    bslc = _bias_by_dist(rel_bias, past - kpos)
    bslc = jnp.pad(bslc.reshape(KVH, HPG, n_pages + 1, page_rows),
                   ((0, 0), (0, SUB - HPG), (0, 0), (0, 0))).transpose(2, 0, 1, 3)
    wpos = past - wlen + np.arange(wlen)
    wdist = past - wpos
    bwin = jnp.where(jnp.asarray((wdist < WINDOW) & (wpos >= 0))[None], _bias_by_dist(rel_bias, wdist), NEG)
    bwin = jnp.pad(bwin.reshape(KVH, HPG, wlen), ((0, 0), (0, SUB - HPG), (0, 0)))
    bnew = jnp.broadcast_to(_bias_by_dist(rel_bias, np.zeros((1,), np.int64)).reshape(KVH, HPG, 1), (KVH, HPG, LANE))
    bnew = jnp.pad(bnew, ((0, 0), (0, SUB - HPG), (0, 0)))

    def pool_map(kk, g):
        def f(b, k, idx_r, pt_r):
            blk = idx_r[(b * KVH + g) * N_SEL + k * SEL_PER_STEP + kk]
            return (pt_r[b * n_pages + jnp.minimum(blk // bpp, n_pages - 1)], 0, g, 0, 0)
        return f

    rowblk = lambda w, c: pl.BlockSpec((None, 1, w), lambda b, k, i, p, c=c: (b, 0, c))
    full = lambda a: pl.BlockSpec(a.shape, lambda b, k, i, p: (0,) * a.ndim)
    win_spec = pl.BlockSpec((None, 2, KVH, DH, wlen), lambda b, k, i, p: (b, 0, 0, 0, 0))
    kern = functools.partial(_slcwin_sample_kernel, past=past, bpp=bpp)
    z3 = _row3(z)
    o, new_win_t = pl.pallas_call(
        kern,
        grid_spec=pltpu.PrefetchScalarGridSpec(
            num_scalar_prefetch=2,
            grid=(bs, N_SEL // SEL_PER_STEP),
            in_specs=[rowblk(NSA_WIDTH, COL_Q // NSA_WIDTH), rowblk(LANE, COL_GATE // LANE),
                      rowblk(NSA_WIDTH, 0),
                      rowblk(KV_WIDTH, COL_KVS // KV_WIDTH), rowblk(KV_WIDTH, COL_KVW // KV_WIDTH)]
                     + [pl.BlockSpec((None, 2, None, DH, page_rows), pool_map(kk, g))
                        for kk in range(SEL_PER_STEP) for g in range(KVH)]
                     + [win_spec, full(bslc), full(bwin), full(bnew)],
            out_specs=[rowblk(NSA_WIDTH, 0), win_spec],
            scratch_shapes=[pltpu.VMEM((KVH, SUB, DH), bf16), pltpu.VMEM((KVH, SUB, 1), f32),
                            pltpu.VMEM((KVH, SUB, 1), f32), pltpu.VMEM((KVH, SUB, DH), f32),
                            pltpu.VMEM((KVH, SUB, DH), f32)]),
        out_shape=[jax.ShapeDtypeStruct((bs, 1, NSA_WIDTH), f32), jax.ShapeDtypeStruct(win_t.shape, f32)],
        compiler_params=_cparams(("parallel", "arbitrary"), V7X_VMEM_LIMIT),
        name="nsa_slcwin_sample",
    )(idx[:, :KVH, :N_SEL].reshape(-1), page_table.reshape(-1).astype(jnp.int32), z3, z3, _row3(ocmp), z3, z3,
      *([pool_t] * (SEL_PER_STEP * KVH)), win_t, bslc, bwin, bnew)
    return o.reshape(bs, NSA_WIDTH), new_win_t


def kernel(x_prompt, x_sample, cache_kv_cmp, cache_kv_slc, state_win_kv, state_hgrn, page_table, norm1, w_in, hg_lower_bound, hg_norm, cmp_pe, cmp_w1, cmp_w2, rel_bias, w_out, norm2, router_w, router_b, moe_w1, moe_b1, moe_w2, moe_b2, norm_f):
    batch, seq, d = x_prompt.shape
    bs, dec_seq, _ = x_sample.shape
    assert norm1.shape[0] == 1 and dec_seq == 1
    n_pool, page_rows = cache_kv_cmp.shape[1:3]
    n_pages = page_table.shape[1]
    past = n_pages * page_rows
    wlen = state_win_kv.shape[2]
    assert wlen == WINDOW and past % CMP_STRIDE == 0 and seq % page_rows == 0

    lb = jnp.cumsum(jax.nn.softmax(hg_lower_bound.astype(f32), axis=0), axis=0)[0]
    w_in_p = jnp.pad(w_in[0], ((0, 0), (0, Z_WIDTH - IN_WIDTH))).astype(bf16)
    xp = x_prompt.reshape(batch * seq, d)
    xs = x_sample.reshape(bs, d)
    tq = min(128, seq)

    zp = _in_proj(xp, norm1[0], w_in_p, min(512, batch * seq))
    o_hg_p, s_p = _hgrn_prompt(zp, lb, hg_norm[0], batch, seq, min(256, seq))
    kvc_p = zp[:, COL_KVC:COL_KVC + KV_WIDTH]
    kvs_p = zp[:, COL_KVS:COL_KVS + KV_WIDTH]
    kvw_p = zp[:, COL_KVW:COL_KVW + KV_WIDTH]
    ident = jnp.arange(batch * seq // page_rows, dtype=jnp.int32).reshape(batch, seq // page_rows)
    kc_p = _compress(kvc_p.reshape(-1, page_rows, KV_WIDTH), ident, page_rows, cmp_pe[0], cmp_w1[0], cmp_w2[0])
    ocmp_p, sel = _cmp_prompt(zp, kc_p, rel_bias, batch, seq, tq)
    o_nsa_p = _slcwin_prompt(zp, sel, ocmp_p, rel_bias, batch, seq, min(256, seq))

    rows_minor = lambda a: jnp.transpose(a, (0, 2, 3, 4, 1))
    zs = _in_proj(xs, norm1[0], w_in_p, bs)
    o_hg_s, s_s = _hgrn_sample(zs, lb, hg_norm[0], state_hgrn[0])
    kc_s = _compress_paged(rows_minor(cache_kv_cmp[0]).reshape(n_pool, 2, KVH * DH, page_rows), page_table,
                           cmp_pe[0], cmp_w1[0], cmp_w2[0])
    ocmp_s, idx = _cmp_sample(zs, kc_s, rel_bias, past)
    o_nsa_s, new_win_t = _slcwin_sample(zs, ocmp_s, idx, rows_minor(cache_kv_slc[0]), page_table,
                                        rows_minor(state_win_kv[0]), rel_bias, past)
    new_win = jnp.transpose(new_win_t, (0, 4, 1, 2, 3))

    wo = w_out[0].astype(bf16)
    rw = jnp.pad(router_w[0], ((0, 0), (0, LANE - N_EXPERTS))).astype(bf16)
    rb = jnp.pad(router_b[0].astype(f32), (0, LANE - N_EXPERTS), constant_values=NEG).reshape(1, LANE)
    x1_p, xn_p, route_p = _outproj_router(xp, o_hg_p, o_nsa_p, wo, norm2[0], rw, rb, min(256, batch * seq))
    x1_s, xn_s, route_s = _outproj_router(xs, o_hg_s, o_nsa_s, wo, norm2[0], rw, rb, bs)
    top_e = jnp.concatenate([route_p[:, :TOP_K], route_s[:, :TOP_K]], axis=0).astype(jnp.int32)
    n_tok = batch * seq + bs
    n_items = -(-n_tok * TOP_K // MOE_ROWS) + N_EXPERTS
    slots, item_e, item_rows = _routing_plan(top_e, n_items)
    slots_p, slots_s = slots[:batch * seq * TOP_K], slots[batch * seq * TOP_K:]
    xsort = jnp.zeros((n_items * MOE_ROWS, d), f32)
    xsort = _dispatch(slots_p, xn_p, xsort, min(256, batch * seq))
    xsort = _dispatch(slots_s, xn_s, xsort, bs)
    ysort = _experts(item_e, item_rows, xsort, moe_w1[0], moe_b1[0], moe_w2[0], moe_b2[0])
    y_p = _combine(slots_p, x1_p, route_p, norm_f, ysort, min(128, batch * seq))
    y_s = _combine(slots_s, x1_s, route_s, norm_f, ysort, bs)

    kv5 = lambda a, n, t: a.reshape(1, n, t, 2, KVH, DH)
    win_p = kvw_p.reshape(batch, seq, KV_WIDTH)[:, seq - min(WINDOW, seq):]
    return (y_p.reshape(batch, seq, d), y_s.reshape(bs, 1, d),
            kv5(kvc_p, batch, seq), kv5(kvs_p, batch, seq), kv5(win_p, batch, min(WINDOW, seq)), s_p[None],
            kv5(zs[:, COL_KVC:COL_KVC + KV_WIDTH], bs, 1), kv5(zs[:, COL_KVS:COL_KVS + KV_WIDTH], bs, 1),
            kv5(new_win, bs, wlen), s_s[None])
```

```python
import functools
import math

import jax
import jax.numpy as jnp
import numpy as np
from jax import lax
from jax.experimental import pallas as pl
from jax.experimental.pallas import tpu as pltpu

f32 = jnp.float32
bf16 = jnp.bfloat16

HG_HEADS, HG_DK, HG_DV = 8, 128, 128
HG_STEP = 16
NSA_HEADS, KVH, DH = 16, 4, 64
HPG = NSA_HEADS // KVH
CMP_LEN, CMP_STRIDE, CMP_HIDDEN = 32, 16, 128
SLC_BLOCK, N_SEL, N_LOCAL_BLOCKS, WINDOW = 64, 16, 2, 512
ATTN_SCALE = DH ** -0.5
NUM_BUCKETS, MAX_DISTANCE = 32, 128
N_EXPERTS, TOP_K, D_FF = 32, 4, 2048
SWIGLU_ALPHA, SWIGLU_LIMIT = 1.702, 7.0
RMS_EPS = 1e-5

HG_WIDTH = HG_HEADS * HG_DV
NSA_WIDTH = NSA_HEADS * DH
KV_WIDTH = 2 * KVH * DH
IN_SPLITS = (HG_WIDTH, HG_WIDTH, HG_WIDTH, HG_WIDTH, NSA_WIDTH, KV_WIDTH, KV_WIDTH, KV_WIDTH, NSA_HEADS * 3)
IN_WIDTH = sum(IN_SPLITS)
Z_WIDTH = 7168
COL_Q, COL_KVC, COL_KVS, COL_KVW, COL_GATE = 4096, 5120, 5632, 6144, 6656
LANE = 128
NEG = -1e30

V7X_VMEM_LIMIT = 56 * 1024 * 1024


def _cparams(sem, vmem=None):
    return pltpu.CompilerParams(dimension_semantics=sem, vmem_limit_bytes=vmem)


def _sigmoid(x):
    return 1.0 / (1.0 + jnp.exp(-x))


def _silu(x):
    return x * _sigmoid(x)


def _proj_kernel(x_ref, g_ref, w_ref, z_ref, hn_ref):
    @pl.when(pl.program_id(1) == 0)
    def _():
        x = x_ref[...]
        y = x * lax.rsqrt(jnp.mean(x * x, axis=-1, keepdims=True) + RMS_EPS) * g_ref[...]
        hn_ref[...] = y.astype(bf16)

    z_ref[...] = jnp.dot(hn_ref[...], w_ref[...], preferred_element_type=f32)


def _in_proj(x, gain, w_bf16, tm):
    n, d = x.shape
    tn = Z_WIDTH // 4
    return pl.pallas_call(
        _proj_kernel,
        grid=(n // tm, Z_WIDTH // tn),
        in_specs=[pl.BlockSpec((tm, d), lambda i, j: (i, 0)),
                  pl.BlockSpec((1, d), lambda i, j: (0, 0)),
                  pl.BlockSpec((d, tn), lambda i, j: (0, j))],
        out_specs=pl.BlockSpec((tm, tn), lambda i, j: (i, j)),
        out_shape=jax.ShapeDtypeStruct((n, Z_WIDTH), f32),
        scratch_shapes=[pltpu.VMEM((tm, d), bf16)],
        compiler_params=_cparams(("parallel", "arbitrary"), V7X_VMEM_LIMIT),
        name="in_proj",
    )(x, gain.reshape(1, d), w_bf16)


def _hgrn_gates(q_raw, f_raw, lb):
    q = _silu(q_raw)
    f = lb + (1.0 - lb) * _sigmoid(f_raw)
    return q, 1.0 - f, jnp.log(f)


def _hgrn_out(o, gain, g_raw):
    y = o * lax.rsqrt(jnp.mean(o * o, axis=-1, keepdims=True) + RMS_EPS) * gain
    return y * _silu(g_raw)


def _hgrn_prompt_kernel(q_ref, f_ref, i_ref, g_ref, lb_ref, gain_ref, o_ref, s_ref, st_ref):
    tb = pl.program_id(1)
    n_steps = q_ref.shape[0] // HG_STEP

    @pl.when(tb == 0)
    def _():
        st_ref[...] = jnp.zeros_like(st_ref)

    row = lax.broadcasted_iota(jnp.int32, (HG_STEP, HG_DK), 0)
    gain = gain_ref[...]

    def step(c, carry):
        r0 = pl.multiple_of(c * HG_STEP, HG_STEP)
        for h in range(HG_HEADS):
            cs = slice(h * HG_DK, (h + 1) * HG_DK)
            q, k, g = _hgrn_gates(q_ref[pl.ds(r0, HG_STEP), cs], f_ref[pl.ds(r0, HG_STEP), cs], lb_ref[:, cs])
            v = i_ref[pl.ds(r0, HG_STEP), cs]
            b = g
            for sh in (1, 2, 4, 8):
                b = b + jnp.where(row >= sh, pltpu.roll(b, sh, 0), 0.0)
            b_last = b[HG_STEP - 1:HG_STEP, :]
            st = st_ref[h]
            o = lax.dot_general((q * jnp.exp(b)).astype(bf16), st.astype(bf16),
                                (((1,), (1,)), ((), ())), preferred_element_type=f32)
            half = HG_STEP // 2
            parts = [(q[:half], b[:half], jnp.zeros((half, HG_DV), f32)), (q[half:], b[half:], jnp.zeros((half, HG_DV), f32))]
            for s in range(HG_STEP):
                for ti in range(s // half, 2):
                    qt, bt, ot = parts[ti]
                    p = qt * k[s:s + 1, :] * jnp.exp(bt - b[s:s + 1, :])
                    if s // half == ti:
                        p = jnp.where(row[:half] >= s - ti * half, p, 0.0)
                    parts[ti] = (qt, bt, ot + jnp.sum(p, axis=-1, keepdims=True) * v[s:s + 1, :])
            o = o + jnp.concatenate([parts[0][2], parts[1][2]], axis=0)
            kd = k * jnp.exp(b_last - b)
            st_ref[h] = jnp.exp(b_last) * st + lax.dot_general(
                v.astype(bf16), kd.astype(bf16), (((0,), (0,)), ((), ())), preferred_element_type=f32)
            o_ref[pl.ds(r0, HG_STEP), cs] = _hgrn_out(o, gain, g_ref[pl.ds(r0, HG_STEP), cs])
        return carry

    lax.fori_loop(0, n_steps, step, 0)

    @pl.when(tb == pl.num_programs(1) - 1)
    def _():
        for h in range(HG_HEADS):
            s_ref[h] = st_ref[h].T


def _hgrn_prompt(z, lb, gain, batch, seq, tt):
    nt = seq // tt
    blk = lambda seg: pl.BlockSpec((tt, HG_WIDTH), lambda b, t, seg=seg: (b * nt + t, seg))
    return pl.pallas_call(
        _hgrn_prompt_kernel,
        grid=(batch, nt),
        in_specs=[blk(0), blk(1), blk(2), blk(3),
                  pl.BlockSpec((1, HG_WIDTH), lambda b, t: (0, 0)),
                  pl.BlockSpec((1, HG_DV), lambda b, t: (0, 0))],
        out_specs=[pl.BlockSpec((tt, HG_WIDTH), lambda b, t: (b * nt + t, 0)),
                   pl.BlockSpec((None, HG_HEADS, HG_DK, HG_DV), lambda b, t: (b, 0, 0, 0))],
        out_shape=[jax.ShapeDtypeStruct((batch * seq, HG_WIDTH), f32),
                   jax.ShapeDtypeStruct((batch, HG_HEADS, HG_DK, HG_DV), f32)],
        scratch_shapes=[pltpu.VMEM((HG_HEADS, HG_DV, HG_DK), f32)],
        compiler_params=_cparams(("parallel", "arbitrary")),
        name="hgrn_prompt",
    )(z, z, z, z, lb.reshape(1, HG_WIDTH), gain.reshape(1, HG_DV))


def _bucket_table(max_dist):
    n = np.arange(max_dist + 1)
    max_exact = NUM_BUCKETS // 2

    def large(dtype):
        nf = np.maximum(n, 1).astype(dtype)
        v = np.log(nf / dtype(max_exact)) / dtype(math.log(MAX_DISTANCE / max_exact)) * dtype(NUM_BUCKETS - max_exact)
        return np.minimum(max_exact + v.astype(np.int32), NUM_BUCKETS - 1)

    lo, hi = large(np.float32), large(np.float64)
    assert (lo == hi).all(), "bucket boundaries must not depend on float rounding"
    return np.where(n < max_exact, n, lo).astype(np.int32)


def _bias_kernel(rb_ref, dist_ref, o_ref, *, thresholds):
    h = pl.program_id(0)
    dist = dist_ref[...]
    acc = jnp.full(dist.shape, rb_ref[h, 0], f32)
    for k, thr in thresholds:
        acc = jnp.where(dist >= thr, rb_ref[h, k], acc)
    o_ref[...] = acc


def _bias_by_dist(rel_bias, dist):
    dist = np.maximum(np.asarray(dist), 0).astype(np.int32)
    shape = dist.shape
    dist2 = dist.reshape(-1, shape[-1])
    table = _bucket_table(int(dist.max()))
    assert (np.diff(table) >= 0).all()
    thresholds = tuple((k, int(np.argmax(table >= k))) for k in range(1, NUM_BUCKETS) if (table >= k).any())
    n_heads = rel_bias.shape[0]
    out = pl.pallas_call(
        functools.partial(_bias_kernel, thresholds=thresholds),
        grid=(n_heads,),
        in_specs=[pl.BlockSpec(memory_space=pltpu.SMEM),
                  pl.BlockSpec(dist2.shape, lambda h: (0, 0))],
        out_specs=pl.BlockSpec((None,) + dist2.shape, lambda h: (h, 0, 0)),
        out_shape=jax.ShapeDtypeStruct((n_heads,) + dist2.shape, f32),
        compiler_params=_cparams(("parallel",)),
        name="rel_bias_table",
    )(rel_bias.astype(f32), jnp.asarray(dist2))
    return out.reshape((n_heads,) + shape)


def _split3(x):
    hi = x.astype(bf16)
    r1 = x - hi.astype(f32)
    mid = r1.astype(bf16)
    lo = (r1 - mid.astype(f32)).astype(bf16)
    return hi, mid, lo


def _dot_nt(a, b):
    return lax.dot_general(a, b, (((1,), (1,)), ((), ())), preferred_element_type=f32)


def _q_pad(q_ref_or_val, g, rows):
    q = q_ref_or_val
    z = jnp.zeros((rows, DH), f32)
    parts = []
    for h in range(HPG):
        c0 = (g * HPG + h) * DH
        qh = q[:, c0:c0 + DH]
        parts.append(jnp.concatenate([qh, z] if g % 2 == 0 else [z, qh], axis=1))
    return jnp.concatenate(parts, axis=0).astype(bf16)


def _compress_compute(xbuf, pe_ref, w1_ref, w1bd_ref, w2bd_ref, o_ref):
    ns = xbuf.shape[1]
    for c in range(2):
        pe_term = jnp.dot(pe_ref[c].astype(bf16), w1_ref[c], preferred_element_type=f32)
        pe_pair = jnp.concatenate([pe_term, pe_term], axis=1)
        for pr in range(KVH // 2):
            lanes = slice(c * KVH * DH + pr * LANE, c * KVH * DH + (pr + 1) * LANE)
            xs = jnp.concatenate([xbuf[s, :, lanes].astype(bf16) for s in range(CMP_STRIDE)], axis=1)
            acc = jnp.dot(xs, w1bd_ref[c], preferred_element_type=f32)
            hid = pe_pair + acc[:, :2 * CMP_HIDDEN] + pltpu.roll(acc[:, 2 * CMP_HIDDEN:], ns - 1, 0)
            o_ref[:, lanes] = jnp.dot(_silu(hid).astype(bf16), w2bd_ref[c], preferred_element_type=f32).astype(bf16)


def _stride_perm(page_rows):
    spp = page_rows // CMP_STRIDE
    perm = np.zeros((page_rows, page_rows), np.float32)
    for s in range(CMP_STRIDE):
        for n in range(spp):
            perm[s * spp + n, n * CMP_STRIDE + s] = 1.0
    return jnp.asarray(perm).astype(bf16)


def _scatter_page(xbuf, xp, r0, spp, lanes):
    for s in range(CMP_STRIDE):
        xbuf[s, pl.ds(r0, spp), lanes] = xp[s * spp:(s + 1) * spp, :]


def _compress_kernel(pt_ref, page_ref, perm_ref, pe_ref, w1_ref, w1bd_ref, w2bd_ref, o_ref, xbuf):
    p = pl.program_id(1)
    spp = page_ref.shape[0] // CMP_STRIDE
    xp = jnp.dot(perm_ref[...], page_ref[...].astype(bf16), preferred_element_type=f32)
    _scatter_page(xbuf, xp, pl.multiple_of(p * spp, spp), spp, slice(None))

    @pl.when(p == pl.num_programs(1) - 1)
    def _():
        _compress_compute(xbuf, pe_ref, w1_ref, w1bd_ref, w2bd_ref, o_ref)


PAGES_PER_STEP = 8


def _compress_paged_kernel(pt_ref, *refs):
    page_refs = refs[:PAGES_PER_STEP]
    perm_ref, pe_ref, w1_ref, w1bd_ref, w2bd_ref, o_ref, xbuf = refs[PAGES_PER_STEP:]
    p = pl.program_id(1)
    page_rows = page_refs[0].shape[2]
    spp = page_rows // CMP_STRIDE
    perm = perm_ref[...]
    for pi, page_ref in enumerate(page_refs):
        r0 = pl.multiple_of((p * PAGES_PER_STEP + pi) * spp, spp)
        for c in range(2):
            for pr in range(KVH // 2):
                xp = _dot_nt(perm, page_ref[c, pr * LANE:(pr + 1) * LANE, :].astype(bf16))
                _scatter_page(xbuf, xp, r0, spp, slice(c * KVH * DH + pr * LANE, c * KVH * DH + (pr + 1) * LANE))

    @pl.when(p == pl.num_programs(1) - 1)
    def _():
        _compress_compute(xbuf, pe_ref, w1_ref, w1bd_ref, w2bd_ref, o_ref)


def _compress_paged(pool_t, page_table, cmp_pe, cmp_w1, cmp_w2):
    batch, n_pages = page_table.shape
    page_rows = pool_t.shape[3]
    assert n_pages % PAGES_PER_STEP == 0 and page_rows == LANE
    ns = n_pages * page_rows // CMP_STRIDE
    pe, w1, w1bd, w2bd = _compress_weights(cmp_pe, cmp_w1, cmp_w2)
    perm = _stride_perm(page_rows)
    full = lambda a: pl.BlockSpec(a.shape, lambda b, p, pt: (0,) * a.ndim)
    page = lambda pi: pl.BlockSpec((None, 2, KVH * DH, page_rows),
                                   lambda b, p, pt, pi=pi: (pt[b * n_pages + p * PAGES_PER_STEP + pi], 0, 0, 0))
    return pl.pallas_call(
        _compress_paged_kernel,
        grid_spec=pltpu.PrefetchScalarGridSpec(
            num_scalar_prefetch=1,
            grid=(batch, n_pages // PAGES_PER_STEP),
            in_specs=[page(pi) for pi in range(PAGES_PER_STEP)]
                     + [full(perm), full(pe), full(w1), full(w1bd), full(w2bd)],
            out_specs=pl.BlockSpec((None, ns, KV_WIDTH), lambda b, p, pt: (b, 0, 0)),
            scratch_shapes=[pltpu.VMEM((CMP_STRIDE, ns, KV_WIDTH), f32)]),
        out_shape=jax.ShapeDtypeStruct((batch, ns, KV_WIDTH), bf16),
        compiler_params=_cparams(("parallel", "arbitrary"), V7X_VMEM_LIMIT),
        name="nsa_compress_paged",
    )(page_table.reshape(-1).astype(jnp.int32), *([pool_t] * PAGES_PER_STEP), perm, pe, w1, w1bd, w2bd)


def _compress_weights(cmp_pe, cmp_w1, cmp_w2):
    r = CMP_LEN // CMP_STRIDE
    w1r = cmp_w1.reshape(2, r, CMP_STRIDE, DH, CMP_HIDDEN)
    zero = jnp.zeros_like(w1r[:, 0])
    top = jnp.concatenate([w1r[:, 0], zero, w1r[:, 1], zero], axis=-1)
    bot = jnp.concatenate([zero, w1r[:, 0], zero, w1r[:, 1]], axis=-1)
    w1bd = jnp.concatenate([top, bot], axis=2).astype(bf16)
    w1bd = w1bd.reshape(2, CMP_STRIDE * LANE, 4 * CMP_HIDDEN)
    z2 = jnp.zeros_like(cmp_w2)
    w2bd = jnp.concatenate([jnp.concatenate([cmp_w2, z2], axis=-1),
                            jnp.concatenate([z2, cmp_w2], axis=-1)], axis=1).astype(bf16)
    pe = cmp_pe.reshape(2, 1, CMP_LEN * DH)
    return pe, cmp_w1.astype(bf16), w1bd, w2bd


def _compress(pool, page_table, page_rows, cmp_pe, cmp_w1, cmp_w2):
    batch, n_pages = page_table.shape
    ns = n_pages * page_rows // CMP_STRIDE
    pe, w1, w1bd, w2bd = _compress_weights(cmp_pe, cmp_w1, cmp_w2)
    perm = _stride_perm(page_rows)
    full = lambda a: pl.BlockSpec(a.shape, lambda b, p, pt: (0,) * a.ndim)
    return pl.pallas_call(
        _compress_kernel,
        grid_spec=pltpu.PrefetchScalarGridSpec(
            num_scalar_prefetch=1,
            grid=(batch, n_pages),
            in_specs=[pl.BlockSpec((None, page_rows, KV_WIDTH), lambda b, p, pt: (pt[b * n_pages + p], 0, 0)),
                      full(perm), full(pe), full(w1), full(w1bd), full(w2bd)],
            out_specs=pl.BlockSpec((None, ns, KV_WIDTH), lambda b, p, pt: (b, 0, 0)),
            scratch_shapes=[pltpu.VMEM((CMP_STRIDE, ns, KV_WIDTH), f32)]),
        out_shape=jax.ShapeDtypeStruct((batch, ns, KV_WIDTH), bf16),
        compiler_params=_cparams(("parallel", "arbitrary"), V7X_VMEM_LIMIT),
        name="nsa_compress",
    )(page_table.reshape(-1).astype(jnp.int32), pool, perm, pe, w1, w1bd, w2bd)


def _stride_to_block_map(ns, n_lanes, lane0):
    ratio = SLC_BLOCK // CMP_STRIDE
    m = np.zeros((ns, n_lanes), np.float32)
    for n in range(ns - (CMP_LEN // CMP_STRIDE - 1)):
        for st in range(n, n + CMP_LEN // CMP_STRIDE):
            if lane0 + st // ratio < n_lanes:
                m[n, lane0 + st // ratio] += 1.0
    return m


def _rank_select(score, jidx, n_keep):
    rank = jnp.zeros(score.shape, f32)
    for jp in range(score.shape[0]):
        row = score[jp:jp + 1, :]
        ahead = (row > score) | ((row == score) & (jidx > jp))
        rank = rank + jnp.where(ahead, 1.0, 0.0)
    return rank < n_keep


def _cmp_prompt_kernel(q_ref, gate_ref, kc_ref, bias_ref, map_ref, o_ref, sel_ref):
    i = pl.program_id(0)
    tq = q_ref.shape[0]
    ns = kc_ref.shape[0]
    nc = ns - (CMP_LEN // CMP_STRIDE - 1)
    q = q_ref[...]
    gates = _sigmoid(gate_ref[...])
    t_glob = i * tq + lax.broadcasted_iota(jnp.int32, (tq, ns), 0)
    n_idx = lax.broadcasted_iota(jnp.int32, (tq, ns), 1)
    valid1 = (t_glob >= n_idx * CMP_STRIDE + (CMP_LEN - 1)) & (n_idx < nc)
    valid = jnp.concatenate([valid1] * HPG, axis=0)
    pieces = []
    p_slc = jnp.zeros((tq, LANE), f32)
    for g in range(KVH):
        pair = slice((g // 2) * LANE, (g // 2 + 1) * LANE)
        vpair = slice(KVH * DH + (g // 2) * LANE, KVH * DH + (g // 2 + 1) * LANE)
        bias = jnp.concatenate([bias_ref[g * HPG + h] for h in range(HPG)], axis=0)
        s = _dot_nt(_q_pad(q, g, tq), kc_ref[:, pair]) * ATTN_SCALE + bias
        s = jnp.where(valid, s, NEG)
        m = jnp.max(s, axis=-1, keepdims=True)
        e = jnp.where(valid, jnp.exp(s - m), 0.0)
        p = e / jnp.maximum(jnp.sum(e, axis=-1, keepdims=True), 1e-30)
        o = jnp.dot(p.astype(bf16), kc_ref[:, vpair], preferred_element_type=f32)
        pg = p[0:tq]
        for h in range(HPG):
            col = (g * HPG + h) * 3
            oh = o[h * tq:(h + 1) * tq, (g % 2) * DH:(g % 2 + 1) * DH]
            pieces.append(gates[:, col:col + 1] * oh)
            if h:
                pg = pg + p[h * tq:(h + 1) * tq]
        for part in _split3(pg):
            p_slc = p_slc + jnp.dot(part, map_ref[g], preferred_element_type=f32)
    o_ref[...] = jnp.concatenate(pieces, axis=1)
    pt = p_slc.T
    n_blk_lanes = LANE // KVH
    jidx = lax.broadcasted_iota(jnp.int32, (n_blk_lanes, tq), 0)
    cur = (i * tq + lax.broadcasted_iota(jnp.int32, (n_blk_lanes, tq), 1)) // SLC_BLOCK
    forced = (jidx == 0) | ((jidx <= cur) & (jidx > cur - N_LOCAL_BLOCKS))
    sels = []
    for g in range(KVH):
        sc = pt[g * n_blk_lanes:(g + 1) * n_blk_lanes]
        sc = jnp.where(forced, jnp.inf, jnp.where(jidx > cur, -jnp.inf, sc))
        keep = _rank_select(sc, jidx, N_SEL) & (jidx <= cur)
        sels.append(jnp.where(keep, 1.0, 0.0))
    sel_ref[...] = jnp.concatenate(sels, axis=0).T


def _cmp_prompt(z, kc, rel_bias, batch, seq, tq):
    nt = seq // tq
    ns = kc.shape[1]
    n_blk = seq // SLC_BLOCK
    n_blk_lanes = LANE // KVH
    assert n_blk <= n_blk_lanes
    t = np.arange(seq)[:, None]
    dist = t - (np.arange(ns)[None, :] * CMP_STRIDE + CMP_LEN - 1)
    bias = _bias_by_dist(rel_bias, dist)
    smap = jnp.asarray(np.stack([_stride_to_block_map(ns, LANE, g * n_blk_lanes) for g in range(KVH)])).astype(bf16)
    nq = COL_Q // NSA_WIDTH
    return pl.pallas_call(
        _cmp_prompt_kernel,
        grid=(nt, batch),
        in_specs=[pl.BlockSpec((tq, NSA_WIDTH), lambda i, b: (b * nt + i, nq)),
                  pl.BlockSpec((tq, LANE), lambda i, b: (b * nt + i, COL_GATE // LANE)),
                  pl.BlockSpec((None, ns, KV_WIDTH), lambda i, b: (b, 0, 0)),
                  pl.BlockSpec((NSA_HEADS, tq, ns), lambda i, b: (0, i, 0)),
                  pl.BlockSpec((KVH, ns, LANE), lambda i, b: (0, 0, 0))],
        out_specs=[pl.BlockSpec((tq, NSA_WIDTH), lambda i, b: (b * nt + i, 0)),
                   pl.BlockSpec((tq, LANE), lambda i, b: (b * nt + i, 0))],
        out_shape=[jax.ShapeDtypeStruct((batch * seq, NSA_WIDTH), f32),
                   jax.ShapeDtypeStruct((batch * seq, LANE), f32)],
        compiler_params=_cparams(("parallel", "arbitrary")),
        name="nsa_cmp_prompt",
    )(z, z, kc, bias, smap)


def _flash_step_t(s, vt, m, l, acc):
    m_new = jnp.maximum(m, jnp.max(s, axis=0, keepdims=True))
    alpha = jnp.exp(m - m_new)
    p = jnp.exp(s - m_new)
    l = alpha * l + jnp.sum(p, axis=0, keepdims=True)
    acc[...] = alpha * acc[...] + jnp.dot(vt, p.astype(bf16), preferred_element_type=f32)
    return m_new, l


def _slcwin_prompt_kernel(q_ref, gate_ref, sel_ref, ocmp_ref, ks_ref, kw_ref, bias_ref, far_ref, exp_ref, o_ref,
                          ksb, kwb, vst, vwt, acc_s, acc_w, s_buf, pk_buf):
    i = pl.program_id(1)
    tq = q_ref.shape[0]
    cols = HPG * tq
    nt = vst.shape[0]
    half_w = KVH * DH

    @pl.when(i == 0)
    def _():
        ksb[...] = ks_ref[:, 0:half_w].astype(bf16)
        kwb[...] = kw_ref[:, 0:half_w].astype(bf16)
        for j in range(nt):
            vst[j] = ks_ref[j * tq:(j + 1) * tq, half_w:].T.astype(bf16)
            vwt[j] = kw_ref[j * tq:(j + 1) * tq, half_w:].T.astype(bf16)

    qs = q_ref[...] * ATTN_SCALE
    gates = _sigmoid(gate_ref[...])
    sel_t = sel_ref[...].T.astype(bf16)
    n_win_tiles = WINDOW // tq + 1
    init = (jnp.full((1, cols), NEG, f32), jnp.zeros((1, cols), f32))
    acc_s[...] = jnp.zeros_like(acc_s)
    acc_w[...] = jnp.zeros_like(acc_w)
    zpad = jnp.zeros((DH, tq), f32)
    kls = [slice((g // 2) * LANE, (g // 2 + 1) * LANE) for g in range(KVH)]
    qps = []
    for g in range(KVH):
        qt = qs[:, g * HPG * DH:(g + 1) * HPG * DH].T
        qps.append(jnp.concatenate(
            [jnp.concatenate([qt[h * DH:(h + 1) * DH], zpad] if g % 2 == 0 else [zpad, qt[h * DH:(h + 1) * DH]], axis=0)
             for h in range(HPG)], axis=1).astype(bf16))

    def scores(kbuf, j, g):
        return jnp.dot(kbuf[pl.ds(pl.multiple_of(j * tq, tq), tq), kls[g]], qps[g], preferred_element_type=f32)

    def slc_body(near):
        def body(j, carry):
            j_next = jnp.minimum(j + 1, i)
            out = []
            for g in range(KVH):
                s = s_buf[g] + (bias_ref[i - j, g] if near else far_ref[g])
                picked = pk_buf[g] > 0.5
                s_buf[g] = scores(ksb, j_next, g)
                pk_buf[g] = jnp.dot(exp_ref[g, j_next], sel_t, preferred_element_type=f32)
                s = jnp.concatenate([jnp.where(picked, s[:, h * tq:(h + 1) * tq], NEG) for h in range(HPG)], axis=1)
                out.append(_flash_step_t(s, vst[j, kls[g]], *carry[g], acc_s.at[g]))
            return tuple(out)
        return body

    for g in range(KVH):
        s_buf[g] = scores(ksb, 0, g)
        pk_buf[g] = jnp.dot(exp_ref[g, 0], sel_t, preferred_element_type=f32)
    n_far = jnp.maximum(i - 1, 0)
    slc = lax.fori_loop(0, n_far, slc_body(False), (init,) * KVH)
    slc = lax.fori_loop(n_far, i + 1, slc_body(True), slc)

    def win_body(kk, carry):
        j = i - kk
        j_next = jnp.maximum(j - 1, 0)
        if n_win_tiles == 3:
            kind = kk
        else:
            kind = jnp.where(kk == n_win_tiles - 1, 2, jnp.minimum(kk, 1))
        out = []
        for g in range(KVH):
            if n_win_tiles == 3:
                s = s_buf[g] + bias_ref[kind, g]
            else:
                s = s_buf[g] + jnp.where((kk >= 2) & (kk < n_win_tiles - 1), far_ref[g], bias_ref[kind, g])
            s_buf[g] = scores(kwb, j_next, g)
            out.append(_flash_step_t(s, vwt[j, kls[g]], *carry[g], acc_w.at[g]))
        return tuple(out)

    for g in range(KVH):
        s_buf[g] = scores(kwb, i, g)
    win = lax.fori_loop(0, jnp.minimum(i, n_win_tiles - 1) + 1, win_body, (init,) * KVH)
    pieces = []
    for g in range(KVH):
        o_s = (acc_s[g] / jnp.maximum(slc[g][1], 1e-30)).T
        o_w = (acc_w[g] / jnp.maximum(win[g][1], 1e-30)).T
        half = slice((g % 2) * DH, (g % 2 + 1) * DH)
        for h in range(HPG):
            col = (g * HPG + h) * 3
            hr = slice(h * tq, (h + 1) * tq)
            pieces.append(gates[:, col + 1:col + 2] * o_s[hr, half] + gates[:, col + 2:col + 3] * o_w[hr, half])
    o_ref[...] = ocmp_ref[...] + jnp.concatenate(pieces, axis=1)


def _slcwin_prompt(z, sel, ocmp, rel_bias, batch, seq, tq):
    nt = seq // tq
    n_blk_lanes = LANE // KVH
    n_win_tiles = WINDOW // tq + 1
    assert n_win_tiles >= 2 and WINDOW % tq == 0
    kk = np.array([0, 1, n_win_tiles - 1])[:, None, None]
    dist = kk * tq + np.arange(tq)[None, :, None] - np.arange(tq)[None, None, :]
    assert 2 * tq - (tq - 1) >= MAX_DISTANCE, "tile distance >= 2 must map to the last bucket"
    masked = dist < 0
    masked[2] |= dist[2] >= WINDOW
    bias = jnp.where(jnp.asarray(masked)[None], NEG, _bias_by_dist(rel_bias, dist))
    bias = bias.reshape(KVH, HPG, 3, tq, tq).transpose(2, 0, 4, 1, 3).reshape(3, KVH, tq, HPG * tq)
    far = _bias_by_dist(rel_bias, np.full((1, 1), MAX_DISTANCE))
    far = jnp.broadcast_to(far.reshape(KVH, 1, HPG, 1), (KVH, 1, HPG, tq)).reshape(KVH, 1, HPG * tq)
    ex = np.zeros((KVH, nt, tq, LANE), np.float32)
    for g in range(KVH):
        for j in range(nt):
            for s in range(tq):
                blk = (j * tq + s) // SLC_BLOCK
                if blk < n_blk_lanes:
                    ex[g, j, s, g * n_blk_lanes + blk] = 1.0
    ex = jnp.asarray(ex).astype(bf16)
    row = lambda w, c: pl.BlockSpec((tq, w), lambda b, i, c=c: (b * nt + i, c))
    return pl.pallas_call(
        _slcwin_prompt_kernel,
        grid=(batch, nt),
        in_specs=[row(NSA_WIDTH, COL_Q // NSA_WIDTH), row(LANE, COL_GATE // LANE),
                  pl.BlockSpec((tq, LANE), lambda b, i: (b * nt + i, 0)),
                  pl.BlockSpec((tq, NSA_WIDTH), lambda b, i: (b * nt + i, 0)),
                  pl.BlockSpec((seq, KV_WIDTH), lambda b, i: (b, COL_KVS // KV_WIDTH), pipeline_mode=pl.Buffered(1)),
                  pl.BlockSpec((seq, KV_WIDTH), lambda b, i: (b, COL_KVW // KV_WIDTH), pipeline_mode=pl.Buffered(1)),
                  pl.BlockSpec(bias.shape, lambda b, i: (0, 0, 0, 0), pipeline_mode=pl.Buffered(1)),
                  pl.BlockSpec(far.shape, lambda b, i: (0, 0, 0)),
                  pl.BlockSpec(ex.shape, lambda b, i: (0, 0, 0, 0), pipeline_mode=pl.Buffered(1))],
        out_specs=pl.BlockSpec((tq, NSA_WIDTH), lambda b, i: (b * nt + i, 0)),
        out_shape=jax.ShapeDtypeStruct((batch * seq, NSA_WIDTH), f32),
        scratch_shapes=[pltpu.VMEM((seq, KVH * DH), bf16), pltpu.VMEM((seq, KVH * DH), bf16),
                        pltpu.VMEM((nt, KVH * DH, tq), bf16), pltpu.VMEM((nt, KVH * DH, tq), bf16),
                        pltpu.VMEM((KVH, LANE, HPG * tq), f32), pltpu.VMEM((KVH, LANE, HPG * tq), f32),
                        pltpu.VMEM((KVH, tq, HPG * tq), f32), pltpu.VMEM((KVH, tq, tq), f32)],
        compiler_params=_cparams(("parallel", "arbitrary"), V7X_VMEM_LIMIT),
        name="nsa_slcwin_prompt",
    )(z, z, sel, ocmp, z, z, bias, far, ex)


ROUTE_GATE_LANE = 8


def _outproj_router_kernel(x_ref, ohg_ref, onsa_ref, wo_ref, g2_ref, rw_ref, rb_ref, x1_ref, xn_ref, route_ref):
    x1 = (x_ref[...]
          + jnp.dot(ohg_ref[...].astype(bf16), wo_ref[0:HG_WIDTH, :], preferred_element_type=f32)
          + jnp.dot(onsa_ref[...].astype(bf16), wo_ref[HG_WIDTH:, :], preferred_element_type=f32))
    x1_ref[...] = x1
    xn = x1 * lax.rsqrt(jnp.mean(x1 * x1, axis=-1, keepdims=True) + RMS_EPS) * g2_ref[...]
    xn_ref[...] = xn
    logits = jnp.dot(xn.astype(bf16), rw_ref[...], preferred_element_type=f32) + rb_ref[...]
    lane = lax.broadcasted_iota(jnp.int32, logits.shape, 1)
    route = jnp.zeros(logits.shape, f32)
    work = logits
    top = []
    for k in range(TOP_K):
        m = jnp.max(work, axis=-1, keepdims=True)
        idx = jnp.min(jnp.where(work == m, lane, LANE), axis=-1, keepdims=True)
        top.append(m)
        route = jnp.where(lane == k, idx.astype(f32), route)
        work = jnp.where(lane == idx, -jnp.inf, work)
    es = [jnp.exp(t - top[0]) for t in top]
    denom = es[0] + es[1] + es[2] + es[3]
    for k in range(TOP_K):
        route = jnp.where(lane == ROUTE_GATE_LANE + k, es[k] / denom, route)
    route_ref[...] = route


def _outproj_router(x, o_hg, o_nsa, wo_bf16, g2, rw_pad, rb_pad, tm):
    n, d = x.shape
    row = lambda w: pl.BlockSpec((tm, w), lambda i: (i, 0))
    full = lambda a: pl.BlockSpec(a.shape, lambda i: (0, 0))
    g2 = g2.reshape(1, d)
    return pl.pallas_call(
        _outproj_router_kernel,
        grid=(n // tm,),
        in_specs=[row(d), row(HG_WIDTH), row(NSA_WIDTH), full(wo_bf16), full(g2), full(rw_pad), full(rb_pad)],
        out_specs=[row(d), row(d), row(LANE)],
        out_shape=[jax.ShapeDtypeStruct((n, d), f32), jax.ShapeDtypeStruct((n, d), f32),
                   jax.ShapeDtypeStruct((n, LANE), f32)],
        compiler_params=_cparams(("parallel",), V7X_VMEM_LIMIT),
        name="outproj_router",
    )(x, o_hg, o_nsa, wo_bf16, g2, rw_pad, rb_pad)


MOE_ROWS = 1088
MOE_REGION = 544
MOE_TF = 256
ROW_DMA_UNROLL = 8


def _row_copies(n_rows, copy_fn):
    def start(r, c):
        for k in range(TOP_K):
            copy_fn(r, k).start(priority=k % 2)
        return c

    def wait(r, c):
        for k in range(TOP_K):
            copy_fn(r, k).wait()
        return c

    lax.fori_loop(0, n_rows, start, 0, unroll=ROW_DMA_UNROLL)
    lax.fori_loop(0, n_rows, wait, 0, unroll=ROW_DMA_UNROLL)


def _dispatch_kernel(slot_ref, xn_ref, xs_in_ref, xs_ref, sem):
    del xs_in_ref
    tb = xn_ref.shape[0]

    def copy(r, k):
        return pltpu.make_async_copy(xn_ref.at[pl.ds(r, 1)], xs_ref.at[pl.ds(slot_ref[r * TOP_K + k], 1)], sem)

    _row_copies(tb, copy)


def _dispatch(slots_flat, xn, xs, tb):
    n, d = xn.shape
    return pl.pallas_call(
        _dispatch_kernel,
        grid=(n // tb,),
        in_specs=[pl.BlockSpec((tb * TOP_K,), lambda i: (i,), memory_space=pltpu.SMEM),
                  pl.BlockSpec((tb, d), lambda i: (i, 0)),
                  pl.BlockSpec(memory_space=pl.ANY)],
        out_specs=pl.BlockSpec(memory_space=pl.ANY),
        out_shape=jax.ShapeDtypeStruct(xs.shape, xs.dtype),
        scratch_shapes=[pltpu.SemaphoreType.DMA(())],
        input_output_aliases={2: 0},
        compiler_params=_cparams(("arbitrary",)),
        name="moe_dispatch",
    )(slots_flat, xn, xs)


def _expert_kernel(ie_ref, ir_ref, x_ref, w1g_ref, w1u_ref, b1g_ref, b1u_ref, w2_ref, b2_ref, y_ref):
    m = pl.program_id(0)
    j = pl.program_id(1)
    rows = ir_ref[m]

    @pl.when(j == 0)
    def _():
        y_ref[...] = jnp.broadcast_to(b2_ref[...], y_ref.shape)

    @pl.when(rows > 0)
    def _():
        w1g = w1g_ref[...].astype(bf16)
        w1u = w1u_ref[...].astype(bf16)
        w2 = w2_ref[...].astype(bf16)
        for start in range(0, MOE_ROWS, MOE_REGION):
            rs = slice(start, min(start + MOE_REGION, MOE_ROWS))

            @pl.when(start < rows)
            def _():
                x = x_ref[rs, :].astype(bf16)
                hg = jnp.dot(x, w1g, preferred_element_type=f32) + b1g_ref[...]
                hu = jnp.dot(x, w1u, preferred_element_type=f32) + b1u_ref[...]
                gl = jnp.minimum(hg, SWIGLU_LIMIT)
                up = jnp.clip(hu, -SWIGLU_LIMIT, SWIGLU_LIMIT)
                act = (up + 1.0) * gl * _sigmoid(SWIGLU_ALPHA * gl)
                y_ref[rs, :] += jnp.dot(act.astype(bf16), w2, preferred_element_type=f32)


def _experts(item_e, item_rows, xs, w1, b1, w2, b2):
    n_items = item_e.shape[0]
    d = xs.shape[1]
    nf = D_FF // MOE_TF
    jj = lambda m, j, ir: jnp.where(ir[m] > 0, j, nf - 1)
    b1 = b1.reshape(N_EXPERTS, 1, 2 * D_FF)
    b2 = b2.reshape(N_EXPERTS, 1, d)
    return pl.pallas_call(
        _expert_kernel,
        grid_spec=pltpu.PrefetchScalarGridSpec(
            num_scalar_prefetch=2,
            grid=(n_items, nf),
            in_specs=[pl.BlockSpec((MOE_ROWS, d), lambda m, j, ie, ir: (m, 0)),
                      pl.BlockSpec((None, d, MOE_TF), lambda m, j, ie, ir: (ie[m], 0, jj(m, j, ir))),
                      pl.BlockSpec((None, d, MOE_TF), lambda m, j, ie, ir: (ie[m], 0, nf + jj(m, j, ir))),
                      pl.BlockSpec((None, 1, MOE_TF), lambda m, j, ie, ir: (ie[m], 0, jj(m, j, ir))),
                      pl.BlockSpec((None, 1, MOE_TF), lambda m, j, ie, ir: (ie[m], 0, nf + jj(m, j, ir))),
                      pl.BlockSpec((None, MOE_TF, d), lambda m, j, ie, ir: (ie[m], jj(m, j, ir), 0)),
                      pl.BlockSpec((None, 1, d), lambda m, j, ie, ir: (ie[m], 0, 0))],
            out_specs=pl.BlockSpec((MOE_ROWS, d), lambda m, j, ie, ir: (m, 0))),
        out_shape=jax.ShapeDtypeStruct(xs.shape, f32),
        compiler_params=_cparams(("arbitrary", "arbitrary"), V7X_VMEM_LIMIT),
        name="moe_experts",
    )(item_e, item_rows, xs, w1, w1, b1, b1, w2, b2)


def _combine_kernel(slot_ref, x1_ref, route_ref, gf_ref, ys_ref, y_ref, buf, sem):
    tb = x1_ref.shape[0]

    def copy(r, k):
        return pltpu.make_async_copy(ys_ref.at[pl.ds(slot_ref[r * TOP_K + k], 1)], buf.at[k, pl.ds(r, 1)], sem)

    _row_copies(tb, copy)
    route = route_ref[...]
    x2 = x1_ref[...]
    for k in range(TOP_K):
        x2 = x2 + route[:, ROUTE_GATE_LANE + k:ROUTE_GATE_LANE + k + 1] * buf[k]
    y_ref[...] = x2 * lax.rsqrt(jnp.mean(x2 * x2, axis=-1, keepdims=True) + RMS_EPS) * gf_ref[...]


def _combine(slots_flat, x1, route, gf, ys, tb):
    n, d = x1.shape
    return pl.pallas_call(
        _combine_kernel,
        grid=(n // tb,),
        in_specs=[pl.BlockSpec((tb * TOP_K,), lambda i: (i,), memory_space=pltpu.SMEM),
                  pl.BlockSpec((tb, d), lambda i: (i, 0)),
                  pl.BlockSpec((tb, LANE), lambda i: (i, 0)),
                  pl.BlockSpec((1, d), lambda i: (0, 0)),
                  pl.BlockSpec(memory_space=pl.ANY)],
        out_specs=pl.BlockSpec((tb, d), lambda i: (i, 0)),
        out_shape=jax.ShapeDtypeStruct((n, d), f32),
        scratch_shapes=[pltpu.VMEM((TOP_K, tb, d), f32), pltpu.SemaphoreType.DMA(())],
        compiler_params=_cparams(("arbitrary",), V7X_VMEM_LIMIT),
        name="moe_combine",
    )(slots_flat, x1, route, gf.reshape(1, d), ys)


def _routing_plan(top_e, n_items):
    flat_e = top_e.reshape(-1)
    onehot = (flat_e[:, None] == jnp.arange(N_EXPERTS, dtype=jnp.int32)[None, :]).astype(jnp.int32)
    csum = jnp.cumsum(onehot, axis=0)
    rank = jnp.sum(onehot * (csum - onehot), axis=1)
    counts = csum[-1]
    padded = (counts + MOE_ROWS - 1) // MOE_ROWS * MOE_ROWS
    pad_end = jnp.cumsum(padded)
    start = pad_end - padded
    slots = (start[flat_e] + rank).astype(jnp.int32)
    row0 = jnp.arange(n_items, dtype=jnp.int32) * MOE_ROWS
    item_e = jnp.minimum(jnp.searchsorted(pad_end, row0, side='right'), N_EXPERTS - 1).astype(jnp.int32)
    item_rows = jnp.clip(counts[item_e] - (row0 - start[item_e]), 0, MOE_ROWS).astype(jnp.int32)
    used = row0 < pad_end[-1]
    last_e = item_e[jnp.maximum(pad_end[-1] // MOE_ROWS - 1, 0)]
    item_e = jnp.where(used, item_e, last_e)
    item_rows = jnp.where(used, item_rows, 0)
    return slots, item_e, item_rows


SUB = 8


def _hgrn_sample_kernel(z_ref, lb_ref, gain_ref, s0_ref, o_ref, s_ref):
    gain = gain_ref[...]
    eye = lax.broadcasted_iota(jnp.int32, (HG_DK, HG_DK), 0) == lax.broadcasted_iota(jnp.int32, (HG_DK, HG_DK), 1)

    def column(rowvec):
        return jnp.sum(jnp.where(eye, jnp.broadcast_to(rowvec, (HG_DK, HG_DK)), 0.0), axis=-1, keepdims=True)

    for h in range(HG_HEADS):
        seg = lambda i, h=h: z_ref[:, i * HG_WIDTH + h * HG_DK:i * HG_WIDTH + (h + 1) * HG_DK]
        cs = slice(h * HG_DK, (h + 1) * HG_DK)
        q, k, g = _hgrn_gates(seg(0), seg(1), lb_ref[:, cs])
        v = seg(2)
        eg = jnp.exp(g)
        s0 = s0_ref[h]
        qe = jnp.broadcast_to(q * eg, (SUB, HG_DK)).astype(bf16)
        o = jnp.sum(q * k, axis=-1, keepdims=True) * v + jnp.dot(qe, s0.astype(bf16), preferred_element_type=f32)[0:1]
        s_ref[h] = column(eg) * s0 + column(k) * v
        o_ref[:, cs] = _hgrn_out(o, gain, seg(3))


def _row3(z):
    return z.reshape(z.shape[0], 1, z.shape[1])


def _hgrn_sample(z, lb, gain, s0):
    bs = z.shape[0]
    o, s = pl.pallas_call(
        _hgrn_sample_kernel,
        grid=(bs,),
        in_specs=[pl.BlockSpec((None, 1, 4 * HG_WIDTH), lambda b: (b, 0, 0)),
                  pl.BlockSpec((1, HG_WIDTH), lambda b: (0, 0)),
                  pl.BlockSpec((1, HG_DV), lambda b: (0, 0)),
                  pl.BlockSpec((None, HG_HEADS, HG_DK, HG_DV), lambda b: (b, 0, 0, 0))],
        out_specs=[pl.BlockSpec((None, 1, HG_WIDTH), lambda b: (b, 0, 0)),
                   pl.BlockSpec((None, HG_HEADS, HG_DK, HG_DV), lambda b: (b, 0, 0, 0))],
        out_shape=[jax.ShapeDtypeStruct((bs, 1, HG_WIDTH), f32), jax.ShapeDtypeStruct(s0.shape, f32)],
        compiler_params=_cparams(("parallel",)),
        name="hgrn_sample",
    )(_row3(z), lb.reshape(1, HG_WIDTH), gain.reshape(1, HG_DV), s0)
    return o.reshape(bs, HG_WIDTH), s


def _q_pad_row(q, g):
    qb = jnp.broadcast_to(q, (SUB, q.shape[1]))
    z = jnp.zeros((SUB, DH), f32)
    row = lax.broadcasted_iota(jnp.int32, (SUB, LANE), 0)
    out = jnp.zeros((SUB, LANE), f32)
    for h in range(HPG):
        c0 = (g * HPG + h) * DH
        piece = jnp.concatenate([qb[:, c0:c0 + DH], z] if g % 2 == 0 else [z, qb[:, c0:c0 + DH]], axis=1)
        out = jnp.where(row == h, piece, out)
    return out.astype(bf16)


def _head_pieces(o, g):
    half = slice((g % 2) * DH, (g % 2 + 1) * DH)
    return [o[h:h + 1, half] for h in range(HPG)]


def _cmp_sample_kernel(q_ref, gate_ref, kc_ref, bias_ref, map_ref, o_ref, idx_ref, *, cur, n_blk_lanes):
    q = q_ref[...]
    gates = _sigmoid(gate_ref[...])
    ns = kc_ref.shape[0]
    pieces = []
    idx_ref[...] = jnp.zeros(idx_ref.shape, jnp.int32)
    r_i = lax.broadcasted_iota(jnp.int32, (n_blk_lanes, n_blk_lanes), 0)
    c_i = lax.broadcasted_iota(jnp.int32, (n_blk_lanes, n_blk_lanes), 1)
    forced_c = (c_i == 0) | ((c_i <= cur) & (c_i > cur - N_LOCAL_BLOCKS))
    slot = lax.broadcasted_iota(jnp.int32, (n_blk_lanes, LANE), 1).astype(f32)
    blk_id = lax.broadcasted_iota(jnp.int32, (n_blk_lanes, LANE), 0)
    for g in range(KVH):
        pair = slice((g // 2) * LANE, (g // 2 + 1) * LANE)
        vpair = slice(KVH * DH + (g // 2) * LANE, KVH * DH + (g // 2 + 1) * LANE)
        s = _dot_nt(_q_pad_row(q, g), kc_ref[:, pair]) * ATTN_SCALE + bias_ref[g]
        valid = s > 0.5 * NEG
        m = jnp.max(s, axis=-1, keepdims=True)
        e = jnp.where(valid, jnp.exp(s - m), 0.0)
        p = e / jnp.maximum(jnp.sum(e, axis=-1, keepdims=True), 1e-30)
        o = jnp.dot(p.astype(bf16), kc_ref[:, vpair], preferred_element_type=f32)
        for h, oh in enumerate(_head_pieces(o, g)):
            col = (g * HPG + h) * 3
            pieces.append(gates[:, col:col + 1] * oh)
        pg = jnp.broadcast_to(jnp.sum(p[0:HPG], axis=0, keepdims=True), (SUB, ns))
        p_slc = jnp.zeros((SUB, n_blk_lanes), f32)
        for part in _split3(pg):
            p_slc = p_slc + jnp.dot(part, map_ref[...], preferred_element_type=f32)
        a = jnp.broadcast_to(p_slc[0:1], (n_blk_lanes, n_blk_lanes))
        a = jnp.where(forced_c, jnp.inf, jnp.where(c_i > cur, -jnp.inf, a))
        bt = a.T
        ahead = (a > bt) | ((a == bt) & (c_i < r_i))
        rank = jnp.sum(jnp.where(ahead, 1.0, 0.0), axis=-1, keepdims=True)
        hit = (rank == slot) & (blk_id <= cur)
        chosen = jnp.sum(jnp.where(hit, blk_id.astype(f32), 0.0), axis=0, keepdims=True)
        idx_ref[g:g + 1, :] = chosen.astype(jnp.int32)
    o_ref[...] = jnp.concatenate(pieces, axis=1)


def _cmp_sample(z, kc, rel_bias, q_pos):
    bs = z.shape[0]
    ns = kc.shape[1]
    nc = ns - (CMP_LEN // CMP_STRIDE - 1)
    n_blk = -(-(q_pos + 1) // SLC_BLOCK)
    assert n_blk >= N_SEL
    n_blk_lanes = -(-n_blk // LANE) * LANE
    k_end = np.arange(ns) * CMP_STRIDE + CMP_LEN - 1
    dist = q_pos - k_end
    bias = _bias_by_dist(rel_bias, dist)
    bias = jnp.where(jnp.asarray((dist >= 0) & (np.arange(ns) < nc))[None], bias, NEG)
    bias = jnp.pad(bias.reshape(KVH, HPG, ns), ((0, 0), (0, SUB - HPG), (0, 0)))
    smap = jnp.asarray(_stride_to_block_map(ns, n_blk_lanes, 0)).astype(bf16)
    kern = functools.partial(_cmp_sample_kernel, cur=q_pos // SLC_BLOCK, n_blk_lanes=n_blk_lanes)
    z3 = _row3(z)
    o, idx = pl.pallas_call(
        kern,
        grid=(bs,),
        in_specs=[pl.BlockSpec((None, 1, NSA_WIDTH), lambda b: (b, 0, COL_Q // NSA_WIDTH)),
                  pl.BlockSpec((None, 1, LANE), lambda b: (b, 0, COL_GATE // LANE)),
                  pl.BlockSpec((None, ns, KV_WIDTH), lambda b: (b, 0, 0)),
                  pl.BlockSpec(bias.shape, lambda b: (0, 0, 0)),
                  pl.BlockSpec(smap.shape, lambda b: (0, 0))],
        out_specs=[pl.BlockSpec((None, 1, NSA_WIDTH), lambda b: (b, 0, 0)),
                   pl.BlockSpec((None, SUB, LANE), lambda b: (b, 0, 0))],
        out_shape=[jax.ShapeDtypeStruct((bs, 1, NSA_WIDTH), f32), jax.ShapeDtypeStruct((bs, SUB, LANE), jnp.int32)],
        compiler_params=_cparams(("parallel",)),
        name="nsa_cmp_sample",
    )(z3, z3, kc, bias, smap)
    return o.reshape(bs, NSA_WIDTH), idx


def _q_rows(q, g):
    qb = jnp.broadcast_to(q, (SUB, q.shape[1]))
    row = lax.broadcasted_iota(jnp.int32, (SUB, DH), 0)
    out = jnp.zeros((SUB, DH), f32)
    for h in range(HPG):
        c0 = (g * HPG + h) * DH
        out = jnp.where(row == h, qb[:, c0:c0 + DH], out)
    return out


def _column(rowvec):
    n = rowvec.shape[1]
    eye = lax.broadcasted_iota(jnp.int32, (n, n), 0) == lax.broadcasted_iota(jnp.int32, (n, n), 1)
    return jnp.sum(jnp.where(eye, jnp.broadcast_to(rowvec, (n, n)), 0.0), axis=-1, keepdims=True)


SEL_PER_STEP = 4


def _slcwin_sample_kernel(idx_ref, pt_ref, q_ref, gate_ref, ocmp_ref, ksn_ref, kwn_ref, *refs, past, bpp):
    del pt_ref
    n_pool_refs = SEL_PER_STEP * KVH
    pools = refs[:n_pool_refs]
    win_ref, bslc_ref, bwin_ref, bnew_ref, o_ref, nwin_ref, qr_s, m_s, l_s, acc_s, ow_s = refs[n_pool_refs:]
    b = pl.program_id(0)
    k = pl.program_id(1)
    wlen = win_ref.shape[3]
    page_rows = pools[0].shape[2]
    half_w = KVH * DH

    @pl.when(k == 0)
    def _():
        q = q_ref[...] * ATTN_SCALE
        wnew = kwn_ref[...]
        lane = lax.broadcasted_iota(jnp.int32, (DH, wlen), 1)
        for g in range(KVH):
            knew = wnew[:, g * DH:(g + 1) * DH]
            vnew = wnew[:, half_w + g * DH:half_w + (g + 1) * DH]
            kt = win_ref[0, g]
            vt = win_ref[1, g]
            nwin_ref[0, g] = jnp.where(lane == wlen - 1, _column(knew), pltpu.roll(kt, wlen - 1, 1))
            nwin_ref[1, g] = jnp.where(lane == wlen - 1, _column(vnew), pltpu.roll(vt, wlen - 1, 1))
            qr = _q_rows(q, g).astype(bf16)
            qr_s[g] = qr
            m_s[g] = jnp.full((SUB, 1), NEG, f32)
            l_s[g] = jnp.zeros((SUB, 1), f32)
            acc_s[g] = jnp.zeros((SUB, DH), f32)
            s1 = jnp.dot(qr, kt.astype(bf16), preferred_element_type=f32) + bwin_ref[g]
            s2 = jnp.sum(qr.astype(f32) * knew.astype(bf16).astype(f32), axis=-1, keepdims=True) + bnew_ref[g][:, 0:1]
            ok = s1 > 0.5 * NEG
            mx = jnp.maximum(jnp.max(s1, axis=-1, keepdims=True), s2)
            e1 = jnp.where(ok, jnp.exp(s1 - mx), 0.0)
            e2 = jnp.exp(s2 - mx)
            den = jnp.maximum(jnp.sum(e1, axis=-1, keepdims=True) + e2, 1e-30)
            ow_s[g] = (_dot_nt(e1.astype(bf16), vt.astype(bf16))
                       + e2.astype(bf16).astype(f32) * vnew.astype(bf16).astype(f32)) / den

    snew = ksn_ref[...]
    lane_k = lax.broadcasted_iota(jnp.int32, (DH, page_rows), 1)
    lane_s = lax.broadcasted_iota(jnp.int32, (SUB, page_rows), 1)
    for g in range(KVH):
        kcol = _column(snew[:, g * DH:(g + 1) * DH])
        vcol = _column(snew[:, half_w + g * DH:half_w + (g + 1) * DH])
        m, l, acc = m_s[g], l_s[g], acc_s[g]
        for kk in range(SEL_PER_STEP):
            blk = idx_ref[(b * KVH + g) * N_SEL + k * SEL_PER_STEP + kk]
            page = blk // bpp
            tile = pools[kk * KVH + g]
            fresh = page * page_rows + lane_k >= past
            kt = jnp.where(fresh, kcol, tile[0]).astype(bf16)
            vt = jnp.where(fresh, vcol, tile[1]).astype(bf16)
            s = jnp.dot(qr_s[g], kt, preferred_element_type=f32) + bslc_ref[page, g]
            kpos = page * page_rows + lane_s
            mask = (lane_s // SLC_BLOCK == blk % bpp) & (kpos <= past)
            s = jnp.where(mask, s, NEG)
            m_new = jnp.maximum(m, jnp.max(s, axis=-1, keepdims=True))
            alpha = jnp.exp(m - m_new)
            p = jnp.where(mask, jnp.exp(s - m_new), 0.0)
            l = alpha * l + jnp.sum(p, axis=-1, keepdims=True)
            acc = alpha * acc + _dot_nt(p.astype(bf16), vt)
            m = m_new
        m_s[g], l_s[g], acc_s[g] = m, l, acc

    @pl.when(k == pl.num_programs(1) - 1)
    def _():
        gates = _sigmoid(gate_ref[...])
        pieces = []
        for g in range(KVH):
            o_sl = acc_s[g] / jnp.maximum(l_s[g], 1e-30)
            o_w = ow_s[g]
            for h in range(HPG):
                col = (g * HPG + h) * 3
                pieces.append(gates[:, col + 1:col + 2] * o_sl[h:h + 1] + gates[:, col + 2:col + 3] * o_w[h:h + 1])
        o_ref[...] = ocmp_ref[...] + jnp.concatenate(pieces, axis=1)


def _slcwin_sample(z, ocmp, idx, pool_t, page_table, win_t, rel_bias, past):
    bs = z.shape[0]
    n_pages = page_table.shape[1]
    page_rows = pool_t.shape[4]
    bpp = page_rows // SLC_BLOCK
    wlen = win_t.shape[4]
    kpos = np.arange(n_pages + 1)[:, None] * page_rows + np.arange(page_rows)[None, :]
    bslc = _bias_by_dist(rel_bias, past - kpos)
    bslc = jnp.pad(bslc.reshape(KVH, HPG, n_pages + 1, page_rows),
                   ((0, 0), (0, SUB - HPG), (0, 0), (0, 0))).transpose(2, 0, 1, 3)
    wpos = past - wlen + np.arange(wlen)
    wdist = past - wpos
    bwin = jnp.where(jnp.asarray((wdist < WINDOW) & (wpos >= 0))[None], _bias_by_dist(rel_bias, wdist), NEG)
    bwin = jnp.pad(bwin.reshape(KVH, HPG, wlen), ((0, 0), (0, SUB - HPG), (0, 0)))
    bnew = jnp.broadcast_to(_bias_by_dist(rel_bias, np.zeros((1,), np.int64)).reshape(KVH, HPG, 1), (KVH, HPG, LANE))
    bnew = jnp.pad(bnew, ((0, 0), (0, SUB - HPG), (0, 0)))

    def pool_map(kk, g):
        def f(b, k, idx_r, pt_r):
            blk = idx_r[(b * KVH + g) * N_SEL + k * SEL_PER_STEP + kk]
            return (pt_r[b * n_pages + jnp.minimum(blk // bpp, n_pages - 1)], 0, g, 0, 0)
        return f

    rowblk = lambda w, c: pl.BlockSpec((None, 1, w), lambda b, k, i, p, c=c: (b, 0, c))
    full = lambda a: pl.BlockSpec(a.shape, lambda b, k, i, p: (0,) * a.ndim)
    win_spec = pl.BlockSpec((None, 2, KVH, DH, wlen), lambda b, k, i, p: (b, 0, 0, 0, 0))
    kern = functools.partial(_slcwin_sample_kernel, past=past, bpp=bpp)
    z3 = _row3(z)
    o, new_win_t = pl.pallas_call(
        kern,
        grid_spec=pltpu.PrefetchScalarGridSpec(
            num_scalar_prefetch=2,
            grid=(bs, N_SEL // SEL_PER_STEP),
            in_specs=[rowblk(NSA_WIDTH, COL_Q // NSA_WIDTH), rowblk(LANE, COL_GATE // LANE),
                      rowblk(NSA_WIDTH, 0),
                      rowblk(KV_WIDTH, COL_KVS // KV_WIDTH), rowblk(KV_WIDTH, COL_KVW // KV_WIDTH)]
                     + [pl.BlockSpec((None, 2, None, DH, page_rows), pool_map(kk, g))
                        for kk in range(SEL_PER_STEP) for g in range(KVH)]
                     + [win_spec, full(bslc), full(bwin), full(bnew)],
            out_specs=[rowblk(NSA_WIDTH, 0), win_spec],
            scratch_shapes=[pltpu.VMEM((KVH, SUB, DH), bf16), pltpu.VMEM((KVH, SUB, 1), f32),
                            pltpu.VMEM((KVH, SUB, 1), f32), pltpu.VMEM((KVH, SUB, DH), f32),
                            pltpu.VMEM((KVH, SUB, DH), f32)]),
        out_shape=[jax.ShapeDtypeStruct((bs, 1, NSA_WIDTH), f32), jax.ShapeDtypeStruct(win_t.shape, f32)],
        compiler_params=_cparams(("parallel", "arbitrary"), V7X_VMEM_LIMIT),
        name="nsa_slcwin_sample",
    )(idx[:, :KVH, :N_SEL].reshape(-1), page_table.reshape(-1).astype(jnp.int32), z3, z3, _row3(ocmp), z3, z3,
      *([pool_t] * (SEL_PER_STEP * KVH)), win_t, bslc, bwin, bnew)
    return o.reshape(bs, NSA_WIDTH), new_win_t


def kernel(x_prompt, x_sample, cache_kv_cmp, cache_kv_slc, state_win_kv, state_hgrn, page_table, norm1, w_in, hg_lower_bound, hg_norm, cmp_pe, cmp_w1, cmp_w2, rel_bias, w_out, norm2, router_w, router_b, moe_w1, moe_b1, moe_w2, moe_b2, norm_f):
    batch, seq, d = x_prompt.shape
    bs, dec_seq, _ = x_sample.shape
    assert norm1.shape[0] == 1 and dec_seq == 1
    n_pool, page_rows = cache_kv_cmp.shape[1:3]
    n_pages = page_table.shape[1]
    past = n_pages * page_rows
    wlen = state_win_kv.shape[2]
    assert wlen == WINDOW and past % CMP_STRIDE == 0 and seq % page_rows == 0

    lb = jnp.cumsum(jax.nn.softmax(hg_lower_bound.astype(f32), axis=0), axis=0)[0]
    w_in_p = jnp.pad(w_in[0], ((0, 0), (0, Z_WIDTH - IN_WIDTH))).astype(bf16)
    xp = x_prompt.reshape(batch * seq, d)
    xs = x_sample.reshape(bs, d)
    tq = min(128, seq)

    zp = _in_proj(xp, norm1[0], w_in_p, min(1024, batch * seq))
    o_hg_p, s_p = _hgrn_prompt(zp, lb, hg_norm[0], batch, seq, min(256, seq))
    kvc_p = zp[:, COL_KVC:COL_KVC + KV_WIDTH]
    kvs_p = zp[:, COL_KVS:COL_KVS + KV_WIDTH]
    kvw_p = zp[:, COL_KVW:COL_KVW + KV_WIDTH]
    ident = jnp.arange(batch * seq // page_rows, dtype=jnp.int32).reshape(batch, seq // page_rows)
    kc_p = _compress(kvc_p.reshape(-1, page_rows, KV_WIDTH), ident, page_rows, cmp_pe[0], cmp_w1[0], cmp_w2[0])
    ocmp_p, sel = _cmp_prompt(zp, kc_p, rel_bias, batch, seq, tq)
    o_nsa_p = _slcwin_prompt(zp, sel, ocmp_p, rel_bias, batch, seq, min(256, seq))

    rows_minor = lambda a: jnp.transpose(a, (0, 2, 3, 4, 1))
    zs = _in_proj(xs, norm1[0], w_in_p, bs)
    o_hg_s, s_s = _hgrn_sample(zs, lb, hg_norm[0], state_hgrn[0])
    kc_s = _compress_paged(rows_minor(cache_kv_cmp[0]).reshape(n_pool, 2, KVH * DH, page_rows), page_table,
                           cmp_pe[0], cmp_w1[0], cmp_w2[0])
    ocmp_s, idx = _cmp_sample(zs, kc_s, rel_bias, past)
    o_nsa_s, new_win_t = _slcwin_sample(zs, ocmp_s, idx, rows_minor(cache_kv_slc[0]), page_table,
                                        rows_minor(state_win_kv[0]), rel_bias, past)
    new_win = jnp.transpose(new_win_t, (0, 4, 1, 2, 3))

    wo = w_out[0].astype(bf16)
    rw = jnp.pad(router_w[0], ((0, 0), (0, LANE - N_EXPERTS))).astype(bf16)
    rb = jnp.pad(router_b[0].astype(f32), (0, LANE - N_EXPERTS), constant_values=NEG).reshape(1, LANE)
    x1_p, xn_p, route_p = _outproj_router(xp, o_hg_p, o_nsa_p, wo, norm2[0], rw, rb, min(256, batch * seq))
    x1_s, xn_s, route_s = _outproj_router(xs, o_hg_s, o_nsa_s, wo, norm2[0], rw, rb, bs)
    top_e = jnp.concatenate([route_p[:, :TOP_K], route_s[:, :TOP_K]], axis=0).astype(jnp.int32)
    n_tok = batch * seq + bs
    n_items = -(-n_tok * TOP_K // MOE_ROWS) + N_EXPERTS
    slots, item_e, item_rows = _routing_plan(top_e, n_items)
    slots_p, slots_s = slots[:batch * seq * TOP_K], slots[batch * seq * TOP_K:]
    xsort = jnp.zeros((n_items * MOE_ROWS, d), f32)
    xsort = _dispatch(slots_p, xn_p, xsort, min(256, batch * seq))
    xsort = _dispatch(slots_s, xn_s, xsort, bs)
    ysort = _experts(item_e, item_rows, xsort, moe_w1[0], moe_b1[0], moe_w2[0], moe_b2[0])
    y_p = _combine(slots_p, x1_p, route_p, norm_f, ysort, min(128, batch * seq))
    y_s = _combine(slots_s, x1_s, route_s, norm_f, ysort, bs)

    kv5 = lambda a, n, t: a.reshape(1, n, t, 2, KVH, DH)
    win_p = kvw_p.reshape(batch, seq, KV_WIDTH)[:, seq - min(WINDOW, seq):]
    return (y_p.reshape(batch, seq, d), y_s.reshape(bs, 1, d),
            kv5(kvc_p, batch, seq), kv5(kvs_p, batch, seq), kv5(win_p, batch, min(WINDOW, seq)), s_p[None],
            kv5(zs[:, COL_KVC:COL_KVC + KV_WIDTH], bs, 1), kv5(zs[:, COL_KVS:COL_KVS + KV_WIDTH], bs, 1),
            kv5(new_win, bs, wlen), s_s[None])
```

```python
import functools
import math

import jax
import jax.numpy as jnp
import numpy as np
from jax import lax
from jax.experimental import pallas as pl
from jax.experimental.pallas import tpu as pltpu

f32 = jnp.float32
bf16 = jnp.bfloat16

HG_HEADS, HG_DK, HG_DV = 8, 128, 128
HG_STEP = 16
NSA_HEADS, KVH, DH = 16, 4, 64
HPG = NSA_HEADS // KVH
CMP_LEN, CMP_STRIDE, CMP_HIDDEN = 32, 16, 128
SLC_BLOCK, N_SEL, N_LOCAL_BLOCKS, WINDOW = 64, 16, 2, 512
ATTN_SCALE = DH ** -0.5
NUM_BUCKETS, MAX_DISTANCE = 32, 128
N_EXPERTS, TOP_K, D_FF = 32, 4, 2048
SWIGLU_ALPHA, SWIGLU_LIMIT = 1.702, 7.0
RMS_EPS = 1e-5

HG_WIDTH = HG_HEADS * HG_DV
NSA_WIDTH = NSA_HEADS * DH
KV_WIDTH = 2 * KVH * DH
IN_SPLITS = (HG_WIDTH, HG_WIDTH, HG_WIDTH, HG_WIDTH, NSA_WIDTH, KV_WIDTH, KV_WIDTH, KV_WIDTH, NSA_HEADS * 3)
IN_WIDTH = sum(IN_SPLITS)
Z_WIDTH = 7168
COL_Q, COL_KVC, COL_KVS, COL_KVW, COL_GATE = 4096, 5120, 5632, 6144, 6656
LANE = 128
NEG = -1e30

V7X_VMEM_LIMIT = 56 * 1024 * 1024


def _cparams(sem, vmem=None):
    return pltpu.CompilerParams(dimension_semantics=sem, vmem_limit_bytes=vmem)


def _sigmoid(x):
    return 1.0 / (1.0 + jnp.exp(-x))


def _silu(x):
    return x * _sigmoid(x)


def _proj_kernel(x_ref, g_ref, w_ref, z_ref, hn_ref):
    @pl.when(pl.program_id(1) == 0)
    def _():
        x = x_ref[...]
        y = x * lax.rsqrt(jnp.mean(x * x, axis=-1, keepdims=True) + RMS_EPS) * g_ref[...]
        hn_ref[...] = y.astype(bf16)

    z_ref[...] = jnp.dot(hn_ref[...], w_ref[...], preferred_element_type=f32)


def _in_proj(x, gain, w_bf16, tm):
    n, d = x.shape
    tn = Z_WIDTH // 4
    return pl.pallas_call(
        _proj_kernel,
        grid=(n // tm, Z_WIDTH // tn),
        in_specs=[pl.BlockSpec((tm, d), lambda i, j: (i, 0)),
                  pl.BlockSpec((1, d), lambda i, j: (0, 0)),
                  pl.BlockSpec((d, tn), lambda i, j: (0, j))],
        out_specs=pl.BlockSpec((tm, tn), lambda i, j: (i, j)),
        out_shape=jax.ShapeDtypeStruct((n, Z_WIDTH), f32),
        scratch_shapes=[pltpu.VMEM((tm, d), bf16)],
        compiler_params=_cparams(("parallel", "arbitrary"), V7X_VMEM_LIMIT),
        name="in_proj",
    )(x, gain.reshape(1, d), w_bf16)


def _hgrn_gates(q_raw, f_raw, lb):
    q = _silu(q_raw)
    f = lb + (1.0 - lb) * _sigmoid(f_raw)
    return q, 1.0 - f, jnp.log(f)


def _hgrn_out(o, gain, g_raw):
    y = o * lax.rsqrt(jnp.mean(o * o, axis=-1, keepdims=True) + RMS_EPS) * gain
    return y * _silu(g_raw)


def _hgrn_prompt_kernel(q_ref, f_ref, i_ref, g_ref, lb_ref, gain_ref, o_ref, s_ref, st_ref):
    tb = pl.program_id(1)
    n_steps = q_ref.shape[0] // HG_STEP

    @pl.when(tb == 0)
    def _():
        st_ref[...] = jnp.zeros_like(st_ref)

    row = lax.broadcasted_iota(jnp.int32, (HG_STEP, HG_DK), 0)
    gain = gain_ref[...]

    def step(c, carry):
        r0 = pl.multiple_of(c * HG_STEP, HG_STEP)
        for h in range(HG_HEADS):
            cs = slice(h * HG_DK, (h + 1) * HG_DK)
            q, k, g = _hgrn_gates(q_ref[pl.ds(r0, HG_STEP), cs], f_ref[pl.ds(r0, HG_STEP), cs], lb_ref[:, cs])
            v = i_ref[pl.ds(r0, HG_STEP), cs]
            b = g
            for sh in (1, 2, 4, 8):
                b = b + jnp.where(row >= sh, pltpu.roll(b, sh, 0), 0.0)
            b_last = b[HG_STEP - 1:HG_STEP, :]
            st = st_ref[h]
            o = lax.dot_general((q * jnp.exp(b)).astype(bf16), st.astype(bf16),
                                (((1,), (1,)), ((), ())), preferred_element_type=f32)
            half = HG_STEP // 2
            parts = [(q[:half], b[:half], jnp.zeros((half, HG_DV), f32)), (q[half:], b[half:], jnp.zeros((half, HG_DV), f32))]
            for s in range(HG_STEP):
                for ti in range(s // half, 2):
                    qt, bt, ot = parts[ti]
                    p = qt * k[s:s + 1, :] * jnp.exp(bt - b[s:s + 1, :])
                    if s // half == ti:
                        p = jnp.where(row[:half] >= s - ti * half, p, 0.0)
                    parts[ti] = (qt, bt, ot + jnp.sum(p, axis=-1, keepdims=True) * v[s:s + 1, :])
            o = o + jnp.concatenate([parts[0][2], parts[1][2]], axis=0)
            kd = k * jnp.exp(b_last - b)
            st_ref[h] = jnp.exp(b_last) * st + lax.dot_general(
                v.astype(bf16), kd.astype(bf16), (((0,), (0,)), ((), ())), preferred_element_type=f32)
            o_ref[pl.ds(r0, HG_STEP), cs] = _hgrn_out(o, gain, g_ref[pl.ds(r0, HG_STEP), cs])
        return carry

    lax.fori_loop(0, n_steps, step, 0)

    @pl.when(tb == pl.num_programs(1) - 1)
    def _():
        for h in range(HG_HEADS):
            s_ref[h] = st_ref[h].T


def _hgrn_prompt(z, lb, gain, batch, seq, tt):
    nt = seq // tt
    blk = lambda seg: pl.BlockSpec((tt, HG_WIDTH), lambda b, t, seg=seg: (b * nt + t, seg))
    return pl.pallas_call(
        _hgrn_prompt_kernel,
        grid=(batch, nt),
        in_specs=[blk(0), blk(1), blk(2), blk(3),
                  pl.BlockSpec((1, HG_WIDTH), lambda b, t: (0, 0)),
                  pl.BlockSpec((1, HG_DV), lambda b, t: (0, 0))],
        out_specs=[pl.BlockSpec((tt, HG_WIDTH), lambda b, t: (b * nt + t, 0)),
                   pl.BlockSpec((None, HG_HEADS, HG_DK, HG_DV), lambda b, t: (b, 0, 0, 0))],
        out_shape=[jax.ShapeDtypeStruct((batch * seq, HG_WIDTH), f32),
                   jax.ShapeDtypeStruct((batch, HG_HEADS, HG_DK, HG_DV), f32)],
        scratch_shapes=[pltpu.VMEM((HG_HEADS, HG_DV, HG_DK), f32)],
        compiler_params=_cparams(("parallel", "arbitrary")),
        name="hgrn_prompt",
    )(z, z, z, z, lb.reshape(1, HG_WIDTH), gain.reshape(1, HG_DV))


def _bucket_table(max_dist):
    n = np.arange(max_dist + 1)
    max_exact = NUM_BUCKETS // 2

    def large(dtype):
        nf = np.maximum(n, 1).astype(dtype)
        v = np.log(nf / dtype(max_exact)) / dtype(math.log(MAX_DISTANCE / max_exact)) * dtype(NUM_BUCKETS - max_exact)
        return np.minimum(max_exact + v.astype(np.int32), NUM_BUCKETS - 1)

    lo, hi = large(np.float32), large(np.float64)
    assert (lo == hi).all(), "bucket boundaries must not depend on float rounding"
    return np.where(n < max_exact, n, lo).astype(np.int32)


def _bias_kernel(rb_ref, dist_ref, o_ref, *, thresholds):
    h = pl.program_id(0)
    dist = dist_ref[...]
    acc = jnp.full(dist.shape, rb_ref[h, 0], f32)
    for k, thr in thresholds:
        acc = jnp.where(dist >= thr, rb_ref[h, k], acc)
    o_ref[...] = acc


def _bias_by_dist(rel_bias, dist):
    dist = np.maximum(np.asarray(dist), 0).astype(np.int32)
    shape = dist.shape
    dist2 = dist.reshape(-1, shape[-1])
    table = _bucket_table(int(dist.max()))
    assert (np.diff(table) >= 0).all()
    thresholds = tuple((k, int(np.argmax(table >= k))) for k in range(1, NUM_BUCKETS) if (table >= k).any())
    n_heads = rel_bias.shape[0]
    out = pl.pallas_call(
        functools.partial(_bias_kernel, thresholds=thresholds),
        grid=(n_heads,),
        in_specs=[pl.BlockSpec(memory_space=pltpu.SMEM),
                  pl.BlockSpec(dist2.shape, lambda h: (0, 0))],
        out_specs=pl.BlockSpec((None,) + dist2.shape, lambda h: (h, 0, 0)),
        out_shape=jax.ShapeDtypeStruct((n_heads,) + dist2.shape, f32),
        compiler_params=_cparams(("parallel",)),
        name="rel_bias_table",
    )(rel_bias.astype(f32), jnp.asarray(dist2))
    return out.reshape((n_heads,) + shape)


def _split3(x):
    hi = x.astype(bf16)
    r1 = x - hi.astype(f32)
    mid = r1.astype(bf16)
    lo = (r1 - mid.astype(f32)).astype(bf16)
    return hi, mid, lo


def _dot_nt(a, b):
    return lax.dot_general(a, b, (((1,), (1,)), ((), ())), preferred_element_type=f32)


def _q_pad(q_ref_or_val, g, rows):
    q = q_ref_or_val
    z = jnp.zeros((rows, DH), f32)
    parts = []
    for h in range(HPG):
        c0 = (g * HPG + h) * DH
        qh = q[:, c0:c0 + DH]
        parts.append(jnp.concatenate([qh, z] if g % 2 == 0 else [z, qh], axis=1))
    return jnp.concatenate(parts, axis=0).astype(bf16)


def _compress_compute(xbuf, pe_ref, w1_ref, w1bd_ref, w2bd_ref, o_ref):
    ns = xbuf.shape[1]
    for c in range(2):
        pe_term = jnp.dot(pe_ref[c].astype(bf16), w1_ref[c], preferred_element_type=f32)
        pe_pair = jnp.concatenate([pe_term, pe_term], axis=1)
        for pr in range(KVH // 2):
            lanes = slice(c * KVH * DH + pr * LANE, c * KVH * DH + (pr + 1) * LANE)
            xs = jnp.concatenate([xbuf[s, :, lanes].astype(bf16) for s in range(CMP_STRIDE)], axis=1)
            acc = jnp.dot(xs, w1bd_ref[c], preferred_element_type=f32)
            hid = pe_pair + acc[:, :2 * CMP_HIDDEN] + pltpu.roll(acc[:, 2 * CMP_HIDDEN:], ns - 1, 0)
            o_ref[:, lanes] = jnp.dot(_silu(hid).astype(bf16), w2bd_ref[c], preferred_element_type=f32).astype(bf16)


def _stride_perm(page_rows):
    spp = page_rows // CMP_STRIDE
    perm = np.zeros((page_rows, page_rows), np.float32)
    for s in range(CMP_STRIDE):
        for n in range(spp):
            perm[s * spp + n, n * CMP_STRIDE + s] = 1.0
    return jnp.asarray(perm).astype(bf16)


def _scatter_page(xbuf, xp, r0, spp, lanes):
    for s in range(CMP_STRIDE):
        xbuf[s, pl.ds(r0, spp), lanes] = xp[s * spp:(s + 1) * spp, :]


def _compress_kernel(pt_ref, page_ref, perm_ref, pe_ref, w1_ref, w1bd_ref, w2bd_ref, o_ref, xbuf):
    p = pl.program_id(1)
    spp = page_ref.shape[0] // CMP_STRIDE
    xp = jnp.dot(perm_ref[...], page_ref[...].astype(bf16), preferred_element_type=f32)
    _scatter_page(xbuf, xp, pl.multiple_of(p * spp, spp), spp, slice(None))

    @pl.when(p == pl.num_programs(1) - 1)
    def _():
        _compress_compute(xbuf, pe_ref, w1_ref, w1bd_ref, w2bd_ref, o_ref)


PAGES_PER_STEP = 8


def _compress_paged_kernel(pt_ref, *refs):
    page_refs = refs[:PAGES_PER_STEP]
    perm_ref, pe_ref, w1_ref, w1bd_ref, w2bd_ref, o_ref, xbuf = refs[PAGES_PER_STEP:]
    p = pl.program_id(1)
    page_rows = page_refs[0].shape[2]
    spp = page_rows // CMP_STRIDE
    perm = perm_ref[...]
    for pi, page_ref in enumerate(page_refs):
        r0 = pl.multiple_of((p * PAGES_PER_STEP + pi) * spp, spp)
        for c in range(2):
            for pr in range(KVH // 2):
                xp = _dot_nt(perm, page_ref[c, pr * LANE:(pr + 1) * LANE, :].astype(bf16))
                _scatter_page(xbuf, xp, r0, spp, slice(c * KVH * DH + pr * LANE, c * KVH * DH + (pr + 1) * LANE))

    @pl.when(p == pl.num_programs(1) - 1)
    def _():
        _compress_compute(xbuf, pe_ref, w1_ref, w1bd_ref, w2bd_ref, o_ref)


def _compress_paged(pool_t, page_table, cmp_pe, cmp_w1, cmp_w2):
    batch, n_pages = page_table.shape
    page_rows = pool_t.shape[3]
    assert n_pages % PAGES_PER_STEP == 0 and page_rows == LANE
    ns = n_pages * page_rows // CMP_STRIDE
    pe, w1, w1bd, w2bd = _compress_weights(cmp_pe, cmp_w1, cmp_w2)
    perm = _stride_perm(page_rows)
    full = lambda a: pl.BlockSpec(a.shape, lambda b, p, pt: (0,) * a.ndim)
    page = lambda pi: pl.BlockSpec((None, 2, KVH * DH, page_rows),
                                   lambda b, p, pt, pi=pi: (pt[b * n_pages + p * PAGES_PER_STEP + pi], 0, 0, 0))
    return pl.pallas_call(
        _compress_paged_kernel,
        grid_spec=pltpu.PrefetchScalarGridSpec(
            num_scalar_prefetch=1,
            grid=(batch, n_pages // PAGES_PER_STEP),
            in_specs=[page(pi) for pi in range(PAGES_PER_STEP)]
                     + [full(perm), full(pe), full(w1), full(w1bd), full(w2bd)],
            out_specs=pl.BlockSpec((None, ns, KV_WIDTH), lambda b, p, pt: (b, 0, 0)),
            scratch_shapes=[pltpu.VMEM((CMP_STRIDE, ns, KV_WIDTH), f32)]),
        out_shape=jax.ShapeDtypeStruct((batch, ns, KV_WIDTH), bf16),
        compiler_params=_cparams(("parallel", "arbitrary"), V7X_VMEM_LIMIT),
        name="nsa_compress_paged",
    )(page_table.reshape(-1).astype(jnp.int32), *([pool_t] * PAGES_PER_STEP), perm, pe, w1, w1bd, w2bd)


def _compress_weights(cmp_pe, cmp_w1, cmp_w2):
    r = CMP_LEN // CMP_STRIDE
    w1r = cmp_w1.reshape(2, r, CMP_STRIDE, DH, CMP_HIDDEN)
    zero = jnp.zeros_like(w1r[:, 0])
    top = jnp.concatenate([w1r[:, 0], zero, w1r[:, 1], zero], axis=-1)
    bot = jnp.concatenate([zero, w1r[:, 0], zero, w1r[:, 1]], axis=-1)
    w1bd = jnp.concatenate([top, bot], axis=2).astype(bf16)
    w1bd = w1bd.reshape(2, CMP_STRIDE * LANE, 4 * CMP_HIDDEN)
    z2 = jnp.zeros_like(cmp_w2)
    w2bd = jnp.concatenate([jnp.concatenate([cmp_w2, z2], axis=-1),
                            jnp.concatenate([z2, cmp_w2], axis=-1)], axis=1).astype(bf16)
    pe = cmp_pe.reshape(2, 1, CMP_LEN * DH)
    return pe, cmp_w1.astype(bf16), w1bd, w2bd


def _compress(pool, page_table, page_rows, cmp_pe, cmp_w1, cmp_w2):
    batch, n_pages = page_table.shape
    ns = n_pages * page_rows // CMP_STRIDE
    pe, w1, w1bd, w2bd = _compress_weights(cmp_pe, cmp_w1, cmp_w2)
    perm = _stride_perm(page_rows)
    full = lambda a: pl.BlockSpec(a.shape, lambda b, p, pt: (0,) * a.ndim)
    return pl.pallas_call(
        _compress_kernel,
        grid_spec=pltpu.PrefetchScalarGridSpec(
            num_scalar_prefetch=1,
            grid=(batch, n_pages),
            in_specs=[pl.BlockSpec((None, page_rows, KV_WIDTH), lambda b, p, pt: (pt[b * n_pages + p], 0, 0)),
                      full(perm), full(pe), full(w1), full(w1bd), full(w2bd)],
            out_specs=pl.BlockSpec((None, ns, KV_WIDTH), lambda b, p, pt: (b, 0, 0)),
            scratch_shapes=[pltpu.VMEM((CMP_STRIDE, ns, KV_WIDTH), f32)]),
        out_shape=jax.ShapeDtypeStruct((batch, ns, KV_WIDTH), bf16),
        compiler_params=_cparams(("parallel", "arbitrary"), V7X_VMEM_LIMIT),
        name="nsa_compress",
    )(page_table.reshape(-1).astype(jnp.int32), pool, perm, pe, w1, w1bd, w2bd)


def _stride_to_block_map(ns, n_lanes, lane0):
    ratio = SLC_BLOCK // CMP_STRIDE
    m = np.zeros((ns, n_lanes), np.float32)
    for n in range(ns - (CMP_LEN // CMP_STRIDE - 1)):
        for st in range(n, n + CMP_LEN // CMP_STRIDE):
            if lane0 + st // ratio < n_lanes:
                m[n, lane0 + st // ratio] += 1.0
    return m


def _rank_select(score, jidx, n_keep):
    rank = jnp.zeros(score.shape, f32)
    for jp in range(score.shape[0]):
        row = score[jp:jp + 1, :]
        ahead = (row > score) | ((row == score) & (jidx > jp))
        rank = rank + jnp.where(ahead, 1.0, 0.0)
    return rank < n_keep


def _cmp_prompt_kernel(q_ref, gate_ref, kc_ref, bias_ref, map_ref, o_ref, sel_ref):
    i = pl.program_id(0)
    tq = q_ref.shape[0]
    ns = kc_ref.shape[0]
    nc = ns - (CMP_LEN // CMP_STRIDE - 1)
    q = q_ref[...]
    gates = _sigmoid(gate_ref[...])
    t_glob = i * tq + lax.broadcasted_iota(jnp.int32, (tq, ns), 0)
    n_idx = lax.broadcasted_iota(jnp.int32, (tq, ns), 1)
    valid1 = (t_glob >= n_idx * CMP_STRIDE + (CMP_LEN - 1)) & (n_idx < nc)
    valid = jnp.concatenate([valid1] * HPG, axis=0)
    pieces = []
    p_slc = jnp.zeros((tq, LANE), f32)
    for g in range(KVH):
        pair = slice((g // 2) * LANE, (g // 2 + 1) * LANE)
        vpair = slice(KVH * DH + (g // 2) * LANE, KVH * DH + (g // 2 + 1) * LANE)
        bias = jnp.concatenate([bias_ref[g * HPG + h] for h in range(HPG)], axis=0)
        s = _dot_nt(_q_pad(q, g, tq), kc_ref[:, pair]) * ATTN_SCALE + bias
        s = jnp.where(valid, s, NEG)
        m = jnp.max(s, axis=-1, keepdims=True)
        e = jnp.where(valid, jnp.exp(s - m), 0.0)
        p = e / jnp.maximum(jnp.sum(e, axis=-1, keepdims=True), 1e-30)
        o = jnp.dot(p.astype(bf16), kc_ref[:, vpair], preferred_element_type=f32)
        pg = p[0:tq]
        for h in range(HPG):
            col = (g * HPG + h) * 3
            oh = o[h * tq:(h + 1) * tq, (g % 2) * DH:(g % 2 + 1) * DH]
            pieces.append(gates[:, col:col + 1] * oh)
            if h:
                pg = pg + p[h * tq:(h + 1) * tq]
        for part in _split3(pg):
            p_slc = p_slc + jnp.dot(part, map_ref[g], preferred_element_type=f32)
    o_ref[...] = jnp.concatenate(pieces, axis=1)
    pt = p_slc.T
    n_blk_lanes = LANE // KVH
    jidx = lax.broadcasted_iota(jnp.int32, (n_blk_lanes, tq), 0)
    cur = (i * tq + lax.broadcasted_iota(jnp.int32, (n_blk_lanes, tq), 1)) // SLC_BLOCK
    forced = (jidx == 0) | ((jidx <= cur) & (jidx > cur - N_LOCAL_BLOCKS))
    sels = []
    for g in range(KVH):
        sc = pt[g * n_blk_lanes:(g + 1) * n_blk_lanes]
        sc = jnp.where(forced, jnp.inf, jnp.where(jidx > cur, -jnp.inf, sc))
        keep = _rank_select(sc, jidx, N_SEL) & (jidx <= cur)
        sels.append(jnp.where(keep, 1.0, 0.0))
    sel_ref[...] = jnp.concatenate(sels, axis=0).T


def _cmp_prompt(z, kc, rel_bias, batch, seq, tq):
    nt = seq // tq
    ns = kc.shape[1]
    n_blk = seq // SLC_BLOCK
    n_blk_lanes = LANE // KVH
    assert n_blk <= n_blk_lanes
    t = np.arange(seq)[:, None]
    dist = t - (np.arange(ns)[None, :] * CMP_STRIDE + CMP_LEN - 1)
    bias = _bias_by_dist(rel_bias, dist)
    smap = jnp.asarray(np.stack([_stride_to_block_map(ns, LANE, g * n_blk_lanes) for g in range(KVH)])).astype(bf16)
    nq = COL_Q // NSA_WIDTH
    return pl.pallas_call(
        _cmp_prompt_kernel,
        grid=(nt, batch),
        in_specs=[pl.BlockSpec((tq, NSA_WIDTH), lambda i, b: (b * nt + i, nq)),
                  pl.BlockSpec((tq, LANE), lambda i, b: (b * nt + i, COL_GATE // LANE)),
                  pl.BlockSpec((None, ns, KV_WIDTH), lambda i, b: (b, 0, 0)),
                  pl.BlockSpec((NSA_HEADS, tq, ns), lambda i, b: (0, i, 0)),
                  pl.BlockSpec((KVH, ns, LANE), lambda i, b: (0, 0, 0))],
        out_specs=[pl.BlockSpec((tq, NSA_WIDTH), lambda i, b: (b * nt + i, 0)),
                   pl.BlockSpec((tq, LANE), lambda i, b: (b * nt + i, 0))],
        out_shape=[jax.ShapeDtypeStruct((batch * seq, NSA_WIDTH), f32),
                   jax.ShapeDtypeStruct((batch * seq, LANE), f32)],
        compiler_params=_cparams(("parallel", "arbitrary")),
        name="nsa_cmp_prompt",
    )(z, z, kc, bias, smap)


def _flash_step_t(s, vt, m, l, acc):
    m_new = jnp.maximum(m, jnp.max(s, axis=0, keepdims=True))
    alpha = jnp.exp(m - m_new)
    p = jnp.exp(s - m_new)
    l = alpha * l + jnp.sum(p, axis=0, keepdims=True)
    acc[...] = alpha * acc[...] + jnp.dot(vt, p.astype(bf16), preferred_element_type=f32)
    return m_new, l


def _slcwin_prompt_kernel(q_ref, gate_ref, sel_ref, ocmp_ref, ks_ref, kw_ref, bias_ref, far_ref, exp_ref, o_ref,
                          ksb, kwb, vst, vwt, acc_s, acc_w, s_buf, pk_buf):
    i = pl.program_id(1)
    tq = q_ref.shape[0]
    cols = HPG * tq
    nt = vst.shape[0]
    half_w = KVH * DH

    @pl.when(i == 0)
    def _():
        ksb[...] = ks_ref[:, 0:half_w].astype(bf16)
        kwb[...] = kw_ref[:, 0:half_w].astype(bf16)
        for j in range(nt):
            vst[j] = ks_ref[j * tq:(j + 1) * tq, half_w:].T.astype(bf16)
            vwt[j] = kw_ref[j * tq:(j + 1) * tq, half_w:].T.astype(bf16)

    qs = q_ref[...] * ATTN_SCALE
    gates = _sigmoid(gate_ref[...])
    sel_t = sel_ref[...].T.astype(bf16)
    n_win_tiles = WINDOW // tq + 1
    init = (jnp.full((1, cols), NEG, f32), jnp.zeros((1, cols), f32))
    acc_s[...] = jnp.zeros_like(acc_s)
    acc_w[...] = jnp.zeros_like(acc_w)
    zpad = jnp.zeros((DH, tq), f32)
    kls = [slice((g // 2) * LANE, (g // 2 + 1) * LANE) for g in range(KVH)]
    qps = []
    for g in range(KVH):
        qt = qs[:, g * HPG * DH:(g + 1) * HPG * DH].T
        qps.append(jnp.concatenate(
            [jnp.concatenate([qt[h * DH:(h + 1) * DH], zpad] if g % 2 == 0 else [zpad, qt[h * DH:(h + 1) * DH]], axis=0)
             for h in range(HPG)], axis=1).astype(bf16))

    def scores(kbuf, j, g):
        return jnp.dot(kbuf[pl.ds(pl.multiple_of(j * tq, tq), tq), kls[g]], qps[g], preferred_element_type=f32)

    def slc_body(near):
        def body(j, carry):
            j_next = jnp.minimum(j + 1, i)
            out = []
            for g in range(KVH):
                s = s_buf[g] + (bias_ref[i - j, g] if near else far_ref[g])
                picked = pk_buf[g] > 0.5
                s_buf[g] = scores(ksb, j_next, g)
                pk_buf[g] = jnp.dot(exp_ref[g, j_next], sel_t, preferred_element_type=f32)
                s = jnp.concatenate([jnp.where(picked, s[:, h * tq:(h + 1) * tq], NEG) for h in range(HPG)], axis=1)
                out.append(_flash_step_t(s, vst[j, kls[g]], *carry[g], acc_s.at[g]))
            return tuple(out)
        return body

    for g in range(KVH):
        s_buf[g] = scores(ksb, 0, g)
        pk_buf[g] = jnp.dot(exp_ref[g, 0], sel_t, preferred_element_type=f32)
    n_far = jnp.maximum(i - 1, 0)
    slc = lax.fori_loop(0, n_far, slc_body(False), (init,) * KVH)
    slc = lax.fori_loop(n_far, i + 1, slc_body(True), slc)

    def win_body(kk, carry):
        j = i - kk
        j_next = jnp.maximum(j - 1, 0)
        if n_win_tiles == 3:
            kind = kk
        else:
            kind = jnp.where(kk == n_win_tiles - 1, 2, jnp.minimum(kk, 1))
        out = []
        for g in range(KVH):
            if n_win_tiles == 3:
                s = s_buf[g] + bias_ref[kind, g]
            else:
                s = s_buf[g] + jnp.where((kk >= 2) & (kk < n_win_tiles - 1), far_ref[g], bias_ref[kind, g])
            s_buf[g] = scores(kwb, j_next, g)
            out.append(_flash_step_t(s, vwt[j, kls[g]], *carry[g], acc_w.at[g]))
        return tuple(out)

    for g in range(KVH):
        s_buf[g] = scores(kwb, i, g)
    win = lax.fori_loop(0, jnp.minimum(i, n_win_tiles - 1) + 1, win_body, (init,) * KVH)
    pieces = []
    for g in range(KVH):
        o_s = (acc_s[g] / jnp.maximum(slc[g][1], 1e-30)).T
        o_w = (acc_w[g] / jnp.maximum(win[g][1], 1e-30)).T
        half = slice((g % 2) * DH, (g % 2 + 1) * DH)
        for h in range(HPG):
            col = (g * HPG + h) * 3
            hr = slice(h * tq, (h + 1) * tq)
            pieces.append(gates[:, col + 1:col + 2] * o_s[hr, half] + gates[:, col + 2:col + 3] * o_w[hr, half])
    o_ref[...] = ocmp_ref[...] + jnp.concatenate(pieces, axis=1)


def _slcwin_prompt(z, sel, ocmp, rel_bias, batch, seq, tq):
    nt = seq // tq
    n_blk_lanes = LANE // KVH
    n_win_tiles = WINDOW // tq + 1
    assert n_win_tiles >= 2 and WINDOW % tq == 0
    kk = np.array([0, 1, n_win_tiles - 1])[:, None, None]
    dist = kk * tq + np.arange(tq)[None, :, None] - np.arange(tq)[None, None, :]
    assert 2 * tq - (tq - 1) >= MAX_DISTANCE, "tile distance >= 2 must map to the last bucket"
    masked = dist < 0
    masked[2] |= dist[2] >= WINDOW
    bias = jnp.where(jnp.asarray(masked)[None], NEG, _bias_by_dist(rel_bias, dist))
    bias = bias.reshape(KVH, HPG, 3, tq, tq).transpose(2, 0, 4, 1, 3).reshape(3, KVH, tq, HPG * tq)
    far = _bias_by_dist(rel_bias, np.full((1, 1), MAX_DISTANCE))
    far = jnp.broadcast_to(far.reshape(KVH, 1, HPG, 1), (KVH, 1, HPG, tq)).reshape(KVH, 1, HPG * tq)
    ex = np.zeros((KVH, nt, tq, LANE), np.float32)
    for g in range(KVH):
        for j in range(nt):
            for s in range(tq):
                blk = (j * tq + s) // SLC_BLOCK
                if blk < n_blk_lanes:
                    ex[g, j, s, g * n_blk_lanes + blk] = 1.0
    ex = jnp.asarray(ex).astype(bf16)
    row = lambda w, c: pl.BlockSpec((tq, w), lambda b, i, c=c: (b * nt + i, c))
    return pl.pallas_call(
        _slcwin_prompt_kernel,
        grid=(batch, nt),
        in_specs=[row(NSA_WIDTH, COL_Q // NSA_WIDTH), row(LANE, COL_GATE // LANE),
                  pl.BlockSpec((tq, LANE), lambda b, i: (b * nt + i, 0)),
                  pl.BlockSpec((tq, NSA_WIDTH), lambda b, i: (b * nt + i, 0)),
                  pl.BlockSpec((seq, KV_WIDTH), lambda b, i: (b, COL_KVS // KV_WIDTH), pipeline_mode=pl.Buffered(1)),
                  pl.BlockSpec((seq, KV_WIDTH), lambda b, i: (b, COL_KVW // KV_WIDTH), pipeline_mode=pl.Buffered(1)),
                  pl.BlockSpec(bias.shape, lambda b, i: (0, 0, 0, 0), pipeline_mode=pl.Buffered(1)),
                  pl.BlockSpec(far.shape, lambda b, i: (0, 0, 0)),
                  pl.BlockSpec(ex.shape, lambda b, i: (0, 0, 0, 0), pipeline_mode=pl.Buffered(1))],
        out_specs=pl.BlockSpec((tq, NSA_WIDTH), lambda b, i: (b * nt + i, 0)),
        out_shape=jax.ShapeDtypeStruct((batch * seq, NSA_WIDTH), f32),
        scratch_shapes=[pltpu.VMEM((seq, KVH * DH), bf16), pltpu.VMEM((seq, KVH * DH), bf16),
                        pltpu.VMEM((nt, KVH * DH, tq), bf16), pltpu.VMEM((nt, KVH * DH, tq), bf16),
                        pltpu.VMEM((KVH, LANE, HPG * tq), f32), pltpu.VMEM((KVH, LANE, HPG * tq), f32),
                        pltpu.VMEM((KVH, tq, HPG * tq), f32), pltpu.VMEM((KVH, tq, tq), f32)],
        compiler_params=_cparams(("parallel", "arbitrary"), V7X_VMEM_LIMIT),
        name="nsa_slcwin_prompt",
    )(z, z, sel, ocmp, z, z, bias, far, ex)


ROUTE_GATE_LANE = 8


def _outproj_router_kernel(x_ref, ohg_ref, onsa_ref, wo_ref, g2_ref, rw_ref, rb_ref, x1_ref, xn_ref, route_ref):
    x1 = (x_ref[...]
          + jnp.dot(ohg_ref[...].astype(bf16), wo_ref[0:HG_WIDTH, :], preferred_element_type=f32)
          + jnp.dot(onsa_ref[...].astype(bf16), wo_ref[HG_WIDTH:, :], preferred_element_type=f32))
    x1_ref[...] = x1
    xn = x1 * lax.rsqrt(jnp.mean(x1 * x1, axis=-1, keepdims=True) + RMS_EPS) * g2_ref[...]
    xn_ref[...] = xn
    logits = jnp.dot(xn.astype(bf16), rw_ref[...], preferred_element_type=f32) + rb_ref[...]
    lane = lax.broadcasted_iota(jnp.int32, logits.shape, 1)
    route = jnp.zeros(logits.shape, f32)
    work = logits
    top = []
    for k in range(TOP_K):
        m = jnp.max(work, axis=-1, keepdims=True)
        idx = jnp.min(jnp.where(work == m, lane, LANE), axis=-1, keepdims=True)
        top.append(m)
        route = jnp.where(lane == k, idx.astype(f32), route)
        work = jnp.where(lane == idx, -jnp.inf, work)
    es = [jnp.exp(t - top[0]) for t in top]
    denom = es[0] + es[1] + es[2] + es[3]
    for k in range(TOP_K):
        route = jnp.where(lane == ROUTE_GATE_LANE + k, es[k] / denom, route)
    route_ref[...] = route


def _outproj_router(x, o_hg, o_nsa, wo_bf16, g2, rw_pad, rb_pad, tm):
    n, d = x.shape
    row = lambda w: pl.BlockSpec((tm, w), lambda i: (i, 0))
    full = lambda a: pl.BlockSpec(a.shape, lambda i: (0, 0))
    g2 = g2.reshape(1, d)
    return pl.pallas_call(
        _outproj_router_kernel,
        grid=(n // tm,),
        in_specs=[row(d), row(HG_WIDTH), row(NSA_WIDTH), full(wo_bf16), full(g2), full(rw_pad), full(rb_pad)],
        out_specs=[row(d), row(d), row(LANE)],
        out_shape=[jax.ShapeDtypeStruct((n, d), f32), jax.ShapeDtypeStruct((n, d), f32),
                   jax.ShapeDtypeStruct((n, LANE), f32)],
        compiler_params=_cparams(("parallel",), V7X_VMEM_LIMIT),
        name="outproj_router",
    )(x, o_hg, o_nsa, wo_bf16, g2, rw_pad, rb_pad)


MOE_ROWS = 2176
MOE_REGION = 544
MOE_TF = 256
ROW_DMA_UNROLL = 8


def _row_copies(n_rows, copy_fn):
    def start(r, c):
        for k in range(TOP_K):
            copy_fn(r, k).start(priority=k % 2)
        return c

    def wait(r, c):
        for k in range(TOP_K):
            copy_fn(r, k).wait()
        return c

    lax.fori_loop(0, n_rows, start, 0, unroll=ROW_DMA_UNROLL)
    lax.fori_loop(0, n_rows, wait, 0, unroll=ROW_DMA_UNROLL)


def _dispatch_kernel(slot_ref, xn_ref, xs_in_ref, xs_ref, sem):
    del xs_in_ref
    tb = xn_ref.shape[0]

    def copy(r, k):
        return pltpu.make_async_copy(xn_ref.at[pl.ds(r, 1)], xs_ref.at[pl.ds(slot_ref[r * TOP_K + k], 1)], sem)

    _row_copies(tb, copy)


def _dispatch(slots_flat, xn, xs, tb):
    n, d = xn.shape
    return pl.pallas_call(
        _dispatch_kernel,
        grid=(n // tb,),
        in_specs=[pl.BlockSpec((tb * TOP_K,), lambda i: (i,), memory_space=pltpu.SMEM),
                  pl.BlockSpec((tb, d), lambda i: (i, 0)),
                  pl.BlockSpec(memory_space=pl.ANY)],
        out_specs=pl.BlockSpec(memory_space=pl.ANY),
        out_shape=jax.ShapeDtypeStruct(xs.shape, xs.dtype),
        scratch_shapes=[pltpu.SemaphoreType.DMA(())],
        input_output_aliases={2: 0},
        compiler_params=_cparams(("arbitrary",)),
        name="moe_dispatch",
    )(slots_flat, xn, xs)


def _expert_kernel(ie_ref, ir_ref, x_ref, w1g_ref, w1u_ref, b1g_ref, b1u_ref, w2_ref, b2_ref, y_ref):
    m = pl.program_id(0)
    j = pl.program_id(1)
    rows = ir_ref[m]

    @pl.when(j == 0)
    def _():
        y_ref[...] = jnp.broadcast_to(b2_ref[...], y_ref.shape)

    @pl.when(rows > 0)
    def _():
        w1g = w1g_ref[...].astype(bf16)
        w1u = w1u_ref[...].astype(bf16)
        w2 = w2_ref[...].astype(bf16)
        for start in range(0, MOE_ROWS, MOE_REGION):
            rs = slice(start, min(start + MOE_REGION, MOE_ROWS))

            @pl.when(start < rows)
            def _():
                x = x_ref[rs, :].astype(bf16)
                hg = jnp.dot(x, w1g, preferred_element_type=f32) + b1g_ref[...]
                hu = jnp.dot(x, w1u, preferred_element_type=f32) + b1u_ref[...]
                gl = jnp.minimum(hg, SWIGLU_LIMIT)
                up = jnp.clip(hu, -SWIGLU_LIMIT, SWIGLU_LIMIT)
                act = (up + 1.0) * gl * _sigmoid(SWIGLU_ALPHA * gl)
                y_ref[rs, :] += jnp.dot(act.astype(bf16), w2, preferred_element_type=f32)


def _experts(item_e, item_rows, xs, w1, b1, w2, b2):
    n_items = item_e.shape[0]
    d = xs.shape[1]
    nf = D_FF // MOE_TF
    jj = lambda m, j, ir: jnp.where(ir[m] > 0, j, nf - 1)
    b1 = b1.reshape(N_EXPERTS, 1, 2 * D_FF)
    b2 = b2.reshape(N_EXPERTS, 1, d)
    return pl.pallas_call(
        _expert_kernel,
        grid_spec=pltpu.PrefetchScalarGridSpec(
            num_scalar_prefetch=2,
            grid=(n_items, nf),
            in_specs=[pl.BlockSpec((MOE_ROWS, d), lambda m, j, ie, ir: (m, 0), pipeline_mode=pl.Buffered(1)),
                      pl.BlockSpec((None, d, MOE_TF), lambda m, j, ie, ir: (ie[m], 0, jj(m, j, ir))),
                      pl.BlockSpec((None, d, MOE_TF), lambda m, j, ie, ir: (ie[m], 0, nf + jj(m, j, ir))),
                      pl.BlockSpec((None, 1, MOE_TF), lambda m, j, ie, ir: (ie[m], 0, jj(m, j, ir))),
                      pl.BlockSpec((None, 1, MOE_TF), lambda m, j, ie, ir: (ie[m], 0, nf + jj(m, j, ir))),
                      pl.BlockSpec((None, MOE_TF, d), lambda m, j, ie, ir: (ie[m], jj(m, j, ir), 0)),
                      pl.BlockSpec((None, 1, d), lambda m, j, ie, ir: (ie[m], 0, 0))],
            out_specs=pl.BlockSpec((MOE_ROWS, d), lambda m, j, ie, ir: (m, 0), pipeline_mode=pl.Buffered(1))),
        out_shape=jax.ShapeDtypeStruct(xs.shape, f32),
        compiler_params=_cparams(("arbitrary", "arbitrary"), V7X_VMEM_LIMIT),
        name="moe_experts",
    )(item_e, item_rows, xs, w1, w1, b1, b1, w2, b2)


def _combine_kernel(slot_ref, x1_ref, route_ref, gf_ref, ys_ref, y_ref, buf, sem):
    tb = x1_ref.shape[0]

    def copy(r, k):
        return pltpu.make_async_copy(ys_ref.at[pl.ds(slot_ref[r * TOP_K + k], 1)], buf.at[k, pl.ds(r, 1)], sem)

    _row_copies(tb, copy)
    route = route_ref[...]
    x2 = x1_ref[...]
    for k in range(TOP_K):
        x2 = x2 + route[:, ROUTE_GATE_LANE + k:ROUTE_GATE_LANE + k + 1] * buf[k]
    y_ref[...] = x2 * lax.rsqrt(jnp.mean(x2 * x2, axis=-1, keepdims=True) + RMS_EPS) * gf_ref[...]


def _combine(slots_flat, x1, route, gf, ys, tb):
    n, d = x1.shape
    return pl.pallas_call(
        _combine_kernel,
        grid=(n // tb,),
        in_specs=[pl.BlockSpec((tb * TOP_K,), lambda i: (i,), memory_space=pltpu.SMEM),
                  pl.BlockSpec((tb, d), lambda i: (i, 0)),
                  pl.BlockSpec((tb, LANE), lambda i: (i, 0)),
                  pl.BlockSpec((1, d), lambda i: (0, 0)),
                  pl.BlockSpec(memory_space=pl.ANY)],
        out_specs=pl.BlockSpec((tb, d), lambda i: (i, 0)),
        out_shape=jax.ShapeDtypeStruct((n, d), f32),
        scratch_shapes=[pltpu.VMEM((TOP_K, tb, d), f32), pltpu.SemaphoreType.DMA(())],
        compiler_params=_cparams(("arbitrary",), V7X_VMEM_LIMIT),
        name="moe_combine",
    )(slots_flat, x1, route, gf.reshape(1, d), ys)


def _routing_plan(top_e, n_items):
    flat_e = top_e.reshape(-1)
    onehot = (flat_e[:, None] == jnp.arange(N_EXPERTS, dtype=jnp.int32)[None, :]).astype(jnp.int32)
    csum = jnp.cumsum(onehot, axis=0)
    rank = jnp.sum(onehot * (csum - onehot), axis=1)
    counts = csum[-1]
    padded = (counts + MOE_ROWS - 1) // MOE_ROWS * MOE_ROWS
    pad_end = jnp.cumsum(padded)
    start = pad_end - padded
    slots = (start[flat_e] + rank).astype(jnp.int32)
    row0 = jnp.arange(n_items, dtype=jnp.int32) * MOE_ROWS
    item_e = jnp.minimum(jnp.searchsorted(pad_end, row0, side='right'), N_EXPERTS - 1).astype(jnp.int32)
    item_rows = jnp.clip(counts[item_e] - (row0 - start[item_e]), 0, MOE_ROWS).astype(jnp.int32)
    used = row0 < pad_end[-1]
    last_e = item_e[jnp.maximum(pad_end[-1] // MOE_ROWS - 1, 0)]
    item_e = jnp.where(used, item_e, last_e)
    item_rows = jnp.where(used, item_rows, 0)
    return slots, item_e, item_rows


SUB = 8


def _hgrn_sample_kernel(z_ref, lb_ref, gain_ref, s0_ref, o_ref, s_ref):
    gain = gain_ref[...]
    eye = lax.broadcasted_iota(jnp.int32, (HG_DK, HG_DK), 0) == lax.broadcasted_iota(jnp.int32, (HG_DK, HG_DK), 1)

    def column(rowvec):
        return jnp.sum(jnp.where(eye, jnp.broadcast_to(rowvec, (HG_DK, HG_DK)), 0.0), axis=-1, keepdims=True)

    for h in range(HG_HEADS):
        seg = lambda i, h=h: z_ref[:, i * HG_WIDTH + h * HG_DK:i * HG_WIDTH + (h + 1) * HG_DK]
        cs = slice(h * HG_DK, (h + 1) * HG_DK)
        q, k, g = _hgrn_gates(seg(0), seg(1), lb_ref[:, cs])
        v = seg(2)
        eg = jnp.exp(g)
        s0 = s0_ref[h]
        qe = jnp.broadcast_to(q * eg, (SUB, HG_DK)).astype(bf16)
        o = jnp.sum(q * k, axis=-1, keepdims=True) * v + jnp.dot(qe, s0.astype(bf16), preferred_element_type=f32)[0:1]
        s_ref[h] = column(eg) * s0 + column(k) * v
        o_ref[:, cs] = _hgrn_out(o, gain, seg(3))


def _row3(z):
    return z.reshape(z.shape[0], 1, z.shape[1])


def _hgrn_sample(z, lb, gain, s0):
    bs = z.shape[0]
    o, s = pl.pallas_call(
        _hgrn_sample_kernel,
        grid=(bs,),
        in_specs=[pl.BlockSpec((None, 1, 4 * HG_WIDTH), lambda b: (b, 0, 0)),
                  pl.BlockSpec((1, HG_WIDTH), lambda b: (0, 0)),
                  pl.BlockSpec((1, HG_DV), lambda b: (0, 0)),
                  pl.BlockSpec((None, HG_HEADS, HG_DK, HG_DV), lambda b: (b, 0, 0, 0))],
        out_specs=[pl.BlockSpec((None, 1, HG_WIDTH), lambda b: (b, 0, 0)),
                   pl.BlockSpec((None, HG_HEADS, HG_DK, HG_DV), lambda b: (b, 0, 0, 0))],
        out_shape=[jax.ShapeDtypeStruct((bs, 1, HG_WIDTH), f32), jax.ShapeDtypeStruct(s0.shape, f32)],
        compiler_params=_cparams(("parallel",)),
        name="hgrn_sample",
    )(_row3(z), lb.reshape(1, HG_WIDTH), gain.reshape(1, HG_DV), s0)
    return o.reshape(bs, HG_WIDTH), s


def _q_pad_row(q, g):
    qb = jnp.broadcast_to(q, (SUB, q.shape[1]))
    z = jnp.zeros((SUB, DH), f32)
    row = lax.broadcasted_iota(jnp.int32, (SUB, LANE), 0)
    out = jnp.zeros((SUB, LANE), f32)
    for h in range(HPG):
        c0 = (g * HPG + h) * DH
        piece = jnp.concatenate([qb[:, c0:c0 + DH], z] if g % 2 == 0 else [z, qb[:, c0:c0 + DH]], axis=1)
        out = jnp.where(row == h, piece, out)
    return out.astype(bf16)


def _head_pieces(o, g):
    half = slice((g % 2) * DH, (g % 2 + 1) * DH)
    return [o[h:h + 1, half] for h in range(HPG)]


def _cmp_sample_kernel(q_ref, gate_ref, kc_ref, bias_ref, map_ref, o_ref, idx_ref, *, cur, n_blk_lanes):
    q = q_ref[...]
    gates = _sigmoid(gate_ref[...])
    ns = kc_ref.shape[0]
    pieces = []
    idx_ref[...] = jnp.zeros(idx_ref.shape, jnp.int32)
    r_i = lax.broadcasted_iota(jnp.int32, (n_blk_lanes, n_blk_lanes), 0)
    c_i = lax.broadcasted_iota(jnp.int32, (n_blk_lanes, n_blk_lanes), 1)
    forced_c = (c_i == 0) | ((c_i <= cur) & (c_i > cur - N_LOCAL_BLOCKS))
    slot = lax.broadcasted_iota(jnp.int32, (n_blk_lanes, LANE), 1).astype(f32)
    blk_id = lax.broadcasted_iota(jnp.int32, (n_blk_lanes, LANE), 0)
    for g in range(KVH):
        pair = slice((g // 2) * LANE, (g // 2 + 1) * LANE)
        vpair = slice(KVH * DH + (g // 2) * LANE, KVH * DH + (g // 2 + 1) * LANE)
        s = _dot_nt(_q_pad_row(q, g), kc_ref[:, pair]) * ATTN_SCALE + bias_ref[g]
        valid = s > 0.5 * NEG
        m = jnp.max(s, axis=-1, keepdims=True)
        e = jnp.where(valid, jnp.exp(s - m), 0.0)
        p = e / jnp.maximum(jnp.sum(e, axis=-1, keepdims=True), 1e-30)
        o = jnp.dot(p.astype(bf16), kc_ref[:, vpair], preferred_element_type=f32)
        for h, oh in enumerate(_head_pieces(o, g)):
            col = (g * HPG + h) * 3
            pieces.append(gates[:, col:col + 1] * oh)
        pg = jnp.broadcast_to(jnp.sum(p[0:HPG], axis=0, keepdims=True), (SUB, ns))
        p_slc = jnp.zeros((SUB, n_blk_lanes), f32)
        for part in _split3(pg):
            p_slc = p_slc + jnp.dot(part, map_ref[...], preferred_element_type=f32)
        a = jnp.broadcast_to(p_slc[0:1], (n_blk_lanes, n_blk_lanes))
        a = jnp.where(forced_c, jnp.inf, jnp.where(c_i > cur, -jnp.inf, a))
        bt = a.T
        ahead = (a > bt) | ((a == bt) & (c_i < r_i))
        rank = jnp.sum(jnp.where(ahead, 1.0, 0.0), axis=-1, keepdims=True)
        hit = (rank == slot) & (blk_id <= cur)
        chosen = jnp.sum(jnp.where(hit, blk_id.astype(f32), 0.0), axis=0, keepdims=True)
        idx_ref[g:g + 1, :] = chosen.astype(jnp.int32)
    o_ref[...] = jnp.concatenate(pieces, axis=1)


def _cmp_sample(z, kc, rel_bias, q_pos):
    bs = z.shape[0]
    ns = kc.shape[1]
    nc = ns - (CMP_LEN // CMP_STRIDE - 1)
    n_blk = -(-(q_pos + 1) // SLC_BLOCK)
    assert n_blk >= N_SEL
    n_blk_lanes = -(-n_blk // LANE) * LANE
    k_end = np.arange(ns) * CMP_STRIDE + CMP_LEN - 1
    dist = q_pos - k_end
    bias = _bias_by_dist(rel_bias, dist)
    bias = jnp.where(jnp.asarray((dist >= 0) & (np.arange(ns) < nc))[None], bias, NEG)
    bias = jnp.pad(bias.reshape(KVH, HPG, ns), ((0, 0), (0, SUB - HPG), (0, 0)))
    smap = jnp.asarray(_stride_to_block_map(ns, n_blk_lanes, 0)).astype(bf16)
    kern = functools.partial(_cmp_sample_kernel, cur=q_pos // SLC_BLOCK, n_blk_lanes=n_blk_lanes)
    z3 = _row3(z)
    o, idx = pl.pallas_call(
        kern,
        grid=(bs,),
        in_specs=[pl.BlockSpec((None, 1, NSA_WIDTH), lambda b: (b, 0, COL_Q // NSA_WIDTH)),
                  pl.BlockSpec((None, 1, LANE), lambda b: (b, 0, COL_GATE // LANE)),
                  pl.BlockSpec((None, ns, KV_WIDTH), lambda b: (b, 0, 0)),
                  pl.BlockSpec(bias.shape, lambda b: (0, 0, 0)),
                  pl.BlockSpec(smap.shape, lambda b: (0, 0))],
        out_specs=[pl.BlockSpec((None, 1, NSA_WIDTH), lambda b: (b, 0, 0)),
                   pl.BlockSpec((None, SUB, LANE), lambda b: (b, 0, 0))],
        out_shape=[jax.ShapeDtypeStruct((bs, 1, NSA_WIDTH), f32), jax.ShapeDtypeStruct((bs, SUB, LANE), jnp.int32)],
        compiler_params=_cparams(("parallel",)),
        name="nsa_cmp_sample",
    )(z3, z3, kc, bias, smap)
    return o.reshape(bs, NSA_WIDTH), idx


def _q_rows(q, g):
    qb = jnp.broadcast_to(q, (SUB, q.shape[1]))
    row = lax.broadcasted_iota(jnp.int32, (SUB, DH), 0)
    out = jnp.zeros((SUB, DH), f32)
    for h in range(HPG):
        c0 = (g * HPG + h) * DH
        out = jnp.where(row == h, qb[:, c0:c0 + DH], out)
    return out


def _column(rowvec):
    n = rowvec.shape[1]
    eye = lax.broadcasted_iota(jnp.int32, (n, n), 0) == lax.broadcasted_iota(jnp.int32, (n, n), 1)
    return jnp.sum(jnp.where(eye, jnp.broadcast_to(rowvec, (n, n)), 0.0), axis=-1, keepdims=True)


SEL_PER_STEP = 8


def _slcwin_sample_kernel(idx_ref, pt_ref, q_ref, gate_ref, ocmp_ref, ksn_ref, kwn_ref, *refs, past, bpp):
    del pt_ref
    n_pool_refs = SEL_PER_STEP * KVH
    pools = refs[:n_pool_refs]
    win_ref, bslc_ref, bwin_ref, bnew_ref, o_ref, nwin_ref, qr_s, m_s, l_s, acc_s, ow_s = refs[n_pool_refs:]
    b = pl.program_id(0)
    k = pl.program_id(1)
    wlen = win_ref.shape[3]
    page_rows = pools[0].shape[2]
    half_w = KVH * DH

    @pl.when(k == 0)
    def _():
        q = q_ref[...] * ATTN_SCALE
        wnew = kwn_ref[...]
        lane = lax.broadcasted_iota(jnp.int32, (DH, wlen), 1)
        for g in range(KVH):
            knew = wnew[:, g * DH:(g + 1) * DH]
            vnew = wnew[:, half_w + g * DH:half_w + (g + 1) * DH]
            kt = win_ref[0, g]
            vt = win_ref[1, g]
            nwin_ref[0, g] = jnp.where(lane == wlen - 1, _column(knew), pltpu.roll(kt, wlen - 1, 1))
            nwin_ref[1, g] = jnp.where(lane == wlen - 1, _column(vnew), pltpu.roll(vt, wlen - 1, 1))
            qr = _q_rows(q, g).astype(bf16)
            qr_s[g] = qr
            m_s[g] = jnp.full((SUB, 1), NEG, f32)
            l_s[g] = jnp.zeros((SUB, 1), f32)
            acc_s[g] = jnp.zeros((SUB, DH), f32)
            s1 = jnp.dot(qr, kt.astype(bf16), preferred_element_type=f32) + bwin_ref[g]
            s2 = jnp.sum(qr.astype(f32) * knew.astype(bf16).astype(f32), axis=-1, keepdims=True) + bnew_ref[g][:, 0:1]
            ok = s1 > 0.5 * NEG
            mx = jnp.maximum(jnp.max(s1, axis=-1, keepdims=True), s2)
            e1 = jnp.where(ok, jnp.exp(s1 - mx), 0.0)
            e2 = jnp.exp(s2 - mx)
            den = jnp.maximum(jnp.sum(e1, axis=-1, keepdims=True) + e2, 1e-30)
            ow_s[g] = (_dot_nt(e1.astype(bf16), vt.astype(bf16))
                       + e2.astype(bf16).astype(f32) * vnew.astype(bf16).astype(f32)) / den

    snew = ksn_ref[...]
    lane_k = lax.broadcasted_iota(jnp.int32, (DH, page_rows), 1)
    lane_s = lax.broadcasted_iota(jnp.int32, (SUB, page_rows), 1)
    for g in range(KVH):
        kcol = _column(snew[:, g * DH:(g + 1) * DH])
        vcol = _column(snew[:, half_w + g * DH:half_w + (g + 1) * DH])
        kts, vts, biases, masks = [], [], [], []
        for kk in range(SEL_PER_STEP):
            blk = idx_ref[(b * KVH + g) * N_SEL + k * SEL_PER_STEP + kk]
            page = blk // bpp
            tile = pools[kk * KVH + g]
            fresh = page * page_rows + lane_k >= past
            kts.append(jnp.where(fresh, kcol, tile[0]).astype(bf16))
            vts.append(jnp.where(fresh, vcol, tile[1]).astype(bf16))
            biases.append(bslc_ref[page, g])
            masks.append((lane_s // SLC_BLOCK == blk % bpp) & (page * page_rows + lane_s <= past))
        mask = jnp.concatenate(masks, axis=1)
        s = jnp.dot(qr_s[g], jnp.concatenate(kts, axis=1), preferred_element_type=f32) + jnp.concatenate(biases, axis=1)
        s = jnp.where(mask, s, NEG)
        m_new = jnp.maximum(m_s[g], jnp.max(s, axis=-1, keepdims=True))
        alpha = jnp.exp(m_s[g] - m_new)
        p = jnp.where(mask, jnp.exp(s - m_new), 0.0)
        l_s[g] = alpha * l_s[g] + jnp.sum(p, axis=-1, keepdims=True)
        acc_s[g] = alpha * acc_s[g] + _dot_nt(p.astype(bf16), jnp.concatenate(vts, axis=1))
        m_s[g] = m_new

    @pl.when(k == pl.num_programs(1) - 1)
    def _():
        gates = _sigmoid(gate_ref[...])
        pieces = []
        for g in range(KVH):
            o_sl = acc_s[g] / jnp.maximum(l_s[g], 1e-30)
            o_w = ow_s[g]
            for h in range(HPG):
                col = (g * HPG + h) * 3
                pieces.append(gates[:, col + 1:col + 2] * o_sl[h:h + 1] + gates[:, col + 2:col + 3] * o_w[h:h + 1])
        o_ref[...] = ocmp_ref[...] + jnp.concatenate(pieces, axis=1)


def _slcwin_sample(z, ocmp, idx, pool_t, page_table, win_t, rel_bias, past):
    bs = z.shape[0]
    n_pages = page_table.shape[1]
    page_rows = pool_t.shape[4]
    bpp = page_rows // SLC_BLOCK
    wlen = win_t.shape[4]
    kpos = np.arange(n_pages + 1)[:, None] * page_rows + np.arange(page_rows)[None, :]
    bslc = _bias_by_dist(rel_bias, past - kpos)
    bslc = jnp.pad(bslc.reshape(KVH, HPG, n_pages + 1, page_rows),
                   ((0, 0), (0, SUB - HPG), (0, 0), (0, 0))).transpose(2, 0, 1, 3)
    wpos = past - wlen + np.arange(wlen)
    wdist = past - wpos
    bwin = jnp.where(jnp.asarray((wdist < WINDOW) & (wpos >= 0))[None], _bias_by_dist(rel_bias, wdist), NEG)
    bwin = jnp.pad(bwin.reshape(KVH, HPG, wlen), ((0, 0), (0, SUB - HPG), (0, 0)))
    bnew = jnp.broadcast_to(_bias_by_dist(rel_bias, np.zeros((1,), np.int64)).reshape(KVH, HPG, 1), (KVH, HPG, LANE))
    bnew = jnp.pad(bnew, ((0, 0), (0, SUB - HPG), (0, 0)))

    def pool_map(kk, g):
        def f(b, k, idx_r, pt_r):
            blk = idx_r[(b * KVH + g) * N_SEL + k * SEL_PER_STEP + kk]
            return (pt_r[b * n_pages + jnp.minimum(blk // bpp, n_pages - 1)], 0, g, 0, 0)
        return f

    rowblk = lambda w, c: pl.BlockSpec((None, 1, w), lambda b, k, i, p, c=c: (b, 0, c))
    full = lambda a: pl.BlockSpec(a.shape, lambda b, k, i, p: (0,) * a.ndim)
    win_spec = pl.BlockSpec((None, 2, KVH, DH, wlen), lambda b, k, i, p: (b, 0, 0, 0, 0))
    kern = functools.partial(_slcwin_sample_kernel, past=past, bpp=bpp)
    z3 = _row3(z)
    o, new_win_t = pl.pallas_call(
        kern,
        grid_spec=pltpu.PrefetchScalarGridSpec(
            num_scalar_prefetch=2,
            grid=(bs, N_SEL // SEL_PER_STEP),
            in_specs=[rowblk(NSA_WIDTH, COL_Q // NSA_WIDTH), rowblk(LANE, COL_GATE // LANE),
                      rowblk(NSA_WIDTH, 0),
                      rowblk(KV_WIDTH, COL_KVS // KV_WIDTH), rowblk(KV_WIDTH, COL_KVW // KV_WIDTH)]
                     + [pl.BlockSpec((None, 2, None, DH, page_rows), pool_map(kk, g))
                        for kk in range(SEL_PER_STEP) for g in range(KVH)]
                     + [win_spec, full(bslc), full(bwin), full(bnew)],
            out_specs=[rowblk(NSA_WIDTH, 0), win_spec],
            scratch_shapes=[pltpu.VMEM((KVH, SUB, DH), bf16), pltpu.VMEM((KVH, SUB, 1), f32),
                            pltpu.VMEM((KVH, SUB, 1), f32), pltpu.VMEM((KVH, SUB, DH), f32),
                            pltpu.VMEM((KVH, SUB, DH), f32)]),
        out_shape=[jax.ShapeDtypeStruct((bs, 1, NSA_WIDTH), f32), jax.ShapeDtypeStruct(win_t.shape, f32)],
        compiler_params=_cparams(("parallel", "arbitrary"), V7X_VMEM_LIMIT),
        name="nsa_slcwin_sample",
    )(idx[:, :KVH, :N_SEL].reshape(-1), page_table.reshape(-1).astype(jnp.int32), z3, z3, _row3(ocmp), z3, z3,
      *([pool_t] * (SEL_PER_STEP * KVH)), win_t, bslc, bwin, bnew)
    return o.reshape(bs, NSA_WIDTH), new_win_t


def kernel(x_prompt, x_sample, cache_kv_cmp, cache_kv_slc, state_win_kv, state_hgrn, page_table, norm1, w_in, hg_lower_bound, hg_norm, cmp_pe, cmp_w1, cmp_w2, rel_bias, w_out, norm2, router_w, router_b, moe_w1, moe_b1, moe_w2, moe_b2, norm_f):
    batch, seq, d = x_prompt.shape
    bs, dec_seq, _ = x_sample.shape
    assert norm1.shape[0] == 1 and dec_seq == 1
    n_pool, page_rows = cache_kv_cmp.shape[1:3]
    n_pages = page_table.shape[1]
    past = n_pages * page_rows
    wlen = state_win_kv.shape[2]
    assert wlen == WINDOW and past % CMP_STRIDE == 0 and seq % page_rows == 0

    lb = jnp.cumsum(jax.nn.softmax(hg_lower_bound.astype(f32), axis=0), axis=0)[0]
    w_in_p = jnp.pad(w_in[0], ((0, 0), (0, Z_WIDTH - IN_WIDTH))).astype(bf16)
    xp = x_prompt.reshape(batch * seq, d)
    xs = x_sample.reshape(bs, d)
    tq = min(128, seq)

    zp = _in_proj(xp, norm1[0], w_in_p, min(1024, batch * seq))
    o_hg_p, s_p = _hgrn_prompt(zp, lb, hg_norm[0], batch, seq, min(256, seq))
    kvc_p = zp[:, COL_KVC:COL_KVC + KV_WIDTH]
    kvs_p = zp[:, COL_KVS:COL_KVS + KV_WIDTH]
    kvw_p = zp[:, COL_KVW:COL_KVW + KV_WIDTH]
    ident = jnp.arange(batch * seq // page_rows, dtype=jnp.int32).reshape(batch, seq // page_rows)
    kc_p = _compress(kvc_p.reshape(-1, page_rows, KV_WIDTH), ident, page_rows, cmp_pe[0], cmp_w1[0], cmp_w2[0])
    ocmp_p, sel = _cmp_prompt(zp, kc_p, rel_bias, batch, seq, tq)
    o_nsa_p = _slcwin_prompt(zp, sel, ocmp_p, rel_bias, batch, seq, min(256, seq))

    rows_minor = lambda a: jnp.transpose(a, (0, 2, 3, 4, 1))
    zs = _in_proj(xs, norm1[0], w_in_p, bs)
    o_hg_s, s_s = _hgrn_sample(zs, lb, hg_norm[0], state_hgrn[0])
    kc_s = _compress_paged(rows_minor(cache_kv_cmp[0]).reshape(n_pool, 2, KVH * DH, page_rows), page_table,
                           cmp_pe[0], cmp_w1[0], cmp_w2[0])
    ocmp_s, idx = _cmp_sample(zs, kc_s, rel_bias, past)
    o_nsa_s, new_win_t = _slcwin_sample(zs, ocmp_s, idx, rows_minor(cache_kv_slc[0]), page_table,
                                        rows_minor(state_win_kv[0]), rel_bias, past)
    new_win = jnp.transpose(new_win_t, (0, 4, 1, 2, 3))

    wo = w_out[0].astype(bf16)
    rw = jnp.pad(router_w[0], ((0, 0), (0, LANE - N_EXPERTS))).astype(bf16)
    rb = jnp.pad(router_b[0].astype(f32), (0, LANE - N_EXPERTS), constant_values=NEG).reshape(1, LANE)
    x1_p, xn_p, route_p = _outproj_router(xp, o_hg_p, o_nsa_p, wo, norm2[0], rw, rb, min(256, batch * seq))
    x1_s, xn_s, route_s = _outproj_router(xs, o_hg_s, o_nsa_s, wo, norm2[0], rw, rb, bs)
    top_e = jnp.concatenate([route_p[:, :TOP_K], route_s[:, :TOP_K]], axis=0).astype(jnp.int32)
    n_tok = batch * seq + bs
    n_items = -(-n_tok * TOP_K // MOE_ROWS) + N_EXPERTS
    slots, item_e, item_rows = _routing_plan(top_e, n_items)
    slots_p, slots_s = slots[:batch * seq * TOP_K], slots[batch * seq * TOP_K:]
    xsort = jnp.zeros((n_items * MOE_ROWS, d), f32)
    xsort = _dispatch(slots_p, xn_p, xsort, min(256, batch * seq))
    xsort = _dispatch(slots_s, xn_s, xsort, bs)
    ysort = _experts(item_e, item_rows, xsort, moe_w1[0], moe_b1[0], moe_w2[0], moe_b2[0])
    y_p = _combine(slots_p, x1_p, route_p, norm_f, ysort, min(128, batch * seq))
    y_s = _combine(slots_s, x1_s, route_s, norm_f, ysort, bs)

    kv5 = lambda a, n, t: a.reshape(1, n, t, 2, KVH, DH)
    win_p = kvw_p.reshape(batch, seq, KV_WIDTH)[:, seq - min(WINDOW, seq):]
    return (y_p.reshape(batch, seq, d), y_s.reshape(bs, 1, d),
            kv5(kvc_p, batch, seq), kv5(kvs_p, batch, seq), kv5(win_p, batch, min(WINDOW, seq)), s_p[None],
            kv5(zs[:, COL_KVC:COL_KVC + KV_WIDTH], bs, 1), kv5(zs[:, COL_KVS:COL_KVS + KV_WIDTH], bs, 1),
            kv5(new_win, bs, wlen), s_s[None])
```

```python
import functools
import math

import jax
import jax.numpy as jnp
import numpy as np
from jax import lax
from jax.experimental import pallas as pl
from jax.experimental.pallas import tpu as pltpu

f32 = jnp.float32
bf16 = jnp.bfloat16

HG_HEADS, HG_DK, HG_DV = 8, 128, 128
HG_STEP = 16
NSA_HEADS, KVH, DH = 16, 4, 64
HPG = NSA_HEADS // KVH
CMP_LEN, CMP_STRIDE, CMP_HIDDEN = 32, 16, 128
SLC_BLOCK, N_SEL, N_LOCAL_BLOCKS, WINDOW = 64, 16, 2, 512
ATTN_SCALE = DH ** -0.5
NUM_BUCKETS, MAX_DISTANCE = 32, 128
N_EXPERTS, TOP_K, D_FF = 32, 4, 2048
SWIGLU_ALPHA, SWIGLU_LIMIT = 1.702, 7.0
RMS_EPS = 1e-5

HG_WIDTH = HG_HEADS * HG_DV
NSA_WIDTH = NSA_HEADS * DH
KV_WIDTH = 2 * KVH * DH
IN_SPLITS = (HG_WIDTH, HG_WIDTH, HG_WIDTH, HG_WIDTH, NSA_WIDTH, KV_WIDTH, KV_WIDTH, KV_WIDTH, NSA_HEADS * 3)
IN_WIDTH = sum(IN_SPLITS)
Z_WIDTH = 7168
COL_Q, COL_KVC, COL_KVS, COL_KVW, COL_GATE = 4096, 5120, 5632, 6144, 6656
LANE = 128
NEG = -1e30

V7X_VMEM_LIMIT = 56 * 1024 * 1024


def _cparams(sem, vmem=None):
    return pltpu.CompilerParams(dimension_semantics=sem, vmem_limit_bytes=vmem)


def _sigmoid(x):
    return 1.0 / (1.0 + jnp.exp(-x))


def _silu(x):
    return x * _sigmoid(x)


def _proj_kernel(x_ref, g_ref, w_ref, z_ref, hn_ref):
    @pl.when(pl.program_id(1) == 0)
    def _():
        x = x_ref[...]
        y = x * lax.rsqrt(jnp.mean(x * x, axis=-1, keepdims=True) + RMS_EPS) * g_ref[...]
        hn_ref[...] = y.astype(bf16)

    z_ref[...] = jnp.dot(hn_ref[...], w_ref[...], preferred_element_type=f32)


def _in_proj(x, gain, w_bf16, tm):
    n, d = x.shape
    tn = Z_WIDTH // 4
    return pl.pallas_call(
        _proj_kernel,
        grid=(n // tm, Z_WIDTH // tn),
        in_specs=[pl.BlockSpec((tm, d), lambda i, j: (i, 0)),
                  pl.BlockSpec((1, d), lambda i, j: (0, 0)),
                  pl.BlockSpec((d, tn), lambda i, j: (0, j))],
        out_specs=pl.BlockSpec((tm, tn), lambda i, j: (i, j)),
        out_shape=jax.ShapeDtypeStruct((n, Z_WIDTH), f32),
        scratch_shapes=[pltpu.VMEM((tm, d), bf16)],
        compiler_params=_cparams(("parallel", "arbitrary"), V7X_VMEM_LIMIT),
        name="in_proj",
    )(x, gain.reshape(1, d), w_bf16)


def _hgrn_gates(q_raw, f_raw, lb):
    q = _silu(q_raw)
    f = lb + (1.0 - lb) * _sigmoid(f_raw)
    return q, 1.0 - f, jnp.log(f)


def _hgrn_out(o, gain, g_raw):
    y = o * lax.rsqrt(jnp.mean(o * o, axis=-1, keepdims=True) + RMS_EPS) * gain
    return y * _silu(g_raw)


def _hgrn_prompt_kernel(q_ref, f_ref, i_ref, g_ref, lb_ref, gain_ref, o_ref, s_ref, st_ref):
    tb = pl.program_id(1)
    n_steps = q_ref.shape[0] // HG_STEP

    @pl.when(tb == 0)
    def _():
        st_ref[...] = jnp.zeros_like(st_ref)

    row = lax.broadcasted_iota(jnp.int32, (HG_STEP, HG_DK), 0)
    gain = gain_ref[...]

    def step(c, carry):
        r0 = pl.multiple_of(c * HG_STEP, HG_STEP)
        for h in range(HG_HEADS):
            cs = slice(h * HG_DK, (h + 1) * HG_DK)
            q, k, g = _hgrn_gates(q_ref[pl.ds(r0, HG_STEP), cs], f_ref[pl.ds(r0, HG_STEP), cs], lb_ref[:, cs])
            v = i_ref[pl.ds(r0, HG_STEP), cs]
            b = g
            for sh in (1, 2, 4, 8):
                b = b + jnp.where(row >= sh, pltpu.roll(b, sh, 0), 0.0)
            b_last = b[HG_STEP - 1:HG_STEP, :]
            st = st_ref[h]
            o = lax.dot_general((q * jnp.exp(b)).astype(bf16), st.astype(bf16),
                                (((1,), (1,)), ((), ())), preferred_element_type=f32)
            half = HG_STEP // 2
            parts = [(q[:half], b[:half], jnp.zeros((half, HG_DV), f32)), (q[half:], b[half:], jnp.zeros((half, HG_DV), f32))]
            for s in range(HG_STEP):
                for ti in range(s // half, 2):
                    qt, bt, ot = parts[ti]
                    p = qt * k[s:s + 1, :] * jnp.exp(bt - b[s:s + 1, :])
                    if s // half == ti:
                        p = jnp.where(row[:half] >= s - ti * half, p, 0.0)
                    parts[ti] = (qt, bt, ot + jnp.sum(p, axis=-1, keepdims=True) * v[s:s + 1, :])
            o = o + jnp.concatenate([parts[0][2], parts[1][2]], axis=0)
            kd = k * jnp.exp(b_last - b)
            st_ref[h] = jnp.exp(b_last) * st + lax.dot_general(
                v.astype(bf16), kd.astype(bf16), (((0,), (0,)), ((), ())), preferred_element_type=f32)
            o_ref[pl.ds(r0, HG_STEP), cs] = _hgrn_out(o, gain, g_ref[pl.ds(r0, HG_STEP), cs])
        return carry

    lax.fori_loop(0, n_steps, step, 0)

    @pl.when(tb == pl.num_programs(1) - 1)
    def _():
        for h in range(HG_HEADS):
            s_ref[h] = st_ref[h].T


def _hgrn_prompt(z, lb, gain, batch, seq, tt):
    nt = seq // tt
    blk = lambda seg: pl.BlockSpec((tt, HG_WIDTH), lambda b, t, seg=seg: (b * nt + t, seg))
    return pl.pallas_call(
        _hgrn_prompt_kernel,
        grid=(batch, nt),
        in_specs=[blk(0), blk(1), blk(2), blk(3),
                  pl.BlockSpec((1, HG_WIDTH), lambda b, t: (0, 0)),
                  pl.BlockSpec((1, HG_DV), lambda b, t: (0, 0))],
        out_specs=[pl.BlockSpec((tt, HG_WIDTH), lambda b, t: (b * nt + t, 0)),
                   pl.BlockSpec((None, HG_HEADS, HG_DK, HG_DV), lambda b, t: (b, 0, 0, 0))],
        out_shape=[jax.ShapeDtypeStruct((batch * seq, HG_WIDTH), f32),
                   jax.ShapeDtypeStruct((batch, HG_HEADS, HG_DK, HG_DV), f32)],
        scratch_shapes=[pltpu.VMEM((HG_HEADS, HG_DV, HG_DK), f32)],
        compiler_params=_cparams(("parallel", "arbitrary")),
        name="hgrn_prompt",
    )(z, z, z, z, lb.reshape(1, HG_WIDTH), gain.reshape(1, HG_DV))


def _bucket_table(max_dist):
    n = np.arange(max_dist + 1)
    max_exact = NUM_BUCKETS // 2

    def large(dtype):
        nf = np.maximum(n, 1).astype(dtype)
        v = np.log(nf / dtype(max_exact)) / dtype(math.log(MAX_DISTANCE / max_exact)) * dtype(NUM_BUCKETS - max_exact)
        return np.minimum(max_exact + v.astype(np.int32), NUM_BUCKETS - 1)

    lo, hi = large(np.float32), large(np.float64)
    assert (lo == hi).all(), "bucket boundaries must not depend on float rounding"
    return np.where(n < max_exact, n, lo).astype(np.int32)


def _bias_kernel(rb_ref, dist_ref, o_ref, *, thresholds):
    h = pl.program_id(0)
    dist = dist_ref[...]
    acc = jnp.full(dist.shape, rb_ref[h, 0], f32)
    for k, thr in thresholds:
        acc = jnp.where(dist >= thr, rb_ref[h, k], acc)
    o_ref[...] = acc


def _bias_by_dist(rel_bias, dist):
    dist = np.maximum(np.asarray(dist), 0).astype(np.int32)
    shape = dist.shape
    dist2 = dist.reshape(-1, shape[-1])
    table = _bucket_table(int(dist.max()))
    assert (np.diff(table) >= 0).all()
    thresholds = tuple((k, int(np.argmax(table >= k))) for k in range(1, NUM_BUCKETS) if (table >= k).any())
    n_heads = rel_bias.shape[0]
    out = pl.pallas_call(
        functools.partial(_bias_kernel, thresholds=thresholds),
        grid=(n_heads,),
        in_specs=[pl.BlockSpec(memory_space=pltpu.SMEM),
                  pl.BlockSpec(dist2.shape, lambda h: (0, 0))],
        out_specs=pl.BlockSpec((None,) + dist2.shape, lambda h: (h, 0, 0)),
        out_shape=jax.ShapeDtypeStruct((n_heads,) + dist2.shape, f32),
        compiler_params=_cparams(("parallel",)),
        name="rel_bias_table",
    )(rel_bias.astype(f32), jnp.asarray(dist2))
    return out.reshape((n_heads,) + shape)


def _split3(x):
    hi = x.astype(bf16)
    r1 = x - hi.astype(f32)
    mid = r1.astype(bf16)
    lo = (r1 - mid.astype(f32)).astype(bf16)
    return hi, mid, lo


def _dot_nt(a, b):
    return lax.dot_general(a, b, (((1,), (1,)), ((), ())), preferred_element_type=f32)


def _q_pad(q_ref_or_val, g, rows):
    q = q_ref_or_val
    z = jnp.zeros((rows, DH), f32)
    parts = []
    for h in range(HPG):
        c0 = (g * HPG + h) * DH
        qh = q[:, c0:c0 + DH]
        parts.append(jnp.concatenate([qh, z] if g % 2 == 0 else [z, qh], axis=1))
    return jnp.concatenate(parts, axis=0).astype(bf16)


def _compress_compute(xbuf, pe_ref, w1_ref, w1bd_ref, w2bd_ref, o_ref):
    ns = xbuf.shape[1]
    for c in range(2):
        pe_term = jnp.dot(pe_ref[c].astype(bf16), w1_ref[c], preferred_element_type=f32)
        pe_pair = jnp.concatenate([pe_term, pe_term], axis=1)
        for pr in range(KVH // 2):
            lanes = slice(c * KVH * DH + pr * LANE, c * KVH * DH + (pr + 1) * LANE)
            xs = jnp.concatenate([xbuf[s, :, lanes].astype(bf16) for s in range(CMP_STRIDE)], axis=1)
            acc = jnp.dot(xs, w1bd_ref[c], preferred_element_type=f32)
            hid = pe_pair + acc[:, :2 * CMP_HIDDEN] + pltpu.roll(acc[:, 2 * CMP_HIDDEN:], ns - 1, 0)
            o_ref[:, lanes] = jnp.dot(_silu(hid).astype(bf16), w2bd_ref[c], preferred_element_type=f32).astype(bf16)


def _stride_perm(page_rows):
    spp = page_rows // CMP_STRIDE
    perm = np.zeros((page_rows, page_rows), np.float32)
    for s in range(CMP_STRIDE):
        for n in range(spp):
            perm[s * spp + n, n * CMP_STRIDE + s] = 1.0
    return jnp.asarray(perm).astype(bf16)


def _scatter_page(xbuf, xp, r0, spp, lanes):
    for s in range(CMP_STRIDE):
        xbuf[s, pl.ds(r0, spp), lanes] = xp[s * spp:(s + 1) * spp, :]


def _compress_kernel(pt_ref, page_ref, perm_ref, pe_ref, w1_ref, w1bd_ref, w2bd_ref, o_ref, xbuf):
    p = pl.program_id(1)
    spp = page_ref.shape[0] // CMP_STRIDE
    xp = jnp.dot(perm_ref[...], page_ref[...].astype(bf16), preferred_element_type=f32)
    _scatter_page(xbuf, xp, pl.multiple_of(p * spp, spp), spp, slice(None))

    @pl.when(p == pl.num_programs(1) - 1)
    def _():
        _compress_compute(xbuf, pe_ref, w1_ref, w1bd_ref, w2bd_ref, o_ref)


PAGES_PER_STEP = 8


def _compress_paged_kernel(pt_ref, *refs):
    page_refs = refs[:PAGES_PER_STEP]
    perm_ref, pe_ref, w1_ref, w1bd_ref, w2bd_ref, o_ref, xbuf = refs[PAGES_PER_STEP:]
    p = pl.program_id(1)
    page_rows = page_refs[0].shape[2]
    spp = page_rows // CMP_STRIDE
    perm = perm_ref[...]
    for pi, page_ref in enumerate(page_refs):
        r0 = pl.multiple_of((p * PAGES_PER_STEP + pi) * spp, spp)
        for c in range(2):
            for pr in range(KVH // 2):
                xp = _dot_nt(perm, page_ref[c, pr * LANE:(pr + 1) * LANE, :].astype(bf16))
                _scatter_page(xbuf, xp, r0, spp, slice(c * KVH * DH + pr * LANE, c * KVH * DH + (pr + 1) * LANE))

    @pl.when(p == pl.num_programs(1) - 1)
    def _():
        _compress_compute(xbuf, pe_ref, w1_ref, w1bd_ref, w2bd_ref, o_ref)


def _compress_paged(pool_t, page_table, cmp_pe, cmp_w1, cmp_w2):
    batch, n_pages = page_table.shape
    page_rows = pool_t.shape[3]
    assert n_pages % PAGES_PER_STEP == 0 and page_rows == LANE
    ns = n_pages * page_rows // CMP_STRIDE
    pe, w1, w1bd, w2bd = _compress_weights(cmp_pe, cmp_w1, cmp_w2)
    perm = _stride_perm(page_rows)
    full = lambda a: pl.BlockSpec(a.shape, lambda b, p, pt: (0,) * a.ndim)
    page = lambda pi: pl.BlockSpec((None, 2, KVH * DH, page_rows),
                                   lambda b, p, pt, pi=pi: (pt[b * n_pages + p * PAGES_PER_STEP + pi], 0, 0, 0))
    return pl.pallas_call(
        _compress_paged_kernel,
        grid_spec=pltpu.PrefetchScalarGridSpec(
            num_scalar_prefetch=1,
            grid=(batch, n_pages // PAGES_PER_STEP),
            in_specs=[page(pi) for pi in range(PAGES_PER_STEP)]
                     + [full(perm), full(pe), full(w1), full(w1bd), full(w2bd)],
            out_specs=pl.BlockSpec((None, ns, KV_WIDTH), lambda b, p, pt: (b, 0, 0)),
            scratch_shapes=[pltpu.VMEM((CMP_STRIDE, ns, KV_WIDTH), f32)]),
        out_shape=jax.ShapeDtypeStruct((batch, ns, KV_WIDTH), bf16),
        compiler_params=_cparams(("parallel", "arbitrary"), V7X_VMEM_LIMIT),
        name="nsa_compress_paged",
    )(page_table.reshape(-1).astype(jnp.int32), *([pool_t] * PAGES_PER_STEP), perm, pe, w1, w1bd, w2bd)


def _compress_weights(cmp_pe, cmp_w1, cmp_w2):
    r = CMP_LEN // CMP_STRIDE
    w1r = cmp_w1.reshape(2, r, CMP_STRIDE, DH, CMP_HIDDEN)
    zero = jnp.zeros_like(w1r[:, 0])
    top = jnp.concatenate([w1r[:, 0], zero, w1r[:, 1], zero], axis=-1)
    bot = jnp.concatenate([zero, w1r[:, 0], zero, w1r[:, 1]], axis=-1)
    w1bd = jnp.concatenate([top, bot], axis=2).astype(bf16)
    w1bd = w1bd.reshape(2, CMP_STRIDE * LANE, 4 * CMP_HIDDEN)
    z2 = jnp.zeros_like(cmp_w2)
    w2bd = jnp.concatenate([jnp.concatenate([cmp_w2, z2], axis=-1),
                            jnp.concatenate([z2, cmp_w2], axis=-1)], axis=1).astype(bf16)
    pe = cmp_pe.reshape(2, 1, CMP_LEN * DH)
    return pe, cmp_w1.astype(bf16), w1bd, w2bd


def _compress(pool, page_table, page_rows, cmp_pe, cmp_w1, cmp_w2):
    batch, n_pages = page_table.shape
    ns = n_pages * page_rows // CMP_STRIDE
    pe, w1, w1bd, w2bd = _compress_weights(cmp_pe, cmp_w1, cmp_w2)
    perm = _stride_perm(page_rows)
    full = lambda a: pl.BlockSpec(a.shape, lambda b, p, pt: (0,) * a.ndim)
    return pl.pallas_call(
        _compress_kernel,
        grid_spec=pltpu.PrefetchScalarGridSpec(
            num_scalar_prefetch=1,
            grid=(batch, n_pages),
            in_specs=[pl.BlockSpec((None, page_rows, KV_WIDTH), lambda b, p, pt: (pt[b * n_pages + p], 0, 0)),
                      full(perm), full(pe), full(w1), full(w1bd), full(w2bd)],
            out_specs=pl.BlockSpec((None, ns, KV_WIDTH), lambda b, p, pt: (b, 0, 0)),
            scratch_shapes=[pltpu.VMEM((CMP_STRIDE, ns, KV_WIDTH), f32)]),
        out_shape=jax.ShapeDtypeStruct((batch, ns, KV_WIDTH), bf16),
        compiler_params=_cparams(("parallel", "arbitrary"), V7X_VMEM_LIMIT),
        name="nsa_compress",
    )(page_table.reshape(-1).astype(jnp.int32), pool, perm, pe, w1, w1bd, w2bd)


def _stride_to_block_map(ns, n_lanes, lane0):
    ratio = SLC_BLOCK // CMP_STRIDE
    m = np.zeros((ns, n_lanes), np.float32)
    for n in range(ns - (CMP_LEN // CMP_STRIDE - 1)):
        for st in range(n, n + CMP_LEN // CMP_STRIDE):
            if lane0 + st // ratio < n_lanes:
                m[n, lane0 + st // ratio] += 1.0
    return m


def _rank_select(score, jidx, n_keep):
    rank = jnp.zeros(score.shape, f32)
    for jp in range(score.shape[0]):
        row = score[jp:jp + 1, :]
        ahead = (row > score) | ((row == score) & (jidx > jp))
        rank = rank + jnp.where(ahead, 1.0, 0.0)
    return rank < n_keep


def _cmp_prompt_kernel(q_ref, gate_ref, kc_ref, bias_ref, map_ref, o_ref, sel_ref):
    i = pl.program_id(0)
    tq = q_ref.shape[0]
    ns = kc_ref.shape[0]
    nc = ns - (CMP_LEN // CMP_STRIDE - 1)
    q = q_ref[...]
    gates = _sigmoid(gate_ref[...])
    t_glob = i * tq + lax.broadcasted_iota(jnp.int32, (tq, ns), 0)
    n_idx = lax.broadcasted_iota(jnp.int32, (tq, ns), 1)
    valid1 = (t_glob >= n_idx * CMP_STRIDE + (CMP_LEN - 1)) & (n_idx < nc)
    valid = jnp.concatenate([valid1] * HPG, axis=0)
    pieces = []
    p_slc = jnp.zeros((tq, LANE), f32)
    for g in range(KVH):
        pair = slice((g // 2) * LANE, (g // 2 + 1) * LANE)
        vpair = slice(KVH * DH + (g // 2) * LANE, KVH * DH + (g // 2 + 1) * LANE)
        bias = jnp.concatenate([bias_ref[g * HPG + h] for h in range(HPG)], axis=0)
        s = _dot_nt(_q_pad(q, g, tq), kc_ref[:, pair]) * ATTN_SCALE + bias
        s = jnp.where(valid, s, NEG)
        m = jnp.max(s, axis=-1, keepdims=True)
        e = jnp.where(valid, jnp.exp(s - m), 0.0)
        p = e / jnp.maximum(jnp.sum(e, axis=-1, keepdims=True), 1e-30)
        o = jnp.dot(p.astype(bf16), kc_ref[:, vpair], preferred_element_type=f32)
        pg = p[0:tq]
        for h in range(HPG):
            col = (g * HPG + h) * 3
            oh = o[h * tq:(h + 1) * tq, (g % 2) * DH:(g % 2 + 1) * DH]
            pieces.append(gates[:, col:col + 1] * oh)
            if h:
                pg = pg + p[h * tq:(h + 1) * tq]
        for part in _split3(pg):
            p_slc = p_slc + jnp.dot(part, map_ref[g], preferred_element_type=f32)
    o_ref[...] = jnp.concatenate(pieces, axis=1)
    pt = p_slc.T
    n_blk_lanes = LANE // KVH
    jidx = lax.broadcasted_iota(jnp.int32, (n_blk_lanes, tq), 0)
    cur = (i * tq + lax.broadcasted_iota(jnp.int32, (n_blk_lanes, tq), 1)) // SLC_BLOCK
    forced = (jidx == 0) | ((jidx <= cur) & (jidx > cur - N_LOCAL_BLOCKS))
    sels = []
    for g in range(KVH):
        sc = pt[g * n_blk_lanes:(g + 1) * n_blk_lanes]
        sc = jnp.where(forced, jnp.inf, jnp.where(jidx > cur, -jnp.inf, sc))
        keep = _rank_select(sc, jidx, N_SEL) & (jidx <= cur)
        sels.append(jnp.where(keep, 1.0, 0.0))
    sel_ref[...] = jnp.concatenate(sels, axis=0).T


def _cmp_prompt(z, kc, rel_bias, batch, seq, tq):
    nt = seq // tq
    ns = kc.shape[1]
    n_blk = seq // SLC_BLOCK
    n_blk_lanes = LANE // KVH
    assert n_blk <= n_blk_lanes
    t = np.arange(seq)[:, None]
    dist = t - (np.arange(ns)[None, :] * CMP_STRIDE + CMP_LEN - 1)
    bias = _bias_by_dist(rel_bias, dist)
    smap = jnp.asarray(np.stack([_stride_to_block_map(ns, LANE, g * n_blk_lanes) for g in range(KVH)])).astype(bf16)
    nq = COL_Q // NSA_WIDTH
    return pl.pallas_call(
        _cmp_prompt_kernel,
        grid=(nt, batch),
        in_specs=[pl.BlockSpec((tq, NSA_WIDTH), lambda i, b: (b * nt + i, nq)),
                  pl.BlockSpec((tq, LANE), lambda i, b: (b * nt + i, COL_GATE // LANE)),
                  pl.BlockSpec((None, ns, KV_WIDTH), lambda i, b: (b, 0, 0)),
                  pl.BlockSpec((NSA_HEADS, tq, ns), lambda i, b: (0, i, 0)),
                  pl.BlockSpec((KVH, ns, LANE), lambda i, b: (0, 0, 0))],
        out_specs=[pl.BlockSpec((tq, NSA_WIDTH), lambda i, b: (b * nt + i, 0)),
                   pl.BlockSpec((tq, LANE), lambda i, b: (b * nt + i, 0))],
        out_shape=[jax.ShapeDtypeStruct((batch * seq, NSA_WIDTH), f32),
                   jax.ShapeDtypeStruct((batch * seq, LANE), f32)],
        compiler_params=_cparams(("parallel", "arbitrary")),
        name="nsa_cmp_prompt",
    )(z, z, kc, bias, smap)


def _flash_step_t(s, vt, m, l, acc):
    m_new = jnp.maximum(m, jnp.max(s, axis=0, keepdims=True))
    alpha = jnp.exp(m - m_new)
    p = jnp.exp(s - m_new)
    l = alpha * l + jnp.sum(p, axis=0, keepdims=True)
    acc[...] = alpha * acc[...] + jnp.dot(vt, p.astype(bf16), preferred_element_type=f32)
    return m_new, l


def _slcwin_prompt_kernel(q_ref, gate_ref, sel_ref, ocmp_ref, ks_ref, kw_ref, bias_ref, far_ref, exp_ref, o_ref,
                          ksb, kwb, vst, vwt, acc_s, acc_w, s_buf, pk_buf):
    i = pl.program_id(1)
    tq = q_ref.shape[0]
    cols = HPG * tq
    nt = vst.shape[0]
    half_w = KVH * DH

    @pl.when(i == 0)
    def _():
        ksb[...] = ks_ref[:, 0:half_w].astype(bf16)
        kwb[...] = kw_ref[:, 0:half_w].astype(bf16)
        for j in range(nt):
            vst[j] = ks_ref[j * tq:(j + 1) * tq, half_w:].T.astype(bf16)
            vwt[j] = kw_ref[j * tq:(j + 1) * tq, half_w:].T.astype(bf16)

    qs = q_ref[...] * ATTN_SCALE
    gates = _sigmoid(gate_ref[...])
    sel_t = sel_ref[...].T.astype(bf16)
    n_win_tiles = WINDOW // tq + 1
    init = (jnp.full((1, cols), NEG, f32), jnp.zeros((1, cols), f32))
    acc_s[...] = jnp.zeros_like(acc_s)
    acc_w[...] = jnp.zeros_like(acc_w)
    zpad = jnp.zeros((DH, tq), f32)
    kls = [slice((g // 2) * LANE, (g // 2 + 1) * LANE) for g in range(KVH)]
    qps = []
    for g in range(KVH):
        qt = qs[:, g * HPG * DH:(g + 1) * HPG * DH].T
        qps.append(jnp.concatenate(
            [jnp.concatenate([qt[h * DH:(h + 1) * DH], zpad] if g % 2 == 0 else [zpad, qt[h * DH:(h + 1) * DH]], axis=0)
             for h in range(HPG)], axis=1).astype(bf16))

    def scores(kbuf, j, g):
        return jnp.dot(kbuf[pl.ds(pl.multiple_of(j * tq, tq), tq), kls[g]], qps[g], preferred_element_type=f32)

    def slc_body(near):
        def body(j, carry):
            j_next = jnp.minimum(j + 1, i)
            out = []
            for g in range(KVH):
                s = s_buf[g] + (bias_ref[i - j, g] if near else far_ref[g])
                picked = pk_buf[g] > 0.5
                s_buf[g] = scores(ksb, j_next, g)
                pk_buf[g] = jnp.dot(exp_ref[g, j_next], sel_t, preferred_element_type=f32)
                s = jnp.concatenate([jnp.where(picked, s[:, h * tq:(h + 1) * tq], NEG) for h in range(HPG)], axis=1)
                out.append(_flash_step_t(s, vst[j, kls[g]], *carry[g], acc_s.at[g]))
            return tuple(out)
        return body

    for g in range(KVH):
        s_buf[g] = scores(ksb, 0, g)
        pk_buf[g] = jnp.dot(exp_ref[g, 0], sel_t, preferred_element_type=f32)
    n_far = jnp.maximum(i - 1, 0)
    slc = lax.fori_loop(0, n_far, slc_body(False), (init,) * KVH)
    slc = lax.fori_loop(n_far, i + 1, slc_body(True), slc)

    def win_body(kk, carry):
        j = i - kk
        j_next = jnp.maximum(j - 1, 0)
        if n_win_tiles == 3:
            kind = kk
        else:
            kind = jnp.where(kk == n_win_tiles - 1, 2, jnp.minimum(kk, 1))
        out = []
        for g in range(KVH):
            if n_win_tiles == 3:
                s = s_buf[g] + bias_ref[kind, g]
            else:
                s = s_buf[g] + jnp.where((kk >= 2) & (kk < n_win_tiles - 1), far_ref[g], bias_ref[kind, g])
            s_buf[g] = scores(kwb, j_next, g)
            out.append(_flash_step_t(s, vwt[j, kls[g]], *carry[g], acc_w.at[g]))
        return tuple(out)

    for g in range(KVH):
        s_buf[g] = scores(kwb, i, g)
    win = lax.fori_loop(0, jnp.minimum(i, n_win_tiles - 1) + 1, win_body, (init,) * KVH)
    pieces = []
    for g in range(KVH):
        o_s = (acc_s[g] / jnp.maximum(slc[g][1], 1e-30)).T
        o_w = (acc_w[g] / jnp.maximum(win[g][1], 1e-30)).T
        half = slice((g % 2) * DH, (g % 2 + 1) * DH)
        for h in range(HPG):
            col = (g * HPG + h) * 3
            hr = slice(h * tq, (h + 1) * tq)
            pieces.append(gates[:, col + 1:col + 2] * o_s[hr, half] + gates[:, col + 2:col + 3] * o_w[hr, half])
    o_ref[...] = ocmp_ref[...] + jnp.concatenate(pieces, axis=1)


def _slcwin_prompt(z, sel, ocmp, rel_bias, batch, seq, tq):
    nt = seq // tq
    n_blk_lanes = LANE // KVH
    n_win_tiles = WINDOW // tq + 1
    assert n_win_tiles >= 2 and WINDOW % tq == 0
    kk = np.array([0, 1, n_win_tiles - 1])[:, None, None]
    dist = kk * tq + np.arange(tq)[None, :, None] - np.arange(tq)[None, None, :]
    assert 2 * tq - (tq - 1) >= MAX_DISTANCE, "tile distance >= 2 must map to the last bucket"
    masked = dist < 0
    masked[2] |= dist[2] >= WINDOW
    bias = jnp.where(jnp.asarray(masked)[None], NEG, _bias_by_dist(rel_bias, dist))
    bias = bias.reshape(KVH, HPG, 3, tq, tq).transpose(2, 0, 4, 1, 3).reshape(3, KVH, tq, HPG * tq)
    far = _bias_by_dist(rel_bias, np.full((1, 1), MAX_DISTANCE))
    far = jnp.broadcast_to(far.reshape(KVH, 1, HPG, 1), (KVH, 1, HPG, tq)).reshape(KVH, 1, HPG * tq)
    ex = np.zeros((KVH, nt, tq, LANE), np.float32)
    for g in range(KVH):
        for j in range(nt):
            for s in range(tq):
                blk = (j * tq + s) // SLC_BLOCK
                if blk < n_blk_lanes:
                    ex[g, j, s, g * n_blk_lanes + blk] = 1.0
    ex = jnp.asarray(ex).astype(bf16)
    row = lambda w, c: pl.BlockSpec((tq, w), lambda b, i, c=c: (b * nt + i, c))
    return pl.pallas_call(
        _slcwin_prompt_kernel,
        grid=(batch, nt),
        in_specs=[row(NSA_WIDTH, COL_Q // NSA_WIDTH), row(LANE, COL_GATE // LANE),
                  pl.BlockSpec((tq, LANE), lambda b, i: (b * nt + i, 0)),
                  pl.BlockSpec((tq, NSA_WIDTH), lambda b, i: (b * nt + i, 0)),
                  pl.BlockSpec((seq, KV_WIDTH), lambda b, i: (b, COL_KVS // KV_WIDTH), pipeline_mode=pl.Buffered(1)),
                  pl.BlockSpec((seq, KV_WIDTH), lambda b, i: (b, COL_KVW // KV_WIDTH), pipeline_mode=pl.Buffered(1)),
                  pl.BlockSpec(bias.shape, lambda b, i: (0, 0, 0, 0), pipeline_mode=pl.Buffered(1)),
                  pl.BlockSpec(far.shape, lambda b, i: (0, 0, 0)),
                  pl.BlockSpec(ex.shape, lambda b, i: (0, 0, 0, 0), pipeline_mode=pl.Buffered(1))],
        out_specs=pl.BlockSpec((tq, NSA_WIDTH), lambda b, i: (b * nt + i, 0)),
        out_shape=jax.ShapeDtypeStruct((batch * seq, NSA_WIDTH), f32),
        scratch_shapes=[pltpu.VMEM((seq, KVH * DH), bf16), pltpu.VMEM((seq, KVH * DH), bf16),
                        pltpu.VMEM((nt, KVH * DH, tq), bf16), pltpu.VMEM((nt, KVH * DH, tq), bf16),
                        pltpu.VMEM((KVH, LANE, HPG * tq), f32), pltpu.VMEM((KVH, LANE, HPG * tq), f32),
                        pltpu.VMEM((KVH, tq, HPG * tq), f32), pltpu.VMEM((KVH, tq, tq), f32)],
        compiler_params=_cparams(("parallel", "arbitrary"), V7X_VMEM_LIMIT),
        name="nsa_slcwin_prompt",
    )(z, z, sel, ocmp, z, z, bias, far, ex)


ROUTE_GATE_LANE = 8


def _outproj_router_kernel(x_ref, ohg_ref, onsa_ref, wo_ref, g2_ref, rw_ref, rb_ref, x1_ref, xn_ref, route_ref):
    x1 = (x_ref[...]
          + jnp.dot(ohg_ref[...].astype(bf16), wo_ref[0:HG_WIDTH, :], preferred_element_type=f32)
          + jnp.dot(onsa_ref[...].astype(bf16), wo_ref[HG_WIDTH:, :], preferred_element_type=f32))
    x1_ref[...] = x1
    xn = x1 * lax.rsqrt(jnp.mean(x1 * x1, axis=-1, keepdims=True) + RMS_EPS) * g2_ref[...]
    xn_ref[...] = xn
    logits = jnp.dot(xn.astype(bf16), rw_ref[...], preferred_element_type=f32) + rb_ref[...]
    lane = lax.broadcasted_iota(jnp.int32, logits.shape, 1)
    route = jnp.zeros(logits.shape, f32)
    work = logits
    top = []
    for k in range(TOP_K):
        m = jnp.max(work, axis=-1, keepdims=True)
        idx = jnp.min(jnp.where(work == m, lane, LANE), axis=-1, keepdims=True)
        top.append(m)
        route = jnp.where(lane == k, idx.astype(f32), route)
        work = jnp.where(lane == idx, -jnp.inf, work)
    es = [jnp.exp(t - top[0]) for t in top]
    denom = es[0] + es[1] + es[2] + es[3]
    for k in range(TOP_K):
        route = jnp.where(lane == ROUTE_GATE_LANE + k, es[k] / denom, route)
    route_ref[...] = route


def _outproj_router(x, o_hg, o_nsa, wo_bf16, g2, rw_pad, rb_pad, tm):
    n, d = x.shape
    row = lambda w: pl.BlockSpec((tm, w), lambda i: (i, 0))
    full = lambda a: pl.BlockSpec(a.shape, lambda i: (0, 0))
    g2 = g2.reshape(1, d)
    return pl.pallas_call(
        _outproj_router_kernel,
        grid=(n // tm,),
        in_specs=[row(d), row(HG_WIDTH), row(NSA_WIDTH), full(wo_bf16), full(g2), full(rw_pad), full(rb_pad)],
        out_specs=[row(d), row(d), row(LANE)],
        out_shape=[jax.ShapeDtypeStruct((n, d), f32), jax.ShapeDtypeStruct((n, d), f32),
                   jax.ShapeDtypeStruct((n, LANE), f32)],
        compiler_params=_cparams(("parallel",), V7X_VMEM_LIMIT),
        name="outproj_router",
    )(x, o_hg, o_nsa, wo_bf16, g2, rw_pad, rb_pad)


MOE_ROWS = 1088
MOE_REGION = 544
MOE_TF = 256
ROW_DMA_UNROLL = 8


def _row_copies(n_rows, copy_fn):
    def start(r, c):
        for k in range(TOP_K):
            copy_fn(r, k).start(priority=k % 2)
        return c

    def wait(r, c):
        for k in range(TOP_K):
            copy_fn(r, k).wait()
        return c

    lax.fori_loop(0, n_rows, start, 0, unroll=ROW_DMA_UNROLL)
    lax.fori_loop(0, n_rows, wait, 0, unroll=ROW_DMA_UNROLL)


def _dispatch_kernel(slot_ref, xn_ref, xs_in_ref, xs_ref, sem):
    del xs_in_ref
    tb = xn_ref.shape[0]

    def copy(r, k):
        return pltpu.make_async_copy(xn_ref.at[pl.ds(r, 1)], xs_ref.at[pl.ds(slot_ref[r * TOP_K + k], 1)], sem)

    _row_copies(tb, copy)


def _dispatch(slots_flat, xn, xs, tb):
    n, d = xn.shape
    return pl.pallas_call(
        _dispatch_kernel,
        grid=(n // tb,),
        in_specs=[pl.BlockSpec((tb * TOP_K,), lambda i: (i,), memory_space=pltpu.SMEM),
                  pl.BlockSpec((tb, d), lambda i: (i, 0)),
                  pl.BlockSpec(memory_space=pl.ANY)],
        out_specs=pl.BlockSpec(memory_space=pl.ANY),
        out_shape=jax.ShapeDtypeStruct(xs.shape, xs.dtype),
        scratch_shapes=[pltpu.SemaphoreType.DMA(())],
        input_output_aliases={2: 0},
        compiler_params=_cparams(("arbitrary",)),
        name="moe_dispatch",
    )(slots_flat, xn, xs)


def _expert_kernel(ie_ref, ir_ref, x_ref, w1g_ref, w1u_ref, b1g_ref, b1u_ref, w2_ref, b2_ref, y_ref):
    m = pl.program_id(0)
    j = pl.program_id(1)
    rows = ir_ref[m]

    @pl.when(j == 0)
    def _():
        y_ref[...] = jnp.broadcast_to(b2_ref[...], y_ref.shape)

    @pl.when(rows > 0)
    def _():
        w1g = w1g_ref[...].astype(bf16)
        w1u = w1u_ref[...].astype(bf16)
        w2 = w2_ref[...].astype(bf16)
        for start in range(0, MOE_ROWS, MOE_REGION):
            rs = slice(start, min(start + MOE_REGION, MOE_ROWS))

            @pl.when(start < rows)
            def _():
                x = x_ref[rs, :].astype(bf16)
                hg = jnp.dot(x, w1g, preferred_element_type=f32) + b1g_ref[...]
                hu = jnp.dot(x, w1u, preferred_element_type=f32) + b1u_ref[...]
                gl = jnp.minimum(hg, SWIGLU_LIMIT)
                up = jnp.clip(hu, -SWIGLU_LIMIT, SWIGLU_LIMIT)
                act = (up + 1.0) * gl * _sigmoid(SWIGLU_ALPHA * gl)
                y_ref[rs, :] += jnp.dot(act.astype(bf16), w2, preferred_element_type=f32)


def _experts(item_e, item_rows, xs, w1, b1, w2, b2):
    n_items = item_e.shape[0]
    d = xs.shape[1]
    nf = D_FF // MOE_TF
    jj = lambda m, j, ir: jnp.where(ir[m] > 0, j, nf - 1)
    b1 = b1.reshape(N_EXPERTS, 1, 2 * D_FF)
    b2 = b2.reshape(N_EXPERTS, 1, d)
    return pl.pallas_call(
        _expert_kernel,
        grid_spec=pltpu.PrefetchScalarGridSpec(
            num_scalar_prefetch=2,
            grid=(n_items, nf),
            in_specs=[pl.BlockSpec((MOE_ROWS, d), lambda m, j, ie, ir: (m, 0)),
                      pl.BlockSpec((None, d, MOE_TF), lambda m, j, ie, ir: (ie[m], 0, jj(m, j, ir))),
                      pl.BlockSpec((None, d, MOE_TF), lambda m, j, ie, ir: (ie[m], 0, nf + jj(m, j, ir))),
                      pl.BlockSpec((None, 1, MOE_TF), lambda m, j, ie, ir: (ie[m], 0, jj(m, j, ir))),
                      pl.BlockSpec((None, 1, MOE_TF), lambda m, j, ie, ir: (ie[m], 0, nf + jj(m, j, ir))),
                      pl.BlockSpec((None, MOE_TF, d), lambda m, j, ie, ir: (ie[m], jj(m, j, ir), 0)),
                      pl.BlockSpec((None, 1, d), lambda m, j, ie, ir: (ie[m], 0, 0))],
            out_specs=pl.BlockSpec((MOE_ROWS, d), lambda m, j, ie, ir: (m, 0))),
        out_shape=jax.ShapeDtypeStruct(xs.shape, f32),
        compiler_params=_cparams(("arbitrary", "arbitrary"), V7X_VMEM_LIMIT),
        name="moe_experts",
    )(item_e, item_rows, xs, w1, w1, b1, b1, w2, b2)


def _combine_kernel(slot_ref, x1_ref, route_ref, gf_ref, ys_ref, y_ref, buf, sem):
    tb = x1_ref.shape[0]

    def copy(r, k):
        return pltpu.make_async_copy(ys_ref.at[pl.ds(slot_ref[r * TOP_K + k], 1)], buf.at[k, pl.ds(r, 1)], sem)

    _row_copies(tb, copy)
    route = route_ref[...]
    x2 = x1_ref[...]
    for k in range(TOP_K):
        x2 = x2 + route[:, ROUTE_GATE_LANE + k:ROUTE_GATE_LANE + k + 1] * buf[k]
    y_ref[...] = x2 * lax.rsqrt(jnp.mean(x2 * x2, axis=-1, keepdims=True) + RMS_EPS) * gf_ref[...]


def _combine(slots_flat, x1, route, gf, ys, tb):
    n, d = x1.shape
    return pl.pallas_call(
        _combine_kernel,
        grid=(n // tb,),
        in_specs=[pl.BlockSpec((tb * TOP_K,), lambda i: (i,), memory_space=pltpu.SMEM),
                  pl.BlockSpec((tb, d), lambda i: (i, 0)),
                  pl.BlockSpec((tb, LANE), lambda i: (i, 0)),
                  pl.BlockSpec((1, d), lambda i: (0, 0)),
                  pl.BlockSpec(memory_space=pl.ANY)],
        out_specs=pl.BlockSpec((tb, d), lambda i: (i, 0)),
        out_shape=jax.ShapeDtypeStruct((n, d), f32),
        scratch_shapes=[pltpu.VMEM((TOP_K, tb, d), f32), pltpu.SemaphoreType.DMA(())],
        compiler_params=_cparams(("arbitrary",), V7X_VMEM_LIMIT),
        name="moe_combine",
    )(slots_flat, x1, route, gf.reshape(1, d), ys)


def _routing_plan(top_e, n_items):
    flat_e = top_e.reshape(-1)
    onehot = (flat_e[:, None] == jnp.arange(N_EXPERTS, dtype=jnp.int32)[None, :]).astype(jnp.int32)
    csum = jnp.cumsum(onehot, axis=0)
    rank = jnp.sum(onehot * (csum - onehot), axis=1)
    counts = csum[-1]
    padded = (counts + MOE_ROWS - 1) // MOE_ROWS * MOE_ROWS
    pad_end = jnp.cumsum(padded)
    start = pad_end - padded
    slots = (start[flat_e] + rank).astype(jnp.int32)
    row0 = jnp.arange(n_items, dtype=jnp.int32) * MOE_ROWS
    item_e = jnp.minimum(jnp.searchsorted(pad_end, row0, side='right'), N_EXPERTS - 1).astype(jnp.int32)
    item_rows = jnp.clip(counts[item_e] - (row0 - start[item_e]), 0, MOE_ROWS).astype(jnp.int32)
    used = row0 < pad_end[-1]
    last_e = item_e[jnp.maximum(pad_end[-1] // MOE_ROWS - 1, 0)]
    item_e = jnp.where(used, item_e, last_e)
    item_rows = jnp.where(used, item_rows, 0)
    return slots, item_e, item_rows


SUB = 8


def _hgrn_sample_kernel(z_ref, lb_ref, gain_ref, s0_ref, o_ref, s_ref):
    gain = gain_ref[...]
    eye = lax.broadcasted_iota(jnp.int32, (HG_DK, HG_DK), 0) == lax.broadcasted_iota(jnp.int32, (HG_DK, HG_DK), 1)

    def column(rowvec):
        return jnp.sum(jnp.where(eye, jnp.broadcast_to(rowvec, (HG_DK, HG_DK)), 0.0), axis=-1, keepdims=True)

    for h in range(HG_HEADS):
        seg = lambda i, h=h: z_ref[:, i * HG_WIDTH + h * HG_DK:i * HG_WIDTH + (h + 1) * HG_DK]
        cs = slice(h * HG_DK, (h + 1) * HG_DK)
        q, k, g = _hgrn_gates(seg(0), seg(1), lb_ref[:, cs])
        v = seg(2)
        eg = jnp.exp(g)
        s0 = s0_ref[h]
        qe = jnp.broadcast_to(q * eg, (SUB, HG_DK)).astype(bf16)
        o = jnp.sum(q * k, axis=-1, keepdims=True) * v + jnp.dot(qe, s0.astype(bf16), preferred_element_type=f32)[0:1]
        s_ref[h] = column(eg) * s0 + column(k) * v
        o_ref[:, cs] = _hgrn_out(o, gain, seg(3))


def _row3(z):
    return z.reshape(z.shape[0], 1, z.shape[1])


def _hgrn_sample(z, lb, gain, s0):
    bs = z.shape[0]
    o, s = pl.pallas_call(
        _hgrn_sample_kernel,
        grid=(bs,),
        in_specs=[pl.BlockSpec((None, 1, 4 * HG_WIDTH), lambda b: (b, 0, 0)),
                  pl.BlockSpec((1, HG_WIDTH), lambda b: (0, 0)),
                  pl.BlockSpec((1, HG_DV), lambda b: (0, 0)),
                  pl.BlockSpec((None, HG_HEADS, HG_DK, HG_DV), lambda b: (b, 0, 0, 0))],
        out_specs=[pl.BlockSpec((None, 1, HG_WIDTH), lambda b: (b, 0, 0)),
                   pl.BlockSpec((None, HG_HEADS, HG_DK, HG_DV), lambda b: (b, 0, 0, 0))],
        out_shape=[jax.ShapeDtypeStruct((bs, 1, HG_WIDTH), f32), jax.ShapeDtypeStruct(s0.shape, f32)],
        compiler_params=_cparams(("parallel",)),
        name="hgrn_sample",
    )(_row3(z), lb.reshape(1, HG_WIDTH), gain.reshape(1, HG_DV), s0)
    return o.reshape(bs, HG_WIDTH), s


def _q_pad_row(q, g):
    qb = jnp.broadcast_to(q, (SUB, q.shape[1]))
    z = jnp.zeros((SUB, DH), f32)
    row = lax.broadcasted_iota(jnp.int32, (SUB, LANE), 0)
    out = jnp.zeros((SUB, LANE), f32)
    for h in range(HPG):
        c0 = (g * HPG + h) * DH
        piece = jnp.concatenate([qb[:, c0:c0 + DH], z] if g % 2 == 0 else [z, qb[:, c0:c0 + DH]], axis=1)
        out = jnp.where(row == h, piece, out)
    return out.astype(bf16)


def _head_pieces(o, g):
    half = slice((g % 2) * DH, (g % 2 + 1) * DH)
    return [o[h:h + 1, half] for h in range(HPG)]


def _cmp_sample_kernel(q_ref, gate_ref, kc_ref, bias_ref, map_ref, o_ref, idx_ref, *, cur, n_blk_lanes):
    q = q_ref[...]
    gates = _sigmoid(gate_ref[...])
    ns = kc_ref.shape[0]
    pieces = []
    idx_ref[...] = jnp.zeros(idx_ref.shape, jnp.int32)
    r_i = lax.broadcasted_iota(jnp.int32, (n_blk_lanes, n_blk_lanes), 0)
    c_i = lax.broadcasted_iota(jnp.int32, (n_blk_lanes, n_blk_lanes), 1)
    forced_c = (c_i == 0) | ((c_i <= cur) & (c_i > cur - N_LOCAL_BLOCKS))
    slot = lax.broadcasted_iota(jnp.int32, (n_blk_lanes, LANE), 1).astype(f32)
    blk_id = lax.broadcasted_iota(jnp.int32, (n_blk_lanes, LANE), 0)
    for g in range(KVH):
        pair = slice((g // 2) * LANE, (g // 2 + 1) * LANE)
        vpair = slice(KVH * DH + (g // 2) * LANE, KVH * DH + (g // 2 + 1) * LANE)
        s = _dot_nt(_q_pad_row(q, g), kc_ref[:, pair]) * ATTN_SCALE + bias_ref[g]
        valid = s > 0.5 * NEG
        m = jnp.max(s, axis=-1, keepdims=True)
        e = jnp.where(valid, jnp.exp(s - m), 0.0)
        p = e / jnp.maximum(jnp.sum(e, axis=-1, keepdims=True), 1e-30)
        o = jnp.dot(p.astype(bf16), kc_ref[:, vpair], preferred_element_type=f32)
        for h, oh in enumerate(_head_pieces(o, g)):
            col = (g * HPG + h) * 3
            pieces.append(gates[:, col:col + 1] * oh)
        pg = jnp.broadcast_to(jnp.sum(p[0:HPG], axis=0, keepdims=True), (SUB, ns))
        p_slc = jnp.zeros((SUB, n_blk_lanes), f32)
        for part in _split3(pg):
            p_slc = p_slc + jnp.dot(part, map_ref[...], preferred_element_type=f32)
        a = jnp.broadcast_to(p_slc[0:1], (n_blk_lanes, n_blk_lanes))
        a = jnp.where(forced_c, jnp.inf, jnp.where(c_i > cur, -jnp.inf, a))
        bt = a.T
        ahead = (a > bt) | ((a == bt) & (c_i < r_i))
        rank = jnp.sum(jnp.where(ahead, 1.0, 0.0), axis=-1, keepdims=True)
        hit = (rank == slot) & (blk_id <= cur)
        chosen = jnp.sum(jnp.where(hit, blk_id.astype(f32), 0.0), axis=0, keepdims=True)
        idx_ref[g:g + 1, :] = chosen.astype(jnp.int32)
    o_ref[...] = jnp.concatenate(pieces, axis=1)


def _cmp_sample(z, kc, rel_bias, q_pos):
    bs = z.shape[0]
    ns = kc.shape[1]
    nc = ns - (CMP_LEN // CMP_STRIDE - 1)
    n_blk = -(-(q_pos + 1) // SLC_BLOCK)
    assert n_blk >= N_SEL
    n_blk_lanes = -(-n_blk // LANE) * LANE
    k_end = np.arange(ns) * CMP_STRIDE + CMP_LEN - 1
    dist = q_pos - k_end
    bias = _bias_by_dist(rel_bias, dist)
    bias = jnp.where(jnp.asarray((dist >= 0) & (np.arange(ns) < nc))[None], bias, NEG)
    bias = jnp.pad(bias.reshape(KVH, HPG, ns), ((0, 0), (0, SUB - HPG), (0, 0)))
    smap = jnp.asarray(_stride_to_block_map(ns, n_blk_lanes, 0)).astype(bf16)
    kern = functools.partial(_cmp_sample_kernel, cur=q_pos // SLC_BLOCK, n_blk_lanes=n_blk_lanes)
    z3 = _row3(z)
    o, idx = pl.pallas_call(
        kern,
        grid=(bs,),
        in_specs=[pl.BlockSpec((None, 1, NSA_WIDTH), lambda b: (b, 0, COL_Q // NSA_WIDTH)),
                  pl.BlockSpec((None, 1, LANE), lambda b: (b, 0, COL_GATE // LANE)),
                  pl.BlockSpec((None, ns, KV_WIDTH), lambda b: (b, 0, 0)),
                  pl.BlockSpec(bias.shape, lambda b: (0, 0, 0)),
                  pl.BlockSpec(smap.shape, lambda b: (0, 0))],
        out_specs=[pl.BlockSpec((None, 1, NSA_WIDTH), lambda b: (b, 0, 0)),
                   pl.BlockSpec((None, SUB, LANE), lambda b: (b, 0, 0))],
        out_shape=[jax.ShapeDtypeStruct((bs, 1, NSA_WIDTH), f32), jax.ShapeDtypeStruct((bs, SUB, LANE), jnp.int32)],
        compiler_params=_cparams(("parallel",)),
        name="nsa_cmp_sample",
    )(z3, z3, kc, bias, smap)
    return o.reshape(bs, NSA_WIDTH), idx


def _q_rows(q, g):
    qb = jnp.broadcast_to(q, (SUB, q.shape[1]))
    row = lax.broadcasted_iota(jnp.int32, (SUB, DH), 0)
    out = jnp.zeros((SUB, DH), f32)
    for h in range(HPG):
        c0 = (g * HPG + h) * DH
        out = jnp.where(row == h, qb[:, c0:c0 + DH], out)
    return out


def _column(rowvec):
    n = rowvec.shape[1]
    eye = lax.broadcasted_iota(jnp.int32, (n, n), 0) == lax.broadcasted_iota(jnp.int32, (n, n), 1)
    return jnp.sum(jnp.where(eye, jnp.broadcast_to(rowvec, (n, n)), 0.0), axis=-1, keepdims=True)


SEL_PER_STEP = 8


def _slcwin_sample_kernel(idx_ref, pt_ref, q_ref, gate_ref, ocmp_ref, ksn_ref, kwn_ref, *refs, past, bpp):
    del pt_ref
    n_pool_refs = SEL_PER_STEP * KVH
    pools = refs[:n_pool_refs]
    win_ref, bslc_ref, bwin_ref, bnew_ref, o_ref, nwin_ref, qr_s, m_s, l_s, acc_s, ow_s = refs[n_pool_refs:]
    b = pl.program_id(0)
    k = pl.program_id(1)
    wlen = win_ref.shape[3]
    page_rows = pools[0].shape[2]
    half_w = KVH * DH

    @pl.when(k == 0)
    def _():
        q = q_ref[...] * ATTN_SCALE
        wnew = kwn_ref[...]
        lane = lax.broadcasted_iota(jnp.int32, (DH, wlen), 1)
        for g in range(KVH):
            knew = wnew[:, g * DH:(g + 1) * DH]
            vnew = wnew[:, half_w + g * DH:half_w + (g + 1) * DH]
            kt = win_ref[0, g]
            vt = win_ref[1, g]
            nwin_ref[0, g] = jnp.where(lane == wlen - 1, _column(knew), pltpu.roll(kt, wlen - 1, 1))
            nwin_ref[1, g] = jnp.where(lane == wlen - 1, _column(vnew), pltpu.roll(vt, wlen - 1, 1))
            qr = _q_rows(q, g).astype(bf16)
            qr_s[g] = qr
            m_s[g] = jnp.full((SUB, 1), NEG, f32)
            l_s[g] = jnp.zeros((SUB, 1), f32)
            acc_s[g] = jnp.zeros((SUB, DH), f32)
            s1 = jnp.dot(qr, kt.astype(bf16), preferred_element_type=f32) + bwin_ref[g]
            s2 = jnp.sum(qr.astype(f32) * knew.astype(bf16).astype(f32), axis=-1, keepdims=True) + bnew_ref[g][:, 0:1]
            ok = s1 > 0.5 * NEG
            mx = jnp.maximum(jnp.max(s1, axis=-1, keepdims=True), s2)
            e1 = jnp.where(ok, jnp.exp(s1 - mx), 0.0)
            e2 = jnp.exp(s2 - mx)
            den = jnp.maximum(jnp.sum(e1, axis=-1, keepdims=True) + e2, 1e-30)
            ow_s[g] = (_dot_nt(e1.astype(bf16), vt.astype(bf16))
                       + e2.astype(bf16).astype(f32) * vnew.astype(bf16).astype(f32)) / den

    snew = ksn_ref[...]
    lane_k = lax.broadcasted_iota(jnp.int32, (DH, page_rows), 1)
    lane_s = lax.broadcasted_iota(jnp.int32, (SUB, page_rows), 1)
    for g in range(KVH):
        kcol = _column(snew[:, g * DH:(g + 1) * DH])
        vcol = _column(snew[:, half_w + g * DH:half_w + (g + 1) * DH])
        kts, vts, biases, masks = [], [], [], []
        for kk in range(SEL_PER_STEP):
            blk = idx_ref[(b * KVH + g) * N_SEL + k * SEL_PER_STEP + kk]
            page = blk // bpp
            tile = pools[kk * KVH + g]
            fresh = page * page_rows + lane_k >= past
            kts.append(jnp.where(fresh, kcol, tile[0]).astype(bf16))
            vts.append(jnp.where(fresh, vcol, tile[1]).astype(bf16))
            biases.append(bslc_ref[page, g])
            masks.append((lane_s // SLC_BLOCK == blk % bpp) & (page * page_rows + lane_s <= past))
        mask = jnp.concatenate(masks, axis=1)
        s = jnp.dot(qr_s[g], jnp.concatenate(kts, axis=1), preferred_element_type=f32) + jnp.concatenate(biases, axis=1)
        s = jnp.where(mask, s, NEG)
        m_new = jnp.maximum(m_s[g], jnp.max(s, axis=-1, keepdims=True))
        alpha = jnp.exp(m_s[g] - m_new)
        p = jnp.where(mask, jnp.exp(s - m_new), 0.0)
        l_s[g] = alpha * l_s[g] + jnp.sum(p, axis=-1, keepdims=True)
        acc_s[g] = alpha * acc_s[g] + _dot_nt(p.astype(bf16), jnp.concatenate(vts, axis=1))
        m_s[g] = m_new

    @pl.when(k == pl.num_programs(1) - 1)
    def _():
        gates = _sigmoid(gate_ref[...])
        pieces = []
        for g in range(KVH):
            o_sl = acc_s[g] / jnp.maximum(l_s[g], 1e-30)
            o_w = ow_s[g]
            for h in range(HPG):
                col = (g * HPG + h) * 3
                pieces.append(gates[:, col + 1:col + 2] * o_sl[h:h + 1] + gates[:, col + 2:col + 3] * o_w[h:h + 1])
        o_ref[...] = ocmp_ref[...] + jnp.concatenate(pieces, axis=1)


def _slcwin_sample(z, ocmp, idx, pool_t, page_table, win_t, rel_bias, past):
    bs = z.shape[0]
    n_pages = page_table.shape[1]
    page_rows = pool_t.shape[4]
    bpp = page_rows // SLC_BLOCK
    wlen = win_t.shape[4]
    kpos = np.arange(n_pages + 1)[:, None] * page_rows + np.arange(page_rows)[None, :]
    bslc = _bias_by_dist(rel_bias, past - kpos)
    bslc = jnp.pad(bslc.reshape(KVH, HPG, n_pages + 1, page_rows),
                   ((0, 0), (0, SUB - HPG), (0, 0), (0, 0))).transpose(2, 0, 1, 3)
    wpos = past - wlen + np.arange(wlen)
    wdist = past - wpos
    bwin = jnp.where(jnp.asarray((wdist < WINDOW) & (wpos >= 0))[None], _bias_by_dist(rel_bias, wdist), NEG)
    bwin = jnp.pad(bwin.reshape(KVH, HPG, wlen), ((0, 0), (0, SUB - HPG), (0, 0)))
    bnew = jnp.broadcast_to(_bias_by_dist(rel_bias, np.zeros((1,), np.int64)).reshape(KVH, HPG, 1), (KVH, HPG, LANE))
    bnew = jnp.pad(bnew, ((0, 0), (0, SUB - HPG), (0, 0)))

    def pool_map(kk, g):
        def f(b, k, idx_r, pt_r):
            blk = idx_r[(b * KVH + g) * N_SEL + k * SEL_PER_STEP + kk]
            return (pt_r[b * n_pages + jnp.minimum(blk // bpp, n_pages - 1)], 0, g, 0, 0)
        return f

    rowblk = lambda w, c: pl.BlockSpec((None, 1, w), lambda b, k, i, p, c=c: (b, 0, c))
    full = lambda a: pl.BlockSpec(a.shape, lambda b, k, i, p: (0,) * a.ndim)
    win_spec = pl.BlockSpec((None, 2, KVH, DH, wlen), lambda b, k, i, p: (b, 0, 0, 0, 0))
    kern = functools.partial(_slcwin_sample_kernel, past=past, bpp=bpp)
    z3 = _row3(z)
    o, new_win_t = pl.pallas_call(
        kern,
        grid_spec=pltpu.PrefetchScalarGridSpec(
            num_scalar_prefetch=2,
            grid=(bs, N_SEL // SEL_PER_STEP),
            in_specs=[rowblk(NSA_WIDTH, COL_Q // NSA_WIDTH), rowblk(LANE, COL_GATE // LANE),
                      rowblk(NSA_WIDTH, 0),
                      rowblk(KV_WIDTH, COL_KVS // KV_WIDTH), rowblk(KV_WIDTH, COL_KVW // KV_WIDTH)]
                     + [pl.BlockSpec((None, 2, None, DH, page_rows), pool_map(kk, g))
                        for kk in range(SEL_PER_STEP) for g in range(KVH)]
                     + [win_spec, full(bslc), full(bwin), full(bnew)],
            out_specs=[rowblk(NSA_WIDTH, 0), win_spec],
            scratch_shapes=[pltpu.VMEM((KVH, SUB, DH), bf16), pltpu.VMEM((KVH, SUB, 1), f32),
                            pltpu.VMEM((KVH, SUB, 1), f32), pltpu.VMEM((KVH, SUB, DH), f32),
                            pltpu.VMEM((KVH, SUB, DH), f32)]),
        out_shape=[jax.ShapeDtypeStruct((bs, 1, NSA_WIDTH), f32), jax.ShapeDtypeStruct(win_t.shape, f32)],
        compiler_params=_cparams(("parallel", "arbitrary"), V7X_VMEM_LIMIT),
        name="nsa_slcwin_sample",
    )(idx[:, :KVH, :N_SEL].reshape(-1), page_table.reshape(-1).astype(jnp.int32), z3, z3, _row3(ocmp), z3, z3,
      *([pool_t] * (SEL_PER_STEP * KVH)), win_t, bslc, bwin, bnew)
    return o.reshape(bs, NSA_WIDTH), new_win_t


def kernel(x_prompt, x_sample, cache_kv_cmp, cache_kv_slc, state_win_kv, state_hgrn, page_table, norm1, w_in, hg_lower_bound, hg_norm, cmp_pe, cmp_w1, cmp_w2, rel_bias, w_out, norm2, router_w, router_b, moe_w1, moe_b1, moe_w2, moe_b2, norm_f):
    batch, seq, d = x_prompt.shape
    bs, dec_seq, _ = x_sample.shape
    assert norm1.shape[0] == 1 and dec_seq == 1
    n_pool, page_rows = cache_kv_cmp.shape[1:3]
    n_pages = page_table.shape[1]
    past = n_pages * page_rows
    wlen = state_win_kv.shape[2]
    assert wlen == WINDOW and past % CMP_STRIDE == 0 and seq % page_rows == 0

    lb = jnp.cumsum(jax.nn.softmax(hg_lower_bound.astype(f32), axis=0), axis=0)[0]
    w_in_p = jnp.pad(w_in[0], ((0, 0), (0, Z_WIDTH - IN_WIDTH))).astype(bf16)
    xp = x_prompt.reshape(batch * seq, d)
    xs = x_sample.reshape(bs, d)
    tq = min(128, seq)

    zp = _in_proj(xp, norm1[0], w_in_p, min(1024, batch * seq))
    o_hg_p, s_p = _hgrn_prompt(zp, lb, hg_norm[0], batch, seq, min(256, seq))
    kvc_p = zp[:, COL_KVC:COL_KVC + KV_WIDTH]
    kvs_p = zp[:, COL_KVS:COL_KVS + KV_WIDTH]
    kvw_p = zp[:, COL_KVW:COL_KVW + KV_WIDTH]
    ident = jnp.arange(batch * seq // page_rows, dtype=jnp.int32).reshape(batch, seq // page_rows)
    kc_p = _compress(kvc_p.reshape(-1, page_rows, KV_WIDTH), ident, page_rows, cmp_pe[0], cmp_w1[0], cmp_w2[0])
    ocmp_p, sel = _cmp_prompt(zp, kc_p, rel_bias, batch, seq, tq)
    o_nsa_p = _slcwin_prompt(zp, sel, ocmp_p, rel_bias, batch, seq, min(256, seq))

    rows_minor = lambda a: jnp.transpose(a, (0, 2, 3, 4, 1))
    zs = _in_proj(xs, norm1[0], w_in_p, bs)
    o_hg_s, s_s = _hgrn_sample(zs, lb, hg_norm[0], state_hgrn[0])
    kc_s = _compress_paged(rows_minor(cache_kv_cmp[0]).reshape(n_pool, 2, KVH * DH, page_rows), page_table,
                           cmp_pe[0], cmp_w1[0], cmp_w2[0])
    ocmp_s, idx = _cmp_sample(zs, kc_s, rel_bias, past)
    o_nsa_s, new_win_t = _slcwin_sample(zs, ocmp_s, idx, rows_minor(cache_kv_slc[0]), page_table,
                                        rows_minor(state_win_kv[0]), rel_bias, past)
    new_win = jnp.transpose(new_win_t, (0, 4, 1, 2, 3))

    wo = w_out[0].astype(bf16)
    rw = jnp.pad(router_w[0], ((0, 0), (0, LANE - N_EXPERTS))).astype(bf16)
    rb = jnp.pad(router_b[0].astype(f32), (0, LANE - N_EXPERTS), constant_values=NEG).reshape(1, LANE)
    x1_p, xn_p, route_p = _outproj_router(xp, o_hg_p, o_nsa_p, wo, norm2[0], rw, rb, min(256, batch * seq))
    x1_s, xn_s, route_s = _outproj_router(xs, o_hg_s, o_nsa_s, wo, norm2[0], rw, rb, bs)
    top_e = jnp.concatenate([route_p[:, :TOP_K], route_s[:, :TOP_K]], axis=0).astype(jnp.int32)
    n_tok = batch * seq + bs
    n_items = -(-n_tok * TOP_K // MOE_ROWS) + N_EXPERTS
    slots, item_e, item_rows = _routing_plan(top_e, n_items)
    slots_p, slots_s = slots[:batch * seq * TOP_K], slots[batch * seq * TOP_K:]
    xsort = jnp.zeros((n_items * MOE_ROWS, d), f32)
    xsort = _dispatch(slots_p, xn_p, xsort, min(256, batch * seq))
    xsort = _dispatch(slots_s, xn_s, xsort, bs)
    ysort = _experts(item_e, item_rows, xsort, moe_w1[0], moe_b1[0], moe_w2[0], moe_b2[0])
    y_p = _combine(slots_p, x1_p, route_p, norm_f, ysort, min(128, batch * seq))
    y_s = _combine(slots_s, x1_s, route_s, norm_f, ysort, bs)

    kv5 = lambda a, n, t: a.reshape(1, n, t, 2, KVH, DH)
    win_p = kvw_p.reshape(batch, seq, KV_WIDTH)[:, seq - min(WINDOW, seq):]
    return (y_p.reshape(batch, seq, d), y_s.reshape(bs, 1, d),
            kv5(kvc_p, batch, seq), kv5(kvs_p, batch, seq), kv5(win_p, batch, min(WINDOW, seq)), s_p[None],
            kv5(zs[:, COL_KVC:COL_KVC + KV_WIDTH], bs, 1), kv5(zs[:, COL_KVS:COL_KVS + KV_WIDTH], bs, 1),
            kv5(new_win, bs, wlen), s_s[None])
```
